```python
import jax, jax.numpy as jnp
from jax import lax
import numpy as np


D_MODEL = 1024
BATCH = 8
SEQ = 2048
DEPTH = 2

BRANCH_WIDTH = 512
N_BRANCH = 4
EPS = 1e-6
S5_GROUP = 16
S5_GROUPS = BRANCH_WIDTH // S5_GROUP
S5_STATE = 64
S5_STEP_MIN = 1e-3
S5_STEP_MAX = 1e-1
SGU_CHUNK = 128
SGU_HEADS = 8
SGU_HEAD_DIM = BRANCH_WIDTH // SGU_HEADS
M2_HEAD_DIM = 64
M2_HEADS = BRANCH_WIDTH // M2_HEAD_DIM
M2_GROUPS = 2
M2_STATE = 128
M2_CONV = 4
M2_CHUNK = 128
M2_CONV_CH = BRANCH_WIDTH + 2 * M2_GROUPS * M2_STATE
M2_DT_MIN = 1e-3
M2_DT_MAX = 1e-1
SC_CONV = 3

IN_SIZES = (
    BRANCH_WIDTH, BRANCH_WIDTH,
    BRANCH_WIDTH, BRANCH_WIDTH, BRANCH_WIDTH,
    BRANCH_WIDTH, M2_CONV_CH, M2_HEADS,
    BRANCH_WIDTH, BRANCH_WIDTH, BRANCH_WIDTH, BRANCH_WIDTH,
    N_BRANCH * D_MODEL,
)
IN_DIM = int(sum(IN_SIZES))
IN_SPLITS = [int(v) for v in np.cumsum(IN_SIZES)[:-1]]

kernel_name = 'hybrid_s5_sgu_ssd_shortconv_gated_merge'


def rmsnorm(x, w):
    x32 = x.astype(jnp.float32)
    y = x32 * lax.rsqrt(jnp.mean(x32 * x32, axis=-1, keepdims=True) + EPS)
    return (y * w.astype(jnp.float32)).astype(x.dtype)


def causal_depthwise_conv(x, w):
    k, c = w.shape
    return lax.conv_general_dilated(
        x, w[:, None, :].astype(x.dtype), window_strides=(1,), padding=[(k - 1, 0)],
        dimension_numbers=('NWC', 'WIO', 'NWC'), feature_group_count=c)


def _complex_affine_combine(e1, e2):
    a1r, a1i, b1r, b1i = e1
    a2r, a2i, b2r, b2i = e2
    ar = a1r * a2r - a1i * a2i
    ai = a1r * a2i + a1i * a2r
    br = a2r * b1r - a2i * b1i + b2r
    bi = a2r * b1i + a2i * b1r + b2i
    return (ar, ai, br, bi)


def s5_branch(u, gate, lam_re, lam_im, b_re, b_im, c_re, c_im, d, log_step, w_glu):
    bsz, seq_len, _ = u.shape
    f32 = jnp.float32
    u32 = u.astype(f32).reshape(bsz, seq_len, S5_GROUPS, S5_GROUP)
    step = jnp.exp(log_step.astype(f32))[:, None]
    lr, li = lam_re.astype(f32), lam_im.astype(f32)
    mag = jnp.exp(lr * step)
    ab_re, ab_im = mag * jnp.cos(li * step), mag * jnp.sin(li * step)
    den = lr * lr + li * li
    nr = ab_re - 1.0
    coef_re = (nr * lr + ab_im * li) / den
    coef_im = (ab_im * lr - nr * li) / den
    br, bi = b_re.astype(f32), b_im.astype(f32)
    bb_re = coef_re[..., None] * br - coef_im[..., None] * bi
    bb_im = coef_re[..., None] * bi + coef_im[..., None] * br
    bu_re = jnp.einsum('blgp,gnp->blgn', u32, bb_re)
    bu_im = jnp.einsum('blgp,gnp->blgn', u32, bb_im)
    a_re = jnp.broadcast_to(ab_re, bu_re.shape)
    a_im = jnp.broadcast_to(ab_im, bu_re.shape)
    _, _, s_re, s_im = lax.associative_scan(
        _complex_affine_combine, (a_re, a_im, bu_re, bu_im), axis=1)
    y = (jnp.einsum('blgn,gpn->blgp', s_re, c_re.astype(f32))
         - jnp.einsum('blgn,gpn->blgp', s_im, c_im.astype(f32))
         + d.astype(f32) * u32)
    y = jax.nn.gelu(y.reshape(bsz, seq_len, BRANCH_WIDTH))
    y = y * jax.nn.sigmoid(y @ w_glu.astype(f32))
    return (y * jax.nn.silu(gate.astype(f32))).astype(u.dtype)


def sgu_branch(u, v, gate, ln_w, ln_b, w_s, b_s):
    bsz, seq_len, _ = u.shape
    f32 = jnp.float32
    u32 = jax.nn.gelu(u.astype(f32))
    v32 = jax.nn.gelu(v.astype(f32))
    mu = jnp.mean(v32, axis=-1, keepdims=True)
    var = jnp.mean(jnp.square(v32 - mu), axis=-1, keepdims=True)
    vn = (v32 - mu) * lax.rsqrt(var + EPS) * ln_w.astype(f32) + ln_b.astype(f32)
    vn = vn.reshape(bsz, seq_len // SGU_CHUNK, SGU_CHUNK, SGU_HEADS, SGU_HEAD_DIM)
    mask = jnp.tril(jnp.ones((SGU_CHUNK, SGU_CHUNK), dtype=bool))
    w_m = jnp.where(mask, w_s.astype(f32), 0.0)
    s = jnp.einsum('hts,bcshe->bcthe', w_m, vn) + b_s.astype(f32).T[:, :, None]
    out = u32 * s.reshape(bsz, seq_len, BRANCH_WIDTH)
    return (out * jax.nn.silu(gate.astype(f32))).astype(u.dtype)


def segsum(a):
    t = a.shape[-1]
    cs = jnp.cumsum(a, axis=-1)
    diff = cs[..., :, None] - cs[..., None, :]
    mask = jnp.tril(jnp.ones((t, t), dtype=bool))
    return jnp.where(mask, diff, -jnp.inf)


def mamba2_branch(z, xbc, dt_raw, conv_w, conv_b, dt_bias, a_log, d, norm_w):
    bsz, seq_len, _ = z.shape
    f32 = jnp.float32
    nc, q = seq_len // M2_CHUNK, M2_CHUNK
    xbc = jax.nn.silu((causal_depthwise_conv(xbc, conv_w) + conv_b).astype(f32))
    x, bm, cm = jnp.split(xbc, [BRANCH_WIDTH, BRANCH_WIDTH + M2_GROUPS * M2_STATE], axis=-1)
    rep = M2_HEADS // M2_GROUPS
    x = x.reshape(bsz, nc, q, M2_HEADS, M2_HEAD_DIM)
    bm = jnp.repeat(bm.reshape(bsz, seq_len, M2_GROUPS, M2_STATE), rep, axis=2).reshape(bsz, nc, q, M2_HEADS, M2_STATE)
    cm = jnp.repeat(cm.reshape(bsz, seq_len, M2_GROUPS, M2_STATE), rep, axis=2).reshape(bsz, nc, q, M2_HEADS, M2_STATE)
    dt = jax.nn.softplus(dt_raw.astype(f32) + dt_bias.astype(f32))
    a = -jnp.exp(a_log.astype(f32))
    da = (dt * a).reshape(bsz, nc, q, M2_HEADS).transpose(0, 3, 1, 2)
    a_cs = jnp.cumsum(da, axis=-1)
    xdt = x * dt.reshape(bsz, nc, q, M2_HEADS)[..., None]
    scores = jnp.einsum('bclhn,bcshn->bhcls', cm, bm) * jnp.exp(segsum(da))
    y_diag = jnp.einsum('bhcls,bcshp->bclhp', scores, xdt)
    decay_states = jnp.exp(a_cs[..., -1:] - a_cs)
    states = jnp.einsum('bclhn,bhcl,bclhp->bchpn', bm, decay_states, xdt)
    states = jnp.concatenate([jnp.zeros_like(states[:, :1]), states], axis=1)
    decay_chunk = jnp.exp(segsum(jnp.pad(a_cs[..., -1], ((0, 0), (0, 0), (1, 0)))))
    states = jnp.einsum('bhzc,bchpn->bzhpn', decay_chunk, states)[:, :-1]
    y_off = jnp.einsum('bclhn,bchpn,bhcl->bclhp', cm, states, jnp.exp(a_cs))
    y = y_diag + y_off + d.astype(f32)[:, None] * x
    y = y.reshape(bsz, seq_len, BRANCH_WIDTH) * jax.nn.silu(z.astype(f32))
    y = y * lax.rsqrt(jnp.mean(y * y, axis=-1, keepdims=True) + EPS) * norm_w.astype(f32)
    return y.astype(z.dtype)


def shortconv_branch(bg, cg, h, gate, conv_w):
    y = bg * causal_depthwise_conv(cg * h, conv_w)
    return y * jax.nn.silu(gate)


def setup_inputs(seed: int = 0) -> dict:
    key = jax.random.key(seed)
    ks = jax.random.split(key, 32)
    f32 = jnp.float32
    W, D, G, N, P = BRANCH_WIDTH, D_MODEL, S5_GROUPS, S5_STATE, S5_GROUP
    nrm = lambda k, shape, s: jax.random.normal(k, shape, f32) * s
    x = jax.random.normal(ks[0], (BATCH, SEQ, D), f32)
    norm_w = 1.0 + nrm(ks[1], (DEPTH, D), 0.01)
    w_in = nrm(ks[2], (DEPTH, D, IN_DIM), D ** -0.5)
    s5_lambda_re = -0.5 + nrm(ks[3], (DEPTH, G, N), 0.01)
    s5_lambda_im = jnp.pi * jnp.arange(N, dtype=f32)[None, None, :] + nrm(ks[4], (DEPTH, G, N), 0.01)
    s5_b_re = nrm(ks[5], (DEPTH, G, N, P), (2.0 * P) ** -0.5)
    s5_b_im = nrm(ks[6], (DEPTH, G, N, P), (2.0 * P) ** -0.5)
    s5_c_re = nrm(ks[7], (DEPTH, G, P, N), (2.0 * N) ** -0.5)
    s5_c_im = nrm(ks[8], (DEPTH, G, P, N), (2.0 * N) ** -0.5)
    s5_d = nrm(ks[9], (DEPTH, G, P), 1.0)
    s5_log_step = jax.random.uniform(ks[10], (DEPTH, G), f32, np.log(S5_STEP_MIN), np.log(S5_STEP_MAX))
    s5_w_glu = nrm(ks[11], (DEPTH, W, W), W ** -0.5)
    sgu_ln_w = 1.0 + nrm(ks[12], (DEPTH, W), 0.01)
    sgu_ln_b = nrm(ks[13], (DEPTH, W), 0.01)
    sgu_w = nrm(ks[14], (DEPTH, SGU_HEADS, SGU_CHUNK, SGU_CHUNK), SGU_CHUNK ** -0.5)
    sgu_b = 1.0 + nrm(ks[15], (DEPTH, SGU_HEADS, SGU_CHUNK), 0.1)
    m2_conv_w = nrm(ks[16], (DEPTH, M2_CONV, M2_CONV_CH), M2_CONV ** -0.5)
    m2_conv_b = nrm(ks[17], (DEPTH, M2_CONV_CH), 0.01)
    dt0 = jnp.exp(jax.random.uniform(ks[18], (DEPTH, M2_HEADS), f32, np.log(M2_DT_MIN), np.log(M2_DT_MAX)))
    m2_dt_bias = dt0 + jnp.log(-jnp.expm1(-dt0))
    m2_a_log = jnp.log(jax.random.uniform(ks[19], (DEPTH, M2_HEADS), f32, 1.0, 16.0))
    m2_d = 1.0 + nrm(ks[20], (DEPTH, M2_HEADS), 0.01)
    m2_norm_w = 1.0 + nrm(ks[21], (DEPTH, W), 0.01)
    sc_conv_w = nrm(ks[22], (DEPTH, SC_CONV, W), SC_CONV ** -0.5)
    merge_b = nrm(ks[23], (DEPTH, N_BRANCH, D), 0.01)
    w_branch = nrm(ks[24], (DEPTH, N_BRANCH, W, D), W ** -0.5)
    w_out = nrm(ks[25], (DEPTH, D, D), D ** -0.5)
    final_norm_w = 1.0 + nrm(ks[26], (D,), 0.01)
    return {'x': x, 'norm_w': norm_w, 'w_in': w_in,
            's5_lambda_re': s5_lambda_re, 's5_lambda_im': s5_lambda_im,
            's5_b_re': s5_b_re, 's5_b_im': s5_b_im, 's5_c_re': s5_c_re, 's5_c_im': s5_c_im,
            's5_d': s5_d, 's5_log_step': s5_log_step, 's5_w_glu': s5_w_glu,
            'sgu_ln_w': sgu_ln_w, 'sgu_ln_b': sgu_ln_b, 'sgu_w': sgu_w, 'sgu_b': sgu_b,
            'm2_conv_w': m2_conv_w, 'm2_conv_b': m2_conv_b, 'm2_dt_bias': m2_dt_bias,
            'm2_a_log': m2_a_log, 'm2_d': m2_d, 'm2_norm_w': m2_norm_w,
            'sc_conv_w': sc_conv_w, 'merge_b': merge_b, 'w_branch': w_branch,
            'w_out': w_out, 'final_norm_w': final_norm_w}


def reference(x, norm_w, w_in, s5_lambda_re, s5_lambda_im, s5_b_re, s5_b_im, s5_c_re, s5_c_im,
              s5_d, s5_log_step, s5_w_glu, sgu_ln_w, sgu_ln_b, sgu_w, sgu_b,
              m2_conv_w, m2_conv_b, m2_dt_bias, m2_a_log, m2_d, m2_norm_w,
              sc_conv_w, merge_b, w_branch, w_out, final_norm_w):
    bsz, seq_len, _ = x.shape
    for i in range(DEPTH):
        h = rmsnorm(x, norm_w[i])
        (s5_u, s5_g, sgu_u, sgu_v, sgu_g, m2_z, m2_xbc, m2_dt,
         sc_b, sc_c, sc_h, sc_g, merge_logits) = jnp.split(h @ w_in[i], IN_SPLITS, axis=-1)
        y_a = s5_branch(s5_u, s5_g, s5_lambda_re[i], s5_lambda_im[i], s5_b_re[i], s5_b_im[i],
                        s5_c_re[i], s5_c_im[i], s5_d[i], s5_log_step[i], s5_w_glu[i])
        y_b = sgu_branch(sgu_u, sgu_v, sgu_g, sgu_ln_w[i], sgu_ln_b[i], sgu_w[i], sgu_b[i])
        y_c = mamba2_branch(m2_z, m2_xbc, m2_dt, m2_conv_w[i], m2_conv_b[i], m2_dt_bias[i],
                            m2_a_log[i], m2_d[i], m2_norm_w[i])
        y_d = shortconv_branch(sc_b, sc_c, sc_h, sc_g, sc_conv_w[i])
        branches = jnp.stack([y_a, y_b, y_c, y_d], axis=2)
        branch_out = jnp.einsum('blkw,kwd->blkd', branches, w_branch[i])
        gates = jax.nn.sigmoid(
            merge_logits.reshape(bsz, seq_len, N_BRANCH, D_MODEL).astype(jnp.float32)
            + merge_b[i].astype(jnp.float32))
        merged = jnp.einsum('blkd,blkd->bld', gates, branch_out.astype(jnp.float32)).astype(x.dtype)
        x = x + merged @ w_out[i]
    return rmsnorm(x, final_norm_w)
```

```python
import functools

import jax
import jax.numpy as jnp
import numpy as np
from jax import lax
from jax.experimental import pallas as pl
from jax.experimental.pallas import tpu as pltpu

F32 = jnp.float32
BF16 = jnp.bfloat16

D_MODEL = 1024
DEPTH = 2
WIDTH = 512
N_BRANCH = 4
EPS = 1e-6
S5_GROUP = 16
S5_GROUPS = WIDTH // S5_GROUP
S5_STATE = 64
S5_PAIRS = S5_GROUPS * S5_STATE
SGU_CHUNK = 128
SGU_HEADS = 8
SGU_HEAD_DIM = WIDTH // SGU_HEADS
M2_HEAD_DIM = 64
M2_HEADS = WIDTH // M2_HEAD_DIM
M2_GROUPS = 2
M2_STATE = 128
M2_CONV = 4
M2_CHUNK = 128
M2_CONV_CH = WIDTH + 2 * M2_GROUPS * M2_STATE
SC_CONV = 3

IN_SIZES = (WIDTH, WIDTH, WIDTH, WIDTH, WIDTH, WIDTH, M2_CONV_CH, M2_HEADS,
            WIDTH, WIDTH, WIDTH, WIDTH, N_BRANCH * D_MODEL)
IN_OFFS = tuple(int(v) for v in np.cumsum((0,) + IN_SIZES))

SUBLANES = 8
LANES = 128
VMEM_LIMIT_BYTES = 56 * 1024 * 1024

NORM_ROWS = 1024
S5_ROWS = 256
SGU_ROWS = 512
SSD_ROWS = 256
SC_ROWS = 512
MERGE_ROWS = 512


def _dot(a, b, precision=None):
    return jnp.dot(a, b, preferred_element_type=F32, precision=precision)


def _dot_nt(a, b):
    return lax.dot_general(a, b, (((1,), (1,)), ((), ())), preferred_element_type=F32)


def _dot_tn(a, b):
    return lax.dot_general(a, b, (((0,), (0,)), ((), ())), preferred_element_type=F32)


def _sigmoid(x):
    return 0.5 * (jnp.tanh(0.5 * x) + 1.0)


def _silu(x):
    return x * _sigmoid(x)


def _gelu(x):
    c = np.float32(np.sqrt(2.0 / np.pi))
    return x * (0.5 * (1.0 + jnp.tanh(c * (x + 0.044715 * (x * x * x)))))


def _softplus(x):
    return jnp.maximum(x, 0.0) + jnp.log1p(jnp.exp(-jnp.abs(x)))


def _rms(x, w):
    return x * lax.rsqrt(jnp.mean(x * x, axis=-1, keepdims=True) + EPS) * w


def _params(*semantics):
    return pltpu.CompilerParams(dimension_semantics=semantics, vmem_limit_bytes=VMEM_LIMIT_BYTES)


def _full(shape):
    zeros = (0,) * len(shape)
    return pl.BlockSpec(shape, lambda *_: zeros)


def _seq_spec(rows, cols):
    return pl.BlockSpec((None, rows, cols), lambda b, c: (b, c, 0))


def _rmsnorm_kernel(x_ref, w_ref, o_ref):
    o_ref[...] = _rms(x_ref[...], w_ref[...]).astype(o_ref.dtype)


def _rmsnorm(x2d, w):
    t, d = x2d.shape
    return pl.pallas_call(
        _rmsnorm_kernel,
        grid=(t // NORM_ROWS,),
        in_specs=[pl.BlockSpec((NORM_ROWS, d), lambda i: (i, 0)), _full((1, d))],
        out_specs=pl.BlockSpec((NORM_ROWS, d), lambda i: (i, 0)),
        out_shape=jax.ShapeDtypeStruct((t, d), BF16),
        compiler_params=_params("parallel"),
        name="rmsnorm",
    )(x2d, w.reshape(1, d))


def _s5_prep_kernel(lr_ref, li_ref, ls_ref, lrb_ref, lib_ref, lsb_ref, bre_ref, bim_ref,
                    tab_ref, bbre_ref, bbim_ref):
    lr, li = lr_ref[...], li_ref[...]
    step = jnp.exp(ls_ref[...])
    row = lax.broadcasted_iota(jnp.int32, (SUBLANES, S5_PAIRS), 0)

    def power(j):
        mag = jnp.exp(lr * step * j)
        ang = li * step * j
        return mag * jnp.cos(ang), mag * jnp.sin(ang)

    for k, d in enumerate((1, 2, 4)):
        pr, pi = power(np.float32(d))
        keep = row >= d
        tab_ref[2 * k] = jnp.where(keep, pr, 0.0)
        tab_ref[2 * k + 1] = jnp.where(keep, pi, 0.0)
    pr, pi = power((row + 1).astype(F32))
    tab_ref[6] = pr
    tab_ref[7] = pi

    lr, li = lrb_ref[...], lib_ref[...]
    step = jnp.exp(lsb_ref[...])
    mag = jnp.exp(lr * step)
    ab_re, ab_im = mag * jnp.cos(li * step), mag * jnp.sin(li * step)
    den = lr * lr + li * li
    nr = ab_re - 1.0
    coef_re = (nr * lr + ab_im * li) / den
    coef_im = (ab_im * lr - nr * li) / den
    br, bi = bre_ref[...], bim_ref[...]
    bbre_ref[...] = coef_re * br - coef_im * bi
    bbim_ref[...] = coef_re * bi + coef_im * br


def _block_diag(blocks):
    n, r, c = blocks.shape
    eye = jnp.eye(n, dtype=blocks.dtype)
    return (blocks[:, :, None, :] * eye[:, None, :, None]).reshape(n * r, n * c)


def _s5_prep(lam_re, lam_im, b_re, b_im, c_re, c_im, log_step):
    g, n, p = S5_GROUPS, S5_STATE, S5_GROUP
    ls = jnp.repeat(log_step, n)
    row = lambda v: v.reshape(1, g * n)
    col = lambda v: jnp.broadcast_to(v.reshape(g * n, 1), (g * n, p))
    tab, bb_re, bb_im = pl.pallas_call(
        _s5_prep_kernel,
        out_shape=(jax.ShapeDtypeStruct((8, SUBLANES, g * n), F32),
                   jax.ShapeDtypeStruct((g * n, p), F32),
                   jax.ShapeDtypeStruct((g * n, p), F32)),
        name="s5_prep",
    )(row(lam_re), row(lam_im), row(ls), col(lam_re), col(lam_im), col(ls),
      b_re.reshape(g * n, p), b_im.reshape(g * n, p))
    half = g // 2
    bb_re = bb_re.reshape(g, n, p).transpose(0, 2, 1)
    bb_im = bb_im.reshape(g, n, p).transpose(0, 2, 1)
    c_re_t = c_re.transpose(0, 2, 1)
    c_im_t = c_im.transpose(0, 2, 1)
    wb, cc = [], []
    for hf in range(2):
        sl = slice(hf * half, (hf + 1) * half)
        wb.append(jnp.concatenate([_block_diag(bb_re[sl]), _block_diag(bb_im[sl])], axis=1))
        cc.append(jnp.concatenate([_block_diag(c_re_t[sl]), -_block_diag(c_im_t[sl])], axis=0))
    return tab, jnp.stack(wb).astype(BF16), jnp.stack(cc).astype(BF16)


S5_HALF = S5_PAIRS
S5_SLAB = 512


def _s5_kernel(h_ref, wug_ref, wb_ref, cc_ref, d_ref, wglu_ref, tab_ref, o_ref, ug_ref, st_ref, *, rows):
    @pl.when(pl.program_id(1) == 0)
    def _():
        st_ref[0:SUBLANES, :] = jnp.zeros((SUBLANES, 2 * S5_HALF), F32)

    ug_ref[...] = _dot(h_ref[...], wug_ref[...])
    ub = ug_ref[:, 0:WIDTH].astype(BF16)
    hw = WIDTH // 2
    for hf in range(2):
        st_ref[SUBLANES:SUBLANES + rows, hf * S5_HALF:(hf + 1) * S5_HALF] = _dot(
            ub[:, hf * hw:(hf + 1) * hw], wb_ref[hf])

    pairs_half = S5_HALF // 2

    def tile_body(i, carry):
        r0 = pl.multiple_of(SUBLANES + i * SUBLANES, SUBLANES)
        for hf in range(2):
            for sl in range(pairs_half // S5_SLAB):
                cre = hf * S5_HALF + sl * S5_SLAB
                cim = cre + pairs_half
                tc = hf * pairs_half + sl * S5_SLAB
                vr = st_ref[pl.ds(r0, SUBLANES), cre:cre + S5_SLAB]
                vi = st_ref[pl.ds(r0, SUBLANES), cim:cim + S5_SLAB]
                for k, dist in enumerate((1, 2, 4)):
                    ar = tab_ref[2 * k, :, tc:tc + S5_SLAB]
                    ai = tab_ref[2 * k + 1, :, tc:tc + S5_SLAB]
                    sr = pltpu.roll(vr, dist, 0)
                    si = pltpu.roll(vi, dist, 0)
                    vr, vi = vr + (ar * sr - ai * si), vi + (ar * si + ai * sr)
                pr = st_ref[pl.ds(r0 - 1, 1), cre:cre + S5_SLAB]
                pi = st_ref[pl.ds(r0 - 1, 1), cim:cim + S5_SLAB]
                ar = tab_ref[6, :, tc:tc + S5_SLAB]
                ai = tab_ref[7, :, tc:tc + S5_SLAB]
                st_ref[pl.ds(r0, SUBLANES), cre:cre + S5_SLAB] = vr + (ar * pr - ai * pi)
                st_ref[pl.ds(r0, SUBLANES), cim:cim + S5_SLAB] = vi + (ar * pi + ai * pr)
        return carry

    lax.fori_loop(0, rows // SUBLANES, tile_body, 0)

    ys = []
    for hf in range(2):
        sb = st_ref[SUBLANES:SUBLANES + rows, hf * S5_HALF:(hf + 1) * S5_HALF].astype(BF16)
        ys.append(_dot(sb, cc_ref[hf]))
    y = jnp.concatenate(ys, axis=-1) + d_ref[...] * ug_ref[:, 0:WIDTH]
    y = _gelu(y)
    y = y * _sigmoid(_dot(y.astype(BF16), wglu_ref[...]))
    o_ref[...] = (y * _silu(ug_ref[:, WIDTH:2 * WIDTH])).astype(o_ref.dtype)
    st_ref[0:SUBLANES, :] = st_ref[rows:rows + SUBLANES, :]


def _s5_branch(h, wug, wb, cc, d, wglu, tab):
    b, l, dm = h.shape
    rows = min(S5_ROWS, l)
    return pl.pallas_call(
        functools.partial(_s5_kernel, rows=rows),
        grid=(b, l // rows),
        in_specs=[_seq_spec(rows, dm), _full(wug.shape), _full(wb.shape), _full(cc.shape),
                  _full((1, WIDTH)), _full(wglu.shape), _full(tab.shape)],
        out_specs=_seq_spec(rows, WIDTH),
        out_shape=jax.ShapeDtypeStruct((b, l, WIDTH), BF16),
        scratch_shapes=[pltpu.VMEM((rows, 2 * WIDTH), F32),
                        pltpu.VMEM((SUBLANES + rows, 2 * S5_HALF), F32)],
        compiler_params=_params("parallel", "arbitrary"),
        name="s5_branch",
    )(h, wug, wb, cc, d.reshape(1, WIDTH), wglu, tab)


def _sgu_kernel(h_ref, w_ref, lnw_ref, lnb_ref, ws_ref, bias_ref, o_ref, *, rows):
    uvg = _dot(h_ref[...], w_ref[...])
    v = _gelu(uvg[:, WIDTH:2 * WIDTH])
    mu = jnp.mean(v, axis=-1, keepdims=True)
    vc = v - mu
    var = jnp.mean(vc * vc, axis=-1, keepdims=True)
    vn = (vc * lax.rsqrt(var + EPS) * lnw_ref[...] + lnb_ref[...]).astype(BF16)
    ug = _gelu(uvg[:, 0:WIDTH]) * _silu(uvg[:, 2 * WIDTH:3 * WIDTH])

    t = SGU_CHUNK
    causal = lax.broadcasted_iota(jnp.int32, (t, t), 0) >= lax.broadcasted_iota(jnp.int32, (t, t), 1)
    first_head = lax.broadcasted_iota(jnp.int32, (t, LANES), 1) < SGU_HEAD_DIM
    wm = [jnp.where(causal, ws_ref[hd], 0.0).astype(BF16) for hd in range(SGU_HEADS)]
    zero = jnp.zeros((), BF16)
    for c in range(rows // t):
        r = slice(c * t, (c + 1) * t)
        outs = []
        for j in range(WIDTH // LANES):
            blk = vn[r, j * LANES:(j + 1) * LANES]
            outs.append(_dot(wm[2 * j], jnp.where(first_head, blk, zero))
                        + _dot(wm[2 * j + 1], jnp.where(first_head, zero, blk)))
        s = jnp.concatenate(outs, axis=-1) + bias_ref[...]
        o_ref[r, :] = (ug[r] * s).astype(o_ref.dtype)


def _sgu_branch(h, w, ln_w, ln_b, w_s, b_s):
    b, l, dm = h.shape
    rows = min(SGU_ROWS, l)
    bias = jnp.repeat(b_s.T, SGU_HEAD_DIM, axis=1)
    return pl.pallas_call(
        functools.partial(_sgu_kernel, rows=rows),
        grid=(b, l // rows),
        in_specs=[_seq_spec(rows, dm), _full(w.shape), _full((1, WIDTH)), _full((1, WIDTH)),
                  _full(w_s.shape), _full(bias.shape)],
        out_specs=_seq_spec(rows, WIDTH),
        out_shape=jax.ShapeDtypeStruct((b, l, WIDTH), BF16),
        compiler_params=_params("parallel", "parallel"),
        name="sgu_branch",
    )(h, w, ln_w.reshape(1, WIDTH), ln_b.reshape(1, WIDTH), w_s, bias)


def _ssd_kernel(h_ref, wz_ref, wxbc_ref, wdt_ref, wdtt_ref, cw_ref, cb_ref, dtb_r_ref, dtb_c_ref,
                al_r_ref, al_c_ref, d_ref, nw_ref, o_ref, ext_ref, st_ref, *, rows):
    @pl.when(pl.program_id(1) == 0)
    def _():
        ext_ref[0:SUBLANES, :] = jnp.zeros((SUBLANES, M2_CONV_CH), F32)
        st_ref[...] = jnp.zeros(st_ref.shape, F32)

    hi = lax.Precision.HIGHEST
    h = h_ref[...]
    ext_ref[SUBLANES:SUBLANES + rows, :] = _dot(h, wxbc_ref[...])
    acc = cb_ref[...]
    for k in range(M2_CONV):
        acc = acc + cw_ref[k:k + 1, :] * ext_ref[pl.ds(SUBLANES - (M2_CONV - 1) + k, rows), :]
    xbc = _silu(acc)
    ext_ref[0:SUBLANES, :] = ext_ref[rows:rows + SUBLANES, :]
    z = _dot(h, wz_ref[...])

    dt = _softplus(_dot(h, wdt_ref[...]) + dtb_r_ref[...])
    da = dt * (-jnp.exp(al_r_ref[...]))
    dt_t = _softplus(_dot_nt(wdtt_ref[...], h) + dtb_c_ref[...])
    da_t = dt_t * (-jnp.exp(al_c_ref[...]))

    q = M2_CHUNK
    hd_w = M2_HEAD_DIM
    expand = (lax.broadcasted_iota(jnp.int32, (LANES, WIDTH), 1) // hd_w
              == lax.broadcasted_iota(jnp.int32, (LANES, WIDTH), 0)).astype(F32)
    ri = lax.broadcasted_iota(jnp.int32, (q, q), 0)
    ci = lax.broadcasted_iota(jnp.int32, (q, q), 1)
    causal = ri >= ci
    tril = causal.astype(F32)
    triu = (ri <= ci).astype(F32)
    first_head = lax.broadcasted_iota(jnp.int32, (q, LANES), 1) < hd_w
    zero = jnp.zeros((), BF16)
    gw = WIDTH // M2_GROUPS

    x = xbc[:, 0:WIDTH]
    xdt = x * _dot(dt, expand, hi)
    for s in range(rows // q):
        r = slice(s * q, (s + 1) * q)
        cs_col = _dot(tril, da[r], hi)
        cs_row = _dot(da_t[:, r], triu, hi)
        cs_last = cs_col[q - 1:q, :]
        exp_cs = _dot(jnp.exp(cs_col), expand, hi)
        dec = _dot(jnp.exp(cs_last - cs_col), expand, hi)
        exp_last = _dot(jnp.broadcast_to(jnp.exp(cs_last), (SUBLANES, LANES)), expand, hi)[0:1]
        xdt_c = xdt[r]
        xdt_b = xdt_c.astype(BF16)
        xdec = (xdt_c * dec).astype(BF16)
        ys = []
        for g in range(M2_GROUPS):
            bo = WIDTH + g * M2_STATE
            co = WIDTH + M2_GROUPS * M2_STATE + g * M2_STATE
            bg = xbc[r, bo:bo + M2_STATE].astype(BF16)
            cg = xbc[r, co:co + M2_STATE].astype(BF16)
            scores = _dot_nt(cg, bg)
            state = st_ref[g]
            y_g = _dot(cg, state.astype(BF16)) * exp_cs[:, g * gw:(g + 1) * gw]
            diag = []
            for jp in range(gw // LANES):
                blk = xdt_b[:, g * gw + jp * LANES:g * gw + (jp + 1) * LANES]
                acc2 = None
                for qq in range(2):
                    hd = g * (M2_HEADS // M2_GROUPS) + jp * 2 + qq
                    seg = cs_col[:, hd:hd + 1] - cs_row[hd:hd + 1, :]
                    decay = jnp.exp(jnp.where(causal, seg, -jnp.inf))
                    m = (scores * decay).astype(BF16)
                    part = jnp.where(first_head, blk, zero) if qq == 0 else jnp.where(first_head, zero, blk)
                    term = _dot(m, part)
                    acc2 = term if acc2 is None else acc2 + term
                diag.append(acc2)
            ys.append(y_g + jnp.concatenate(diag, axis=-1))
            st_ref[g] = state * exp_last[:, g * gw:(g + 1) * gw] + _dot_tn(bg, xdec[:, g * gw:(g + 1) * gw])
        y = jnp.concatenate(ys, axis=-1) + d_ref[...] * x[r]
        y = y * _silu(z[r])
        o_ref[r, :] = _rms(y, nw_ref[...]).astype(o_ref.dtype)


def _ssd_branch(h, wz, wxbc, wdt, conv_w, conv_b, dt_bias, a_log, d, norm_w):
    b, l, dm = h.shape
    rows = min(SSD_ROWS, l)
    nh = M2_HEADS
    wdt_pad = jnp.zeros((dm, LANES), BF16).at[:, 0:nh].set(wdt)
    pad_row = lambda v: jnp.zeros((1, LANES), F32).at[0, 0:nh].set(v)
    args = (h, wz, wxbc, wdt_pad, wdt.T, conv_w, conv_b.reshape(1, -1), pad_row(dt_bias),
            dt_bias.reshape(nh, 1), pad_row(a_log), a_log.reshape(nh, 1),
            jnp.repeat(d, M2_HEAD_DIM).reshape(1, WIDTH), norm_w.reshape(1, WIDTH))
    return pl.pallas_call(
        functools.partial(_ssd_kernel, rows=rows),
        grid=(b, l // rows),
        in_specs=[_seq_spec(rows, dm)] + [_full(a.shape) for a in args[1:]],
        out_specs=_seq_spec(rows, WIDTH),
        out_shape=jax.ShapeDtypeStruct((b, l, WIDTH), BF16),
        scratch_shapes=[pltpu.VMEM((SUBLANES + rows, M2_CONV_CH), F32),
                        pltpu.VMEM((M2_GROUPS, M2_STATE, WIDTH // M2_GROUPS), F32)],
        compiler_params=_params("parallel", "arbitrary"),
        name="ssd_branch",
    )(*args)


def _sc_kernel(h_ref, w_ref, cw_ref, o_ref, ext_ref, *, rows):
    @pl.when(pl.program_id(1) == 0)
    def _():
        ext_ref[0:SUBLANES, :] = jnp.zeros((SUBLANES, WIDTH), F32)

    proj = _dot(h_ref[...], w_ref[...])
    ext_ref[SUBLANES:SUBLANES + rows, :] = proj[:, WIDTH:2 * WIDTH] * proj[:, 2 * WIDTH:3 * WIDTH]
    acc = None
    for k in range(SC_CONV):
        term = cw_ref[k:k + 1, :] * ext_ref[pl.ds(SUBLANES - (SC_CONV - 1) + k, rows), :]
        acc = term if acc is None else acc + term
    ext_ref[0:SUBLANES, :] = ext_ref[rows:rows + SUBLANES, :]
    o_ref[...] = (proj[:, 0:WIDTH] * acc * _silu(proj[:, 3 * WIDTH:4 * WIDTH])).astype(o_ref.dtype)


def _sc_branch(h, w, conv_w):
    b, l, dm = h.shape
    rows = min(SC_ROWS, l)
    return pl.pallas_call(
        functools.partial(_sc_kernel, rows=rows),
        grid=(b, l // rows),
        in_specs=[_seq_spec(rows, dm), _full(w.shape), _full(conv_w.shape)],
        out_specs=_seq_spec(rows, WIDTH),
        out_shape=jax.ShapeDtypeStruct((b, l, WIDTH), BF16),
        scratch_shapes=[pltpu.VMEM((SUBLANES + rows, WIDTH), F32)],
        compiler_params=_params("parallel", "arbitrary"),
        name="sc_branch",
    )(h, w, conv_w)


def _merge_kernel(x_ref, h_ref, ya_ref, yb_ref, yc_ref, yd_ref, wm_ref, mb_ref, wbr_ref, wo_ref, nw_ref,
                  xo_ref, no_ref):
    h = h_ref[...]
    merged = None
    for k, y_ref in enumerate((ya_ref, yb_ref, yc_ref, yd_ref)):
        gate = _sigmoid(_dot(h, wm_ref[k]) + mb_ref[k:k + 1, :])
        term = gate * _dot(y_ref[...], wbr_ref[k])
        merged = term if merged is None else merged + term
    xn = x_ref[...] + _dot(merged.astype(BF16), wo_ref[...])
    xo_ref[...] = xn
    no_ref[...] = _rms(xn, nw_ref[...]).astype(no_ref.dtype)


def _merge(x2d, h2d, ys, wm, mb, wbr, wo, next_norm_w, norm_dtype):
    t, d = x2d.shape
    tile = lambda cols: pl.BlockSpec((MERGE_ROWS, cols), lambda i: (i, 0))
    return pl.pallas_call(
        _merge_kernel,
        grid=(t // MERGE_ROWS,),
        in_specs=[tile(d), tile(d)] + [tile(WIDTH)] * N_BRANCH
                 + [_full(wm.shape), _full(mb.shape), _full(wbr.shape), _full(wo.shape), _full((1, d))],
        out_specs=(tile(d), tile(d)),
        out_shape=(jax.ShapeDtypeStruct((t, d), F32), jax.ShapeDtypeStruct((t, d), norm_dtype)),
        compiler_params=_params("parallel"),
        name="merge",
    )(x2d, h2d, *ys, wm, mb, wbr, wo, next_norm_w.reshape(1, d))


def kernel(x, norm_w, w_in, s5_lambda_re, s5_lambda_im, s5_b_re, s5_b_im, s5_c_re, s5_c_im, s5_d,
           s5_log_step, s5_w_glu, sgu_ln_w, sgu_ln_b, sgu_w, sgu_b, m2_conv_w, m2_conv_b, m2_dt_bias,
           m2_a_log, m2_d, m2_norm_w, sc_conv_w, merge_b, w_branch, w_out, final_norm_w):
    b, l, d = x.shape
    t = b * l
    x2d = x.reshape(t, d)
    h2d = _rmsnorm(x2d, norm_w[0])
    out = None
    for i in range(DEPTH):
        win = w_in[i].astype(BF16)
        cols = lambda first, last: win[:, IN_OFFS[first]:IN_OFFS[last]]
        h = h2d.reshape(b, l, d)
        tab, wb, cc = _s5_prep(s5_lambda_re[i], s5_lambda_im[i], s5_b_re[i], s5_b_im[i],
                               s5_c_re[i], s5_c_im[i], s5_log_step[i])
        ya = _s5_branch(h, cols(0, 2), wb, cc, s5_d[i].reshape(-1), s5_w_glu[i].astype(BF16), tab)
        yb = _sgu_branch(h, cols(2, 5), sgu_ln_w[i], sgu_ln_b[i], sgu_w[i], sgu_b[i])
        yc = _ssd_branch(h, cols(5, 6), cols(6, 7), cols(7, 8), m2_conv_w[i], m2_conv_b[i],
                         m2_dt_bias[i], m2_a_log[i], m2_d[i], m2_norm_w[i])
        yd = _sc_branch(h, cols(8, 12), sc_conv_w[i])
        wm = cols(12, 13).reshape(d, N_BRANCH, d).transpose(1, 0, 2)
        last = i == DEPTH - 1
        x2d, h2d = _merge(x2d, h2d, [y.reshape(t, WIDTH) for y in (ya, yb, yc, yd)], wm, merge_b[i],
                          w_branch[i].astype(BF16), w_out[i].astype(BF16),
                          final_norm_w if last else norm_w[i + 1], F32 if last else BF16)
        out = h2d
    return out.reshape(b, l, d)
```

```python
import functools

import jax
import jax.numpy as jnp
import numpy as np
from jax import lax
from jax.experimental import pallas as pl
from jax.experimental.pallas import tpu as pltpu

F32 = jnp.float32
BF16 = jnp.bfloat16

D_MODEL = 1024
DEPTH = 2
WIDTH = 512
N_BRANCH = 4
EPS = 1e-6
S5_GROUP = 16
S5_GROUPS = WIDTH // S5_GROUP
S5_STATE = 64
S5_PAIRS = S5_GROUPS * S5_STATE
SGU_CHUNK = 128
SGU_HEADS = 8
SGU_HEAD_DIM = WIDTH // SGU_HEADS
M2_HEAD_DIM = 64
M2_HEADS = WIDTH // M2_HEAD_DIM
M2_GROUPS = 2
M2_STATE = 128
M2_CONV = 4
M2_CHUNK = 128
M2_CONV_CH = WIDTH + 2 * M2_GROUPS * M2_STATE
SC_CONV = 3

IN_SIZES = (WIDTH, WIDTH, WIDTH, WIDTH, WIDTH, WIDTH, M2_CONV_CH, M2_HEADS,
            WIDTH, WIDTH, WIDTH, WIDTH, N_BRANCH * D_MODEL)
IN_OFFS = tuple(int(v) for v in np.cumsum((0,) + IN_SIZES))

SUBLANES = 8
LANES = 128
MXU_DIM = 256
VMEM_LIMIT_BYTES = 56 * 1024 * 1024

NORM_ROWS = 1024
S5_STEPS = 64
SGU_ROWS = 512
SSD_ROWS = 256
SC_ROWS = 512
MERGE_ROWS = 512


def _dot(a, b):
    return jnp.dot(a, b, preferred_element_type=F32)


def _dot_nt(a, b):
    return lax.dot_general(a, b, (((1,), (1,)), ((), ())), preferred_element_type=F32)


def _dot_tn(a, b):
    return lax.dot_general(a, b, (((0,), (0,)), ((), ())), preferred_element_type=F32)


def _sigmoid(x):
    return 0.5 * (jnp.tanh(0.5 * x) + 1.0)


def _silu(x):
    return x * _sigmoid(x)


def _gelu(x):
    c = np.float32(np.sqrt(2.0 / np.pi))
    return x * (0.5 * (1.0 + jnp.tanh(c * (x + 0.044715 * (x * x * x)))))


def _softplus(x):
    return jnp.maximum(x, 0.0) + jnp.log1p(jnp.exp(-jnp.abs(x)))


def _rms(x, w):
    return x * lax.rsqrt(jnp.mean(x * x, axis=-1, keepdims=True) + EPS) * w


def _params(*semantics):
    return pltpu.CompilerParams(dimension_semantics=semantics, vmem_limit_bytes=VMEM_LIMIT_BYTES)


def _full(shape):
    zeros = (0,) * len(shape)
    return pl.BlockSpec(shape, lambda *_: zeros)


def _seq_spec(rows, cols):
    return pl.BlockSpec((None, rows, cols), lambda b, c: (b, c, 0))


def _rmsnorm_kernel(x_ref, w_ref, o_ref):
    o_ref[...] = _rms(x_ref[...], w_ref[...]).astype(o_ref.dtype)


def _rmsnorm(x2d, w):
    t, d = x2d.shape
    return pl.pallas_call(
        _rmsnorm_kernel,
        grid=(t // NORM_ROWS,),
        in_specs=[pl.BlockSpec((NORM_ROWS, d), lambda i: (i, 0)), _full((1, d))],
        out_specs=pl.BlockSpec((NORM_ROWS, d), lambda i: (i, 0)),
        out_shape=jax.ShapeDtypeStruct((t, d), BF16),
        compiler_params=_params("parallel"),
        name="rmsnorm",
    )(x2d, w.reshape(1, d))


def _s5_prep_kernel(lr_ref, li_ref, ls_ref, lrb_ref, lib_ref, lsb_ref, bre_ref, bim_ref,
                    ab_ref, bbre_ref, bbim_ref):
    step = jnp.exp(ls_ref[...])
    mag = jnp.exp(lr_ref[...] * step)
    ang = li_ref[...] * step
    ab_ref[0:1, :] = mag * jnp.cos(ang)
    ab_ref[1:2, :] = mag * jnp.sin(ang)

    lr, li = lrb_ref[...], lib_ref[...]
    step = jnp.exp(lsb_ref[...])
    mag = jnp.exp(lr * step)
    ab_re, ab_im = mag * jnp.cos(li * step), mag * jnp.sin(li * step)
    den = lr * lr + li * li
    nr = ab_re - 1.0
    coef_re = (nr * lr + ab_im * li) / den
    coef_im = (ab_im * lr - nr * li) / den
    br, bi = bre_ref[...], bim_ref[...]
    bbre_ref[...] = coef_re * br - coef_im * bi
    bbim_ref[...] = coef_re * bi + coef_im * br


def _block_diag(blocks):
    n, r, c = blocks.shape
    eye = jnp.eye(n, dtype=blocks.dtype)
    return (blocks[:, :, None, :] * eye[:, None, :, None]).reshape(n * r, n * c)


def _s5_prep(lam_re, lam_im, b_re, b_im, c_re, c_im, log_step):
    g, n, p = S5_GROUPS, S5_STATE, S5_GROUP
    ls = jnp.repeat(log_step, n)
    row = lambda v: v.reshape(1, g * n)
    col = lambda v: jnp.broadcast_to(v.reshape(g * n, 1), (g * n, p))
    ab, bb_re, bb_im = pl.pallas_call(
        _s5_prep_kernel,
        out_shape=(jax.ShapeDtypeStruct((2, g * n), F32),
                   jax.ShapeDtypeStruct((g * n, p), F32),
                   jax.ShapeDtypeStruct((g * n, p), F32)),
        name="s5_prep",
    )(row(lam_re), row(lam_im), row(ls), col(lam_re), col(lam_im), col(ls),
      b_re.reshape(g * n, p), b_im.reshape(g * n, p))
    half = g // 2
    bb_re = bb_re.reshape(g, n, p).transpose(0, 2, 1)
    bb_im = bb_im.reshape(g, n, p).transpose(0, 2, 1)
    c_re_t = c_re.transpose(0, 2, 1)
    c_im_t = c_im.transpose(0, 2, 1)
    wb, cc = [], []
    for hf in range(2):
        sl = slice(hf * half, (hf + 1) * half)
        wb.append(jnp.concatenate([_block_diag(bb_re[sl]), _block_diag(bb_im[sl])], axis=1))
        cc.append(jnp.concatenate([_block_diag(c_re_t[sl]), -_block_diag(c_im_t[sl])], axis=0))
    return ab, jnp.stack(wb).astype(BF16), jnp.stack(cc).astype(BF16)


S5_HALF = S5_PAIRS
S5_PERM = MXU_DIM


def _perm_time_major(nb):
    tt = S5_PERM // nb
    i = lax.broadcasted_iota(jnp.int32, (S5_PERM, S5_PERM), 0)
    j = lax.broadcasted_iota(jnp.int32, (S5_PERM, S5_PERM), 1)
    return jnp.where(j == (i % nb) * tt + i // nb, 1.0, 0.0).astype(BF16)


def _s5_kernel(h_ref, wug_ref, wb_ref, cc_ref, d_ref, wglu_ref, ab_ref, o_ref,
               hp_ref, ug_ref, st_ref, carry_ref, *, steps, nb):
    @pl.when(pl.program_id(0) == 0)
    def _():
        carry_ref[...] = jnp.zeros(carry_ref.shape, F32)

    rows = steps * nb
    tt = S5_PERM // nb
    perm = _perm_time_major(nb)
    for k in range(steps // tt):
        blk = h_ref[:, k * tt:(k + 1) * tt, :].reshape(S5_PERM, D_MODEL)
        hp_ref[k * S5_PERM:(k + 1) * S5_PERM, :] = _dot(perm, blk).astype(BF16)

    ug_ref[...] = _dot(hp_ref[...], wug_ref[...])
    ub = ug_ref[:, 0:WIDTH].astype(BF16)
    hw = WIDTH // 2
    for hf in range(2):
        st_ref[:, hf * S5_HALF:(hf + 1) * S5_HALF] = _dot(ub[:, hf * hw:(hf + 1) * hw], wb_ref[hf])

    pairs_half = S5_HALF // 2
    for hf in range(2):
        cre = hf * S5_HALF
        cim = cre + pairs_half
        tc = hf * pairs_half
        ar = jnp.broadcast_to(ab_ref[0:1, tc:tc + pairs_half], (nb, pairs_half))
        ai = jnp.broadcast_to(ab_ref[1:2, tc:tc + pairs_half], (nb, pairs_half))

        def step(t, carry, cre=cre, cim=cim, ar=ar, ai=ai):
            sr, si = carry
            r0 = pl.multiple_of(t * nb, nb)
            nr = (ar * sr - ai * si) + st_ref[pl.ds(r0, nb), cre:cre + pairs_half]
            ni = (ar * si + ai * sr) + st_ref[pl.ds(r0, nb), cim:cim + pairs_half]
            st_ref[pl.ds(r0, nb), cre:cre + pairs_half] = nr
            st_ref[pl.ds(r0, nb), cim:cim + pairs_half] = ni
            return nr, ni

        sr, si = lax.fori_loop(
            0, steps, step,
            (carry_ref[:, cre:cre + pairs_half], carry_ref[:, cim:cim + pairs_half]), unroll=4)
        carry_ref[:, cre:cre + pairs_half] = sr
        carry_ref[:, cim:cim + pairs_half] = si

    ys = []
    for hf in range(2):
        ys.append(_dot(st_ref[:, hf * S5_HALF:(hf + 1) * S5_HALF].astype(BF16), cc_ref[hf]))
    y = jnp.concatenate(ys, axis=-1) + d_ref[...] * ug_ref[:, 0:WIDTH]
    y = _gelu(y)
    y = y * _sigmoid(_dot(y.astype(BF16), wglu_ref[...]))
    y = (y * _silu(ug_ref[:, WIDTH:2 * WIDTH])).astype(BF16)
    for k in range(steps // tt):
        back = _dot_tn(perm, y[k * S5_PERM:(k + 1) * S5_PERM, :])
        o_ref[:, k * tt:(k + 1) * tt, :] = back.reshape(nb, tt, WIDTH).astype(o_ref.dtype)


def _s5_branch(h, wug, wb, cc, d, wglu, ab):
    b, l, dm = h.shape
    assert S5_PERM % b == 0 and b % SUBLANES == 0
    steps = min(S5_STEPS, l)
    rows = steps * b
    return pl.pallas_call(
        functools.partial(_s5_kernel, steps=steps, nb=b),
        grid=(l // steps,),
        in_specs=[pl.BlockSpec((b, steps, dm), lambda c: (0, c, 0)), _full(wug.shape), _full(wb.shape),
                  _full(cc.shape), _full((1, WIDTH)), _full(wglu.shape), _full(ab.shape)],
        out_specs=pl.BlockSpec((b, steps, WIDTH), lambda c: (0, c, 0)),
        out_shape=jax.ShapeDtypeStruct((b, l, WIDTH), BF16),
        scratch_shapes=[pltpu.VMEM((rows, dm), BF16),
                        pltpu.VMEM((rows, 2 * WIDTH), F32),
                        pltpu.VMEM((rows, 2 * S5_HALF), F32),
                        pltpu.VMEM((b, 2 * S5_HALF), F32)],
        compiler_params=_params("arbitrary"),
        name="s5_branch",
    )(h, wug, wb, cc, d.reshape(1, WIDTH), wglu, ab)


def _sgu_kernel(h_ref, w_ref, lnw_ref, lnb_ref, ws_ref, bias_ref, o_ref, *, rows):
    uvg = _dot(h_ref[...], w_ref[...])
    v = _gelu(uvg[:, WIDTH:2 * WIDTH])
    mu = jnp.mean(v, axis=-1, keepdims=True)
    vc = v - mu
    var = jnp.mean(vc * vc, axis=-1, keepdims=True)
    vn = (vc * lax.rsqrt(var + EPS) * lnw_ref[...] + lnb_ref[...]).astype(BF16)
    ug = _gelu(uvg[:, 0:WIDTH]) * _silu(uvg[:, 2 * WIDTH:3 * WIDTH])

    t = SGU_CHUNK
    causal = lax.broadcasted_iota(jnp.int32, (t, t), 0) >= lax.broadcasted_iota(jnp.int32, (t, t), 1)
    first_head = lax.broadcasted_iota(jnp.int32, (t, LANES), 1) < SGU_HEAD_DIM
    wm = [jnp.where(causal, ws_ref[hd], 0.0).astype(BF16) for hd in range(SGU_HEADS)]
    zero = jnp.zeros((), BF16)
    for c in range(rows // t):
        r = slice(c * t, (c + 1) * t)
        outs = []
        for j in range(WIDTH // LANES):
            blk = vn[r, j * LANES:(j + 1) * LANES]
            outs.append(_dot(wm[2 * j], jnp.where(first_head, blk, zero))
                        + _dot(wm[2 * j + 1], jnp.where(first_head, zero, blk)))
        s = jnp.concatenate(outs, axis=-1) + bias_ref[...]
        o_ref[r, :] = (ug[r] * s).astype(o_ref.dtype)


def _sgu_branch(h, w, ln_w, ln_b, w_s, b_s):
    b, l, dm = h.shape
    rows = min(SGU_ROWS, l)
    bias = jnp.repeat(b_s.T, SGU_HEAD_DIM, axis=1)
    return pl.pallas_call(
        functools.partial(_sgu_kernel, rows=rows),
        grid=(b, l // rows),
        in_specs=[_seq_spec(rows, dm), _full(w.shape), _full((1, WIDTH)), _full((1, WIDTH)),
                  _full(w_s.shape), _full(bias.shape)],
        out_specs=_seq_spec(rows, WIDTH),
        out_shape=jax.ShapeDtypeStruct((b, l, WIDTH), BF16),
        compiler_params=_params("parallel", "parallel"),
        name="sgu_branch",
    )(h, w, ln_w.reshape(1, WIDTH), ln_b.reshape(1, WIDTH), w_s, bias)


SSD_QUANT = 3
SSD_PARTS = 3
SSD_LANE_GROUPS = SSD_QUANT * SSD_PARTS


def _ssd_expand_matrix():
    e = np.zeros((LANES, SSD_QUANT * WIDTH), np.float32)
    for grp in range(SSD_LANE_GROUPS):
        for hd in range(M2_HEADS):
            c0 = (grp // SSD_PARTS) * WIDTH + hd * M2_HEAD_DIM
            e[grp * M2_HEADS + hd, c0:c0 + M2_HEAD_DIM] = 1.0
    return jnp.asarray(e, BF16)


def _cumsum_rows(x):
    q = x.shape[0]
    row = lax.broadcasted_iota(jnp.int32, x.shape, 0)
    dist = 1
    while dist < q:
        x = x + jnp.where(row >= dist, pltpu.roll(x, dist, 0), 0.0)
        dist *= 2
    return x


def _ssd_kernel(h_ref, wz_ref, wxbc_ref, wdt_ref, cw_ref, cb_ref, dtb_ref, al_ref, e_ref, d_ref, nw_ref,
                o_ref, prev_ref, st_ref, *, rows):
    @pl.when(pl.program_id(1) == 0)
    def _():
        prev_ref[...] = jnp.zeros(prev_ref.shape, F32)
        st_ref[...] = jnp.zeros(st_ref.shape, F32)

    h = h_ref[...]
    raw = _dot(h, wxbc_ref[...])
    cat = jnp.concatenate([prev_ref[...], raw], axis=0)
    acc = cb_ref[...] + cw_ref[M2_CONV - 1:M2_CONV, :] * raw
    for j in range(1, M2_CONV):
        k = M2_CONV - 1 - j
        acc = acc + cw_ref[k:k + 1, :] * pltpu.roll(cat, j, 0)[SUBLANES:, :]
    prev_ref[...] = raw[rows - SUBLANES:rows, :]
    xbc = _silu(acc)
    z = _dot(h, wz_ref[...])

    dt = _softplus(_dot(h, wdt_ref[...]) + dtb_ref[...])
    da = dt * (-jnp.exp(al_ref[...]))
    grp = lax.broadcasted_iota(jnp.int32, (1, LANES), 1) // M2_HEADS
    quant = grp // SSD_PARTS
    part = grp % SSD_PARTS

    q = M2_CHUNK
    hd_w = M2_HEAD_DIM
    causal = lax.broadcasted_iota(jnp.int32, (q, q), 0) >= lax.broadcasted_iota(jnp.int32, (q, q), 1)
    first_head = lax.broadcasted_iota(jnp.int32, (q, LANES), 1) < hd_w
    zero = jnp.zeros((), BF16)
    gw = WIDTH // M2_GROUPS

    x = xbc[:, 0:WIDTH]
    for s in range(rows // q):
        r = slice(s * q, (s + 1) * q)
        cs = _cumsum_rows(da[r])
        cs_row = cs.T[0:M2_HEADS, :]
        cs_last = cs[q - 1:q, :]
        val = jnp.where(quant == 0, dt[r], jnp.where(quant == 1, dt[r] * jnp.exp(cs_last - cs), jnp.exp(cs)))
        p0 = val.astype(BF16)
        r1 = val - p0.astype(F32)
        p1 = r1.astype(BF16)
        p2 = (r1 - p1.astype(F32)).astype(BF16)
        spread = _dot(jnp.where(part == 0, p0, jnp.where(part == 1, p1, p2)), e_ref[...])
        dt_e = spread[:, 0:WIDTH]
        dtdec_e = spread[:, WIDTH:2 * WIDTH]
        exp_cs = spread[:, 2 * WIDTH:3 * WIDTH]
        exp_last = exp_cs[q - 1:q, :]
        x_c = x[r]
        xdt_b = (x_c * dt_e).astype(BF16)
        xdec = (x_c * dtdec_e).astype(BF16)
        ys = []
        for g in range(M2_GROUPS):
            bo = WIDTH + g * M2_STATE
            co = WIDTH + M2_GROUPS * M2_STATE + g * M2_STATE
            bg = xbc[r, bo:bo + M2_STATE].astype(BF16)
            cg = xbc[r, co:co + M2_STATE].astype(BF16)
            scores = _dot_nt(cg, bg)
            state = st_ref[g]
            y_g = _dot(cg, state.astype(BF16)) * exp_cs[:, g * gw:(g + 1) * gw]
            diag = []
            for jp in range(gw // LANES):
                blk = xdt_b[:, g * gw + jp * LANES:g * gw + (jp + 1) * LANES]
                acc2 = None
                for qq in range(2):
                    hd = g * (M2_HEADS // M2_GROUPS) + jp * 2 + qq
                    seg = cs[:, hd:hd + 1] - cs_row[hd:hd + 1, :]
                    decay = jnp.exp(jnp.where(causal, seg, -jnp.inf))
                    m = (scores * decay).astype(BF16)
                    half = jnp.where(first_head, blk, zero) if qq == 0 else jnp.where(first_head, zero, blk)
                    term = _dot(m, half)
                    acc2 = term if acc2 is None else acc2 + term
                diag.append(acc2)
            ys.append(y_g + jnp.concatenate(diag, axis=-1))
            st_ref[g] = state * exp_last[:, g * gw:(g + 1) * gw] + _dot_tn(bg, xdec[:, g * gw:(g + 1) * gw])
        y = jnp.concatenate(ys, axis=-1) + d_ref[...] * x_c
        y = y * _silu(z[r])
        o_ref[r, :] = _rms(y, nw_ref[...]).astype(o_ref.dtype)


def _ssd_branch(h, wz, wxbc, wdt, conv_w, conv_b, dt_bias, a_log, d, norm_w):
    b, l, dm = h.shape
    rows = min(SSD_ROWS, l)
    used = SSD_LANE_GROUPS * M2_HEADS
    group_row = lambda v: jnp.zeros((1, LANES), F32).at[0, 0:used].set(jnp.tile(v, SSD_LANE_GROUPS))
    wdt_groups = jnp.zeros((dm, LANES), BF16).at[:, 0:used].set(jnp.tile(wdt, (1, SSD_LANE_GROUPS)))
    args = (h, wz, wxbc, wdt_groups, conv_w, conv_b.reshape(1, -1), group_row(dt_bias), group_row(a_log),
            _ssd_expand_matrix(), jnp.repeat(d, M2_HEAD_DIM).reshape(1, WIDTH), norm_w.reshape(1, WIDTH))
    return pl.pallas_call(
        functools.partial(_ssd_kernel, rows=rows),
        grid=(b, l // rows),
        in_specs=[_seq_spec(rows, dm)] + [_full(a.shape) for a in args[1:]],
        out_specs=_seq_spec(rows, WIDTH),
        out_shape=jax.ShapeDtypeStruct((b, l, WIDTH), BF16),
        scratch_shapes=[pltpu.VMEM((SUBLANES, M2_CONV_CH), F32),
                        pltpu.VMEM((M2_GROUPS, M2_STATE, WIDTH // M2_GROUPS), F32)],
        compiler_params=_params("parallel", "arbitrary"),
        name="ssd_branch",
    )(*args)


def _sc_kernel(h_ref, w_ref, cw_ref, o_ref, prev_ref, *, rows):
    @pl.when(pl.program_id(1) == 0)
    def _():
        prev_ref[...] = jnp.zeros(prev_ref.shape, F32)

    proj = _dot(h_ref[...], w_ref[...])
    v = proj[:, WIDTH:2 * WIDTH] * proj[:, 2 * WIDTH:3 * WIDTH]
    cat = jnp.concatenate([prev_ref[...], v], axis=0)
    acc = cw_ref[SC_CONV - 1:SC_CONV, :] * v
    for j in range(1, SC_CONV):
        k = SC_CONV - 1 - j
        acc = acc + cw_ref[k:k + 1, :] * pltpu.roll(cat, j, 0)[SUBLANES:, :]
    prev_ref[...] = v[rows - SUBLANES:rows, :]
    o_ref[...] = (proj[:, 0:WIDTH] * acc * _silu(proj[:, 3 * WIDTH:4 * WIDTH])).astype(o_ref.dtype)


def _sc_branch(h, w, conv_w):
    b, l, dm = h.shape
    rows = min(SC_ROWS, l)
    return pl.pallas_call(
        functools.partial(_sc_kernel, rows=rows),
        grid=(b, l // rows),
        in_specs=[_seq_spec(rows, dm), _full(w.shape), _full(conv_w.shape)],
        out_specs=_seq_spec(rows, WIDTH),
        out_shape=jax.ShapeDtypeStruct((b, l, WIDTH), BF16),
        scratch_shapes=[pltpu.VMEM((SUBLANES, WIDTH), F32)],
        compiler_params=_params("parallel", "arbitrary"),
        name="sc_branch",
    )(h, w, conv_w)


def _merge_kernel(x_ref, h_ref, ya_ref, yb_ref, yc_ref, yd_ref, wm_ref, mb_ref, wbr_ref, wo_ref, nw_ref,
                  xo_ref, no_ref):
    h = h_ref[...]
    d = D_MODEL
    merged = None
    for k, y_ref in enumerate((ya_ref, yb_ref, yc_ref, yd_ref)):
        gate = _sigmoid(_dot(h, wm_ref[:, k * d:(k + 1) * d]) + mb_ref[k:k + 1, :])
        term = gate * _dot(y_ref[...], wbr_ref[k])
        merged = term if merged is None else merged + term
    xn = x_ref[...] + _dot(merged.astype(BF16), wo_ref[...])
    xo_ref[...] = xn
    no_ref[...] = _rms(xn, nw_ref[...]).astype(no_ref.dtype)


def _merge(x2d, h2d, ys, wm, mb, wbr, wo, next_norm_w, norm_dtype):
    t, d = x2d.shape
    tile = lambda cols: pl.BlockSpec((MERGE_ROWS, cols), lambda i: (i, 0))
    return pl.pallas_call(
        _merge_kernel,
        grid=(t // MERGE_ROWS,),
        in_specs=[tile(d), tile(d)] + [tile(WIDTH)] * N_BRANCH
                 + [_full(wm.shape), _full(mb.shape), _full(wbr.shape), _full(wo.shape), _full((1, d))],
        out_specs=(tile(d), tile(d)),
        out_shape=(jax.ShapeDtypeStruct((t, d), F32), jax.ShapeDtypeStruct((t, d), norm_dtype)),
        compiler_params=_params("parallel"),
        name="merge",
    )(x2d, h2d, *ys, wm, mb, wbr, wo, next_norm_w.reshape(1, d))


def kernel(x, norm_w, w_in, s5_lambda_re, s5_lambda_im, s5_b_re, s5_b_im, s5_c_re, s5_c_im, s5_d,
           s5_log_step, s5_w_glu, sgu_ln_w, sgu_ln_b, sgu_w, sgu_b, m2_conv_w, m2_conv_b, m2_dt_bias,
           m2_a_log, m2_d, m2_norm_w, sc_conv_w, merge_b, w_branch, w_out, final_norm_w):
    b, l, d = x.shape
    t = b * l
    x2d = x.reshape(t, d)
    h2d = _rmsnorm(x2d, norm_w[0])
    out = None
    for i in range(DEPTH):
        cols = lambda first, last: w_in[i, :, IN_OFFS[first]:IN_OFFS[last]].astype(BF16)
        h = h2d.reshape(b, l, d)
        ab, wb, cc = _s5_prep(s5_lambda_re[i], s5_lambda_im[i], s5_b_re[i], s5_b_im[i],
                              s5_c_re[i], s5_c_im[i], s5_log_step[i])
        ya = _s5_branch(h, cols(0, 2), wb, cc, s5_d[i].reshape(-1), s5_w_glu[i].astype(BF16), ab)
        yb = _sgu_branch(h, cols(2, 5), sgu_ln_w[i], sgu_ln_b[i], sgu_w[i], sgu_b[i])
        yc = _ssd_branch(h, cols(5, 6), cols(6, 7), cols(7, 8), m2_conv_w[i], m2_conv_b[i],
                         m2_dt_bias[i], m2_a_log[i], m2_d[i], m2_norm_w[i])
        yd = _sc_branch(h, cols(8, 12), sc_conv_w[i])
        last = i == DEPTH - 1
        x2d, h2d = _merge(x2d, h2d, [y.reshape(t, WIDTH) for y in (ya, yb, yc, yd)], cols(12, 13),
                          merge_b[i], w_branch[i].astype(BF16), w_out[i].astype(BF16),
                          final_norm_w if last else norm_w[i + 1], F32 if last else BF16)
        out = h2d
    return out.reshape(b, l, d)
```

```python
import functools

import jax
import jax.numpy as jnp
import numpy as np
from jax import lax
from jax.experimental import pallas as pl
from jax.experimental.pallas import tpu as pltpu

F32 = jnp.float32
BF16 = jnp.bfloat16

D_MODEL = 1024
DEPTH = 2
WIDTH = 512
N_BRANCH = 4
EPS = 1e-6
S5_GROUP = 16
S5_GROUPS = WIDTH // S5_GROUP
S5_STATE = 64
S5_PAIRS = S5_GROUPS * S5_STATE
SGU_CHUNK = 128
SGU_HEADS = 8
SGU_HEAD_DIM = WIDTH // SGU_HEADS
M2_HEAD_DIM = 64
M2_HEADS = WIDTH // M2_HEAD_DIM
M2_GROUPS = 2
M2_STATE = 128
M2_CONV = 4
M2_CHUNK = 128
M2_CONV_CH = WIDTH + 2 * M2_GROUPS * M2_STATE
SC_CONV = 3

IN_SIZES = (WIDTH, WIDTH, WIDTH, WIDTH, WIDTH, WIDTH, M2_CONV_CH, M2_HEADS,
            WIDTH, WIDTH, WIDTH, WIDTH, N_BRANCH * D_MODEL)
IN_OFFS = tuple(int(v) for v in np.cumsum((0,) + IN_SIZES))

SUBLANES = 8
LANES = 128
MXU_DIM = 256
VMEM_LIMIT_BYTES = 56 * 1024 * 1024

NORM_ROWS = 1024
S5_STEPS = 64
SGU_ROWS = 1024
SSD_ROWS = 256
SC_ROWS = 1024
MERGE_ROWS = 512
SPLIT = 2


def _dot(a, b):
    return jnp.dot(a, b, preferred_element_type=F32)


def _dot_nt(a, b):
    return lax.dot_general(a, b, (((1,), (1,)), ((), ())), preferred_element_type=F32)


def _dot_tn(a, b):
    return lax.dot_general(a, b, (((0,), (0,)), ((), ())), preferred_element_type=F32)


def _sigmoid(x):
    return 0.5 * (jnp.tanh(0.5 * x) + 1.0)


def _silu(x):
    hx = 0.5 * x
    return hx * jnp.tanh(hx) + hx


def _gelu(x):
    c = np.float32(np.sqrt(2.0 / np.pi))
    return x * (0.5 * (1.0 + jnp.tanh(c * (x + 0.044715 * (x * x * x)))))


def _softplus(x):
    return jnp.maximum(x, 0.0) + jnp.log1p(jnp.exp(-jnp.abs(x)))


def _rms(x, w):
    return x * lax.rsqrt(jnp.mean(x * x, axis=-1, keepdims=True) + EPS) * w


def _shift_rows(cat, j):
    n8, c = cat.shape
    rot = pltpu.roll(cat.reshape(n8 // SUBLANES, SUBLANES, c), j, 1)
    sub = lax.broadcasted_iota(jnp.int32, (1, SUBLANES, c), 1)
    out = jnp.where(sub >= j, rot[1:], rot[:-1])
    return out.reshape(n8 - SUBLANES, c)


def _params(*semantics):
    return pltpu.CompilerParams(dimension_semantics=semantics, vmem_limit_bytes=VMEM_LIMIT_BYTES)


def _full(shape):
    zeros = (0,) * len(shape)
    return pl.BlockSpec(tuple(shape), lambda *_: zeros)


def _of_layer(arr, layer):
    zeros = (0,) * (arr.ndim - 1)
    return pl.BlockSpec((None,) + tuple(arr.shape[1:]), lambda *_: (layer,) + zeros)


def _in_cols(layer, first, width):
    assert first % width == 0
    return pl.BlockSpec((None, D_MODEL, width), lambda *_: (layer, 0, first // width))


def _seq_spec(rows, cols):
    return pl.BlockSpec((None, rows, cols), lambda b, c: (b, c, 0))


def _rmsnorm_kernel(x_ref, w_ref, o_ref):
    o_ref[...] = _rms(x_ref[...], w_ref[0:1, :]).astype(o_ref.dtype)


def _rmsnorm(x2d, norm_w):
    t, d = x2d.shape
    return pl.pallas_call(
        _rmsnorm_kernel,
        grid=(t // NORM_ROWS,),
        in_specs=[pl.BlockSpec((NORM_ROWS, d), lambda i: (i, 0)), _full(norm_w.shape)],
        out_specs=pl.BlockSpec((NORM_ROWS, d), lambda i: (i, 0)),
        out_shape=jax.ShapeDtypeStruct((t, d), BF16),
        compiler_params=_params("parallel"),
        name="rmsnorm",
    )(x2d, norm_w)


def _s5_prep_kernel(lr_ref, li_ref, ls_ref, lrb_ref, lib_ref, lsb_ref, bre_ref, bim_ref,
                    abre_ref, abim_ref, bbre_ref, bbim_ref):
    step = jnp.exp(ls_ref[...])
    mag = jnp.exp(lr_ref[...] * step)
    ang = li_ref[...] * step
    abre_ref[...] = mag * jnp.cos(ang)
    abim_ref[...] = mag * jnp.sin(ang)

    lr, li = lrb_ref[...], lib_ref[...]
    step = jnp.exp(lsb_ref[...])
    mag = jnp.exp(lr * step)
    ab_re, ab_im = mag * jnp.cos(li * step), mag * jnp.sin(li * step)
    den = lr * lr + li * li
    nr = ab_re - 1.0
    coef_re = (nr * lr + ab_im * li) / den
    coef_im = (ab_im * lr - nr * li) / den
    br, bi = bre_ref[...], bim_ref[...]
    bbre_ref[...] = coef_re * br - coef_im * bi
    bbim_ref[...] = coef_re * bi + coef_im * br


def _s5_prep(lam_re, lam_im, b_re, b_im, c_re, c_im, log_step):
    nl = lam_re.shape[0]
    g, n, p = S5_GROUPS, S5_STATE, S5_GROUP
    ls = jnp.repeat(log_step, n, axis=-1)
    row = lambda v: v.reshape(nl, g * n)
    col = lambda v: jnp.broadcast_to(v.reshape(nl * g * n, 1), (nl * g * n, p))
    ab_re, ab_im, bb_re, bb_im = pl.pallas_call(
        _s5_prep_kernel,
        out_shape=(jax.ShapeDtypeStruct((nl, g * n), F32), jax.ShapeDtypeStruct((nl, g * n), F32),
                   jax.ShapeDtypeStruct((nl * g * n, p), F32), jax.ShapeDtypeStruct((nl * g * n, p), F32)),
        name="s5_prep",
    )(row(lam_re), row(lam_im), row(ls), col(lam_re), col(lam_im), col(ls),
      b_re.reshape(nl * g * n, p), b_im.reshape(nl * g * n, p))

    gh = g // 2
    in_mask = (np.arange(gh * p)[:, None] // p) == (np.arange(gh * n)[None, :] // n)
    out_mask = in_mask.T

    def in_half(bb):
        t = bb.reshape(nl, 2, gh, n, p).transpose(0, 1, 2, 4, 3).reshape(nl, 2, gh * p, n)
        return jnp.where(in_mask, jnp.tile(t, (1, 1, 1, gh)), 0.0)

    def out_half(c):
        t = c.reshape(nl, 2, gh, p, n).transpose(0, 1, 2, 4, 3).reshape(nl, 2, gh * n, p)
        return jnp.where(out_mask, jnp.tile(t, (1, 1, 1, gh)), 0.0)

    wb = jnp.concatenate([in_half(bb_re), in_half(bb_im)], axis=3).astype(BF16)
    cc = jnp.concatenate([out_half(c_re), -out_half(c_im)], axis=2).astype(BF16)
    return ab_re, ab_im, wb, cc


S5_HALF = S5_PAIRS
S5_PERM = MXU_DIM


def _perm_time_major(nb):
    tt = S5_PERM // nb
    i = lax.broadcasted_iota(jnp.int32, (S5_PERM, S5_PERM), 0)
    j = lax.broadcasted_iota(jnp.int32, (S5_PERM, S5_PERM), 1)
    return jnp.where(j == (i % nb) * tt + i // nb, 1.0, 0.0).astype(BF16)


def _s5_kernel(h_ref, wug_ref, wb_ref, cc_ref, d_ref, wglu_ref, abre_ref, abim_ref, o_ref,
               hp_ref, ug_ref, st_ref, carry_ref, *, steps, nb, layer):
    @pl.when(pl.program_id(0) == 0)
    def _():
        carry_ref[...] = jnp.zeros(carry_ref.shape, F32)

    tt = S5_PERM // nb
    perm = _perm_time_major(nb)
    for k in range(steps // tt):
        blk = h_ref[:, k * tt:(k + 1) * tt, :].reshape(S5_PERM, D_MODEL)
        hp_ref[k * S5_PERM:(k + 1) * S5_PERM, :] = _dot(perm, blk).astype(BF16)

    ug_ref[...] = _dot(hp_ref[...], wug_ref[...])
    ub = ug_ref[:, 0:WIDTH].astype(BF16)
    hw = WIDTH // 2
    for hf in range(2):
        st_ref[:, hf * S5_HALF:(hf + 1) * S5_HALF] = _dot(ub[:, hf * hw:(hf + 1) * hw], wb_ref[hf])

    pairs_half = S5_HALF // 2
    for hf in range(2):
        cre = hf * S5_HALF
        cim = cre + pairs_half
        tc = hf * pairs_half
        ar = jnp.broadcast_to(abre_ref[layer:layer + 1, tc:tc + pairs_half], (nb, pairs_half))
        ai = jnp.broadcast_to(abim_ref[layer:layer + 1, tc:tc + pairs_half], (nb, pairs_half))

        def step(t, carry, cre=cre, cim=cim, ar=ar, ai=ai):
            sr, si = carry
            r0 = pl.multiple_of(t * nb, nb)
            nr = (ar * sr - ai * si) + st_ref[pl.ds(r0, nb), cre:cre + pairs_half]
            ni = (ar * si + ai * sr) + st_ref[pl.ds(r0, nb), cim:cim + pairs_half]
            st_ref[pl.ds(r0, nb), cre:cre + pairs_half] = nr
            st_ref[pl.ds(r0, nb), cim:cim + pairs_half] = ni
            return nr, ni

        sr, si = lax.fori_loop(
            0, steps, step,
            (carry_ref[:, cre:cre + pairs_half], carry_ref[:, cim:cim + pairs_half]), unroll=4)
        carry_ref[:, cre:cre + pairs_half] = sr
        carry_ref[:, cim:cim + pairs_half] = si

    ys = []
    for hf in range(2):
        ys.append(_dot(st_ref[:, hf * S5_HALF:(hf + 1) * S5_HALF].astype(BF16), cc_ref[hf]))
    y = jnp.concatenate(ys, axis=-1) + d_ref[layer:layer + 1, :] * ug_ref[:, 0:WIDTH]
    y = _gelu(y)
    y = y * _sigmoid(_dot(y.astype(BF16), wglu_ref[...]))
    y = (y * _silu(ug_ref[:, WIDTH:2 * WIDTH])).astype(BF16)
    for k in range(steps // tt):
        back = _dot_tn(perm, y[k * S5_PERM:(k + 1) * S5_PERM, :])
        o_ref[:, k * tt:(k + 1) * tt, :] = back.reshape(nb, tt, WIDTH).astype(o_ref.dtype)


def _s5_branch(h, win, wb, cc, d, wglu, ab_re, ab_im, layer):
    b, l, dm = h.shape
    assert S5_PERM % b == 0 and b % SUBLANES == 0
    steps = min(S5_STEPS, l)
    rows = steps * b
    return pl.pallas_call(
        functools.partial(_s5_kernel, steps=steps, nb=b, layer=layer),
        grid=(l // steps,),
        in_specs=[pl.BlockSpec((b, steps, dm), lambda c: (0, c, 0)), _in_cols(layer, IN_OFFS[0], 2 * WIDTH),
                  _of_layer(wb, layer), _of_layer(cc, layer), _full(d.shape), _of_layer(wglu, layer),
                  _full(ab_re.shape), _full(ab_im.shape)],
        out_specs=pl.BlockSpec((b, steps, WIDTH), lambda c: (0, c, 0)),
        out_shape=jax.ShapeDtypeStruct((b, l, WIDTH), BF16),
        scratch_shapes=[pltpu.VMEM((rows, dm), BF16),
                        pltpu.VMEM((rows, 2 * WIDTH), F32),
                        pltpu.VMEM((rows, 2 * S5_HALF), F32),
                        pltpu.VMEM((b, 2 * S5_HALF), F32)],
        compiler_params=_params("arbitrary"),
        name="s5_branch",
    )(h, win, wb, cc, d, wglu, ab_re, ab_im)


def _sgu_kernel(h_ref, wu_ref, wv_ref, wg_ref, lnw_ref, lnb_ref, ws_ref, bias_ref, o_ref, *, rows, layer):
    t = SGU_CHUNK
    causal = lax.broadcasted_iota(jnp.int32, (t, t), 0) >= lax.broadcasted_iota(jnp.int32, (t, t), 1)
    first_head = lax.broadcasted_iota(jnp.int32, (t, LANES), 1) < SGU_HEAD_DIM
    wm = [jnp.where(causal, ws_ref[hd], 0.0).astype(BF16) for hd in range(SGU_HEADS)]
    zero = jnp.zeros((), BF16)
    part = rows // SPLIT
    for sp in range(SPLIT):
        h = h_ref[sp * part:(sp + 1) * part, :]
        v = _gelu(_dot(h, wv_ref[...]))
        mu = jnp.mean(v, axis=-1, keepdims=True)
        vc = v - mu
        var = jnp.mean(vc * vc, axis=-1, keepdims=True)
        vn = (vc * lax.rsqrt(var + EPS) * lnw_ref[layer:layer + 1, :] + lnb_ref[layer:layer + 1, :]).astype(BF16)
        ug = _gelu(_dot(h, wu_ref[...])) * _silu(_dot(h, wg_ref[...]))
        for c in range(part // t):
            r = slice(c * t, (c + 1) * t)
            outs = []
            for j in range(WIDTH // LANES):
                blk = vn[r, j * LANES:(j + 1) * LANES]
                outs.append(_dot(wm[2 * j], jnp.where(first_head, blk, zero))
                            + _dot(wm[2 * j + 1], jnp.where(first_head, zero, blk)))
            s = jnp.concatenate(outs, axis=-1) + bias_ref[...]
            o_ref[sp * part + c * t:sp * part + (c + 1) * t, :] = (ug[r] * s).astype(o_ref.dtype)


def _sgu_branch(h, win, ln_w, ln_b, w_s, bias, layer):
    b, l, dm = h.shape
    rows = min(SGU_ROWS, l)
    return pl.pallas_call(
        functools.partial(_sgu_kernel, rows=rows, layer=layer),
        grid=(b, l // rows),
        in_specs=[_seq_spec(rows, dm)] + [_in_cols(layer, IN_OFFS[k], WIDTH) for k in (2, 3, 4)]
                 + [_full(ln_w.shape), _full(ln_b.shape), _of_layer(w_s, layer), _of_layer(bias, layer)],
        out_specs=_seq_spec(rows, WIDTH),
        out_shape=jax.ShapeDtypeStruct((b, l, WIDTH), BF16),
        compiler_params=_params("parallel", "parallel"),
        name="sgu_branch",
    )(h, win, win, win, ln_w, ln_b, w_s, bias)


SSD_QUANT = 3
SSD_PARTS = 3
SSD_LANE_GROUPS = SSD_QUANT * SSD_PARTS


def _ssd_expand_matrix():
    e = np.zeros((LANES, SSD_QUANT * WIDTH), np.float32)
    for grp in range(SSD_LANE_GROUPS):
        for hd in range(M2_HEADS):
            c0 = (grp // SSD_PARTS) * WIDTH + hd * M2_HEAD_DIM
            e[grp * M2_HEADS + hd, c0:c0 + M2_HEAD_DIM] = 1.0
    return jnp.asarray(e, BF16)


def _cumsum_rows(x):
    q = x.shape[0]
    row = lax.broadcasted_iota(jnp.int32, x.shape, 0)
    dist = 1
    while dist < q:
        x = x + jnp.where(row >= dist, pltpu.roll(x, dist, 0), 0.0)
        dist *= 2
    return x


def _ssd_kernel(h_ref, wz_ref, wxbc_ref, wdt_ref, cw_ref, cb_ref, dtb_ref, al_ref, e_ref, d_ref, nw_ref,
                o_ref, prev_ref, st_ref, *, rows, layer):
    @pl.when(pl.program_id(1) == 0)
    def _():
        prev_ref[...] = jnp.zeros(prev_ref.shape, F32)
        st_ref[...] = jnp.zeros(st_ref.shape, F32)

    h = h_ref[...]
    raw = _dot(h, wxbc_ref[...])
    cat = jnp.concatenate([prev_ref[...], raw], axis=0)
    acc = cb_ref[layer:layer + 1, :] + cw_ref[M2_CONV - 1:M2_CONV, :] * raw
    for j in range(1, M2_CONV):
        k = M2_CONV - 1 - j
        acc = acc + cw_ref[k:k + 1, :] * _shift_rows(cat, j)
    prev_ref[...] = raw[rows - SUBLANES:rows, :]
    xbc = _silu(acc)
    z = _dot(h, wz_ref[...])

    dt = _softplus(_dot(h, wdt_ref[...]) + dtb_ref[layer:layer + 1, :])
    da = dt * (-jnp.exp(al_ref[layer:layer + 1, :]))
    grp = lax.broadcasted_iota(jnp.int32, (1, LANES), 1) // M2_HEADS
    quant = grp // SSD_PARTS
    part = grp % SSD_PARTS

    q = M2_CHUNK
    hd_w = M2_HEAD_DIM
    causal = lax.broadcasted_iota(jnp.int32, (q, q), 0) >= lax.broadcasted_iota(jnp.int32, (q, q), 1)
    first_head = lax.broadcasted_iota(jnp.int32, (q, LANES), 1) < hd_w
    zero = jnp.zeros((), BF16)
    gw = WIDTH // M2_GROUPS

    x = xbc[:, 0:WIDTH]
    for s in range(rows // q):
        r = slice(s * q, (s + 1) * q)
        cs = _cumsum_rows(da[r])
        cs_row = cs.T[0:M2_HEADS, :]
        cs_last = cs[q - 1:q, :]
        val = jnp.where(quant == 0, dt[r], jnp.where(quant == 1, dt[r] * jnp.exp(cs_last - cs), jnp.exp(cs)))
        p0 = val.astype(BF16)
        r1 = val - p0.astype(F32)
        p1 = r1.astype(BF16)
        p2 = (r1 - p1.astype(F32)).astype(BF16)
        spread = _dot(jnp.where(part == 0, p0, jnp.where(part == 1, p1, p2)), e_ref[...])
        dt_e = spread[:, 0:WIDTH]
        dtdec_e = spread[:, WIDTH:2 * WIDTH]
        exp_cs = spread[:, 2 * WIDTH:3 * WIDTH]
        exp_last = exp_cs[q - 1:q, :]
        x_c = x[r]
        xdt_b = (x_c * dt_e).astype(BF16)
        xdec = (x_c * dtdec_e).astype(BF16)
        ys = []
        for g in range(M2_GROUPS):
            bo = WIDTH + g * M2_STATE
            co = WIDTH + M2_GROUPS * M2_STATE + g * M2_STATE
            bg = xbc[r, bo:bo + M2_STATE].astype(BF16)
            cg = xbc[r, co:co + M2_STATE].astype(BF16)
            scores = _dot_nt(cg, bg)
            state = st_ref[g]
            y_g = _dot(cg, state.astype(BF16)) * exp_cs[:, g * gw:(g + 1) * gw]
            diag = []
            for jp in range(gw // LANES):
                blk = xdt_b[:, g * gw + jp * LANES:g * gw + (jp + 1) * LANES]
                acc2 = None
                for qq in range(2):
                    hd = g * (M2_HEADS // M2_GROUPS) + jp * 2 + qq
                    seg = cs[:, hd:hd + 1] - cs_row[hd:hd + 1, :]
                    decay = jnp.exp(jnp.where(causal, seg, -jnp.inf))
                    m = (scores * decay).astype(BF16)
                    half = jnp.where(first_head, blk, zero) if qq == 0 else jnp.where(first_head, zero, blk)
                    term = _dot(m, half)
                    acc2 = term if acc2 is None else acc2 + term
                diag.append(acc2)
            ys.append(y_g + jnp.concatenate(diag, axis=-1))
            st_ref[g] = state * exp_last[:, g * gw:(g + 1) * gw] + _dot_tn(bg, xdec[:, g * gw:(g + 1) * gw])
        y = jnp.concatenate(ys, axis=-1) + d_ref[layer:layer + 1, :] * x_c
        y = y * _silu(z[r])
        o_ref[r, :] = _rms(y, nw_ref[layer:layer + 1, :]).astype(o_ref.dtype)


def _ssd_branch(h, win, wdt_groups, conv_w, conv_b, dtb_groups, al_groups, expand, d_lanes, norm_w, layer):
    b, l, dm = h.shape
    rows = min(SSD_ROWS, l)
    return pl.pallas_call(
        functools.partial(_ssd_kernel, rows=rows, layer=layer),
        grid=(b, l // rows),
        in_specs=[_seq_spec(rows, dm), _in_cols(layer, IN_OFFS[5], WIDTH), _in_cols(layer, IN_OFFS[6], M2_CONV_CH),
                  _of_layer(wdt_groups, layer), _of_layer(conv_w, layer), _full(conv_b.shape),
                  _full(dtb_groups.shape), _full(al_groups.shape), _full(expand.shape), _full(d_lanes.shape),
                  _full(norm_w.shape)],
        out_specs=_seq_spec(rows, WIDTH),
        out_shape=jax.ShapeDtypeStruct((b, l, WIDTH), BF16),
        scratch_shapes=[pltpu.VMEM((SUBLANES, M2_CONV_CH), F32),
                        pltpu.VMEM((M2_GROUPS, M2_STATE, WIDTH // M2_GROUPS), F32)],
        compiler_params=_params("parallel", "arbitrary"),
        name="ssd_branch",
    )(h, win, win, wdt_groups, conv_w, conv_b, dtb_groups, al_groups, expand, d_lanes, norm_w)


def _sc_kernel(h_ref, w_ref, cw_ref, o_ref, prev_ref, *, rows):
    @pl.when(pl.program_id(1) == 0)
    def _():
        prev_ref[...] = jnp.zeros(prev_ref.shape, F32)

    part = rows // SPLIT
    prev = prev_ref[...]
    for sp in range(SPLIT):
        proj = _dot(h_ref[sp * part:(sp + 1) * part, :], w_ref[...])
        v = proj[:, WIDTH:2 * WIDTH] * proj[:, 2 * WIDTH:3 * WIDTH]
        cat = jnp.concatenate([prev, v], axis=0)
        acc = cw_ref[SC_CONV - 1:SC_CONV, :] * v
        for j in range(1, SC_CONV):
            k = SC_CONV - 1 - j
            acc = acc + cw_ref[k:k + 1, :] * _shift_rows(cat, j)
        prev = v[part - SUBLANES:part, :]
        o_ref[sp * part:(sp + 1) * part, :] = (
            proj[:, 0:WIDTH] * acc * _silu(proj[:, 3 * WIDTH:4 * WIDTH])).astype(o_ref.dtype)
    prev_ref[...] = prev


def _sc_branch(h, w, conv_w, layer):
    b, l, dm = h.shape
    rows = min(SC_ROWS, l)
    return pl.pallas_call(
        functools.partial(_sc_kernel, rows=rows),
        grid=(b, l // rows),
        in_specs=[_seq_spec(rows, dm), _of_layer(w, layer), _of_layer(conv_w, layer)],
        out_specs=_seq_spec(rows, WIDTH),
        out_shape=jax.ShapeDtypeStruct((b, l, WIDTH), BF16),
        scratch_shapes=[pltpu.VMEM((SUBLANES, WIDTH), F32)],
        compiler_params=_params("parallel", "arbitrary"),
        name="sc_branch",
    )(h, w, conv_w)


def _merge_kernel(x_ref, h_ref, ya_ref, yb_ref, yc_ref, yd_ref, wm_ref, mb_ref, wbr_ref, wo_ref, nw_ref,
                  xo_ref, no_ref, *, norm_row):
    h = h_ref[...]
    d = D_MODEL
    merged = None
    for k, y_ref in enumerate((ya_ref, yb_ref, yc_ref, yd_ref)):
        gate = _sigmoid(_dot(h, wm_ref[:, k * d:(k + 1) * d]) + mb_ref[k:k + 1, :])
        term = gate * _dot(y_ref[...], wbr_ref[k])
        merged = term if merged is None else merged + term
    xn = x_ref[...] + _dot(merged.astype(BF16), wo_ref[...])
    xo_ref[...] = xn
    no_ref[...] = _rms(xn, nw_ref[norm_row:norm_row + 1, :]).astype(no_ref.dtype)


def _merge(x2d, h2d, ys, wm, mb, wbr, wo, next_norm_w, norm_row, norm_dtype, layer):
    t, d = x2d.shape
    tile = lambda cols: pl.BlockSpec((MERGE_ROWS, cols), lambda i: (i, 0))
    return pl.pallas_call(
        functools.partial(_merge_kernel, norm_row=norm_row),
        grid=(t // MERGE_ROWS,),
        in_specs=[tile(d), tile(d)] + [tile(WIDTH)] * N_BRANCH
                 + [_of_layer(wm, layer), _of_layer(mb, layer), _of_layer(wbr, layer), _of_layer(wo, layer),
                    _full(next_norm_w.shape)],
        out_specs=(tile(d), tile(d)),
        out_shape=(jax.ShapeDtypeStruct((t, d), F32), jax.ShapeDtypeStruct((t, d), norm_dtype)),
        compiler_params=_params("parallel"),
        name="merge",
    )(x2d, h2d, *ys, wm, mb, wbr, wo, next_norm_w)


def kernel(x, norm_w, w_in, s5_lambda_re, s5_lambda_im, s5_b_re, s5_b_im, s5_c_re, s5_c_im, s5_d,
           s5_log_step, s5_w_glu, sgu_ln_w, sgu_ln_b, sgu_w, sgu_b, m2_conv_w, m2_conv_b, m2_dt_bias,
           m2_a_log, m2_d, m2_norm_w, sc_conv_w, merge_b, w_branch, w_out, final_norm_w):
    b, l, d = x.shape
    t = b * l
    nl = w_in.shape[0]

    win = w_in.astype(BF16)
    w_sc = win[:, :, IN_OFFS[8]:IN_OFFS[12]]
    w_merge = win[:, :, IN_OFFS[12]:IN_OFFS[13]]
    used = SSD_LANE_GROUPS * M2_HEADS
    lane_groups = lambda v: jnp.pad(jnp.tile(v, (1,) * (v.ndim - 1) + (SSD_LANE_GROUPS,)),
                                    ((0, 0),) * (v.ndim - 1) + ((0, LANES - used),))
    wdt_groups = lane_groups(win[:, :, IN_OFFS[7]:IN_OFFS[8]])
    dtb_groups = lane_groups(m2_dt_bias)
    al_groups = lane_groups(m2_a_log)
    d_lanes = jnp.repeat(m2_d, M2_HEAD_DIM, axis=-1)
    expand = _ssd_expand_matrix()
    sgu_bias = jnp.repeat(sgu_b.transpose(0, 2, 1), SGU_HEAD_DIM, axis=-1)
    ab_re, ab_im, s5_wb, s5_cc = _s5_prep(s5_lambda_re, s5_lambda_im, s5_b_re, s5_b_im, s5_c_re, s5_c_im,
                                          s5_log_step)
    s5_dd = s5_d.reshape(nl, WIDTH)
    wglu = s5_w_glu.astype(BF16)
    wbr = w_branch.astype(BF16)
    wo = w_out.astype(BF16)
    norms = jnp.concatenate([norm_w, final_norm_w.reshape(1, d)], axis=0)

    x2d = x.reshape(t, d)
    h2d = _rmsnorm(x2d, norm_w)
    out = None
    for i in range(DEPTH):
        h = h2d.reshape(b, l, d)
        ya = _s5_branch(h, win, s5_wb, s5_cc, s5_dd, wglu, ab_re, ab_im, i)
        yb = _sgu_branch(h, win, sgu_ln_w, sgu_ln_b, sgu_w, sgu_bias, i)
        yc = _ssd_branch(h, win, wdt_groups, m2_conv_w, m2_conv_b, dtb_groups, al_groups, expand, d_lanes,
                         m2_norm_w, i)
        yd = _sc_branch(h, w_sc, sc_conv_w, i)
        last = i == DEPTH - 1
        x2d, h2d = _merge(x2d, h2d, [y.reshape(t, WIDTH) for y in (ya, yb, yc, yd)], w_merge, merge_b, wbr, wo,
                          norms, i + 1, F32 if last else BF16, i)
        out = h2d
    return out.reshape(b, l, d)
```

```python
import functools

import jax
import jax.numpy as jnp
import numpy as np
from jax import lax
from jax.experimental import pallas as pl
from jax.experimental.pallas import tpu as pltpu

F32 = jnp.float32
BF16 = jnp.bfloat16

D_MODEL = 1024
DEPTH = 2
WIDTH = 512
N_BRANCH = 4
EPS = 1e-6
S5_GROUP = 16
S5_GROUPS = WIDTH // S5_GROUP
S5_STATE = 64
S5_PAIRS = S5_GROUPS * S5_STATE
SGU_CHUNK = 128
SGU_HEADS = 8
SGU_HEAD_DIM = WIDTH // SGU_HEADS
M2_HEAD_DIM = 64
M2_HEADS = WIDTH // M2_HEAD_DIM
M2_GROUPS = 2
M2_STATE = 128
M2_CONV = 4
M2_CHUNK = 128
M2_CONV_CH = WIDTH + 2 * M2_GROUPS * M2_STATE
SC_CONV = 3

IN_SIZES = (WIDTH, WIDTH, WIDTH, WIDTH, WIDTH, WIDTH, M2_CONV_CH, M2_HEADS,
            WIDTH, WIDTH, WIDTH, WIDTH, N_BRANCH * D_MODEL)
IN_OFFS = tuple(int(v) for v in np.cumsum((0,) + IN_SIZES))

SUBLANES = 8
LANES = 128
MXU_DIM = 256
VMEM_LIMIT_BYTES = 56 * 1024 * 1024

NORM_ROWS = 1024
S5_STEPS = 64
SGU_ROWS = 1024
SSD_ROWS = 512
SC_ROWS = 1024
MERGE_ROWS = 1024
SPLIT = 2


def _dot(a, b):
    return jnp.dot(a, b, preferred_element_type=F32)


def _dot_nt(a, b):
    return lax.dot_general(a, b, (((1,), (1,)), ((), ())), preferred_element_type=F32)


def _dot_tn(a, b):
    return lax.dot_general(a, b, (((0,), (0,)), ((), ())), preferred_element_type=F32)


def _sigmoid(x):
    return 0.5 * (jnp.tanh(0.5 * x) + 1.0)


def _silu(x):
    hx = 0.5 * x
    return hx * jnp.tanh(hx) + hx


def _gelu(x):
    c = np.float32(np.sqrt(2.0 / np.pi))
    return x * (0.5 * (1.0 + jnp.tanh(c * (x + 0.044715 * (x * x * x)))))


def _softplus(x):
    return jnp.maximum(x, 0.0) + jnp.log1p(jnp.exp(-jnp.abs(x)))


def _rms(x, w):
    return x * lax.rsqrt(jnp.mean(x * x, axis=-1, keepdims=True) + EPS) * w


def _shift_rows(cat, j):
    n8, c = cat.shape
    rot = pltpu.roll(cat.reshape(n8 // SUBLANES, SUBLANES, c), j, 1)
    sub = lax.broadcasted_iota(jnp.int32, (1, SUBLANES, c), 1)
    out = jnp.where(sub >= j, rot[1:], rot[:-1])
    return out.reshape(n8 - SUBLANES, c)


def _params(*semantics):
    return pltpu.CompilerParams(dimension_semantics=semantics, vmem_limit_bytes=VMEM_LIMIT_BYTES)


RESIDENT = pl.Buffered(1)


def _full(shape):
    zeros = (0,) * len(shape)
    return pl.BlockSpec(tuple(shape), lambda *_: zeros, pipeline_mode=RESIDENT)


def _of_layer(arr, layer):
    zeros = (0,) * (arr.ndim - 1)
    return pl.BlockSpec((None,) + tuple(arr.shape[1:]), lambda *_: (layer,) + zeros, pipeline_mode=RESIDENT)


def _in_cols(layer, first, width):
    assert first % width == 0
    return pl.BlockSpec((None, D_MODEL, width), lambda *_: (layer, 0, first // width), pipeline_mode=RESIDENT)


def _seq_spec(rows, cols):
    return pl.BlockSpec((None, rows, cols), lambda b, c: (b, c, 0))


def _rmsnorm_kernel(x_ref, w_ref, o_ref):
    o_ref[...] = _rms(x_ref[...], w_ref[0:1, :]).astype(o_ref.dtype)


def _rmsnorm(x2d, norm_w):
    t, d = x2d.shape
    return pl.pallas_call(
        _rmsnorm_kernel,
        grid=(t // NORM_ROWS,),
        in_specs=[pl.BlockSpec((NORM_ROWS, d), lambda i: (i, 0)), _full(norm_w.shape)],
        out_specs=pl.BlockSpec((NORM_ROWS, d), lambda i: (i, 0)),
        out_shape=jax.ShapeDtypeStruct((t, d), BF16),
        compiler_params=_params("parallel"),
        name="rmsnorm",
    )(x2d, norm_w)


def _s5_prep_kernel(lr_ref, li_ref, ls_ref, lrb_ref, lib_ref, lsb_ref, bre_ref, bim_ref,
                    abre_ref, abim_ref, bbre_ref, bbim_ref):
    step = jnp.exp(ls_ref[...])
    mag = jnp.exp(lr_ref[...] * step)
    ang = li_ref[...] * step
    abre_ref[...] = mag * jnp.cos(ang)
    abim_ref[...] = mag * jnp.sin(ang)

    lr, li = lrb_ref[...], lib_ref[...]
    step = jnp.exp(lsb_ref[...])
    mag = jnp.exp(lr * step)
    ab_re, ab_im = mag * jnp.cos(li * step), mag * jnp.sin(li * step)
    den = lr * lr + li * li
    nr = ab_re - 1.0
    coef_re = (nr * lr + ab_im * li) / den
    coef_im = (ab_im * lr - nr * li) / den
    br, bi = bre_ref[...], bim_ref[...]
    bbre_ref[...] = coef_re * br - coef_im * bi
    bbim_ref[...] = coef_re * bi + coef_im * br


def _s5_prep(lam_re, lam_im, b_re, b_im, c_re, c_im, log_step):
    nl = lam_re.shape[0]
    g, n, p = S5_GROUPS, S5_STATE, S5_GROUP
    ls = jnp.repeat(log_step, n, axis=-1)
    row = lambda v: v.reshape(nl, g * n)
    col = lambda v: jnp.broadcast_to(v.reshape(nl * g * n, 1), (nl * g * n, p))
    ab_re, ab_im, bb_re, bb_im = pl.pallas_call(
        _s5_prep_kernel,
        out_shape=(jax.ShapeDtypeStruct((nl, g * n), F32), jax.ShapeDtypeStruct((nl, g * n), F32),
                   jax.ShapeDtypeStruct((nl * g * n, p), F32), jax.ShapeDtypeStruct((nl * g * n, p), F32)),
        name="s5_prep",
    )(row(lam_re), row(lam_im), row(ls), col(lam_re), col(lam_im), col(ls),
      b_re.reshape(nl * g * n, p), b_im.reshape(nl * g * n, p))

    gh = g // 2
    in_mask = (np.arange(gh * p)[:, None] // p) == (np.arange(gh * n)[None, :] // n)
    out_mask = in_mask.T

    def in_half(bb):
        t = bb.reshape(nl, 2, gh, n, p).transpose(0, 1, 2, 4, 3).reshape(nl, 2, gh * p, n)
        return jnp.where(in_mask, jnp.tile(t, (1, 1, 1, gh)), 0.0)

    def out_half(c):
        t = c.reshape(nl, 2, gh, p, n).transpose(0, 1, 2, 4, 3).reshape(nl, 2, gh * n, p)
        return jnp.where(out_mask, jnp.tile(t, (1, 1, 1, gh)), 0.0)

    wb = jnp.concatenate([in_half(bb_re), in_half(bb_im)], axis=3).astype(BF16)
    cc = jnp.concatenate([out_half(c_re), -out_half(c_im)], axis=2).astype(BF16)
    return ab_re, ab_im, wb, cc


S5_HALF = S5_PAIRS
S5_PERM = MXU_DIM


def _perm_time_major(nb):
    tt = S5_PERM // nb
    i = lax.broadcasted_iota(jnp.int32, (S5_PERM, S5_PERM), 0)
    j = lax.broadcasted_iota(jnp.int32, (S5_PERM, S5_PERM), 1)
    return jnp.where(j == (i % nb) * tt + i // nb, 1.0, 0.0).astype(BF16)


def _s5_kernel(h_ref, wug_ref, wb_ref, cc_ref, d_ref, wglu_ref, abre_ref, abim_ref, o_ref,
               hp_ref, ug_ref, st_ref, carry_ref, *, steps, nb, layer):
    @pl.when(pl.program_id(0) == 0)
    def _():
        carry_ref[...] = jnp.zeros(carry_ref.shape, F32)

    tt = S5_PERM // nb
    perm = _perm_time_major(nb)
    for k in range(steps // tt):
        blk = h_ref[:, k * tt:(k + 1) * tt, :].reshape(S5_PERM, D_MODEL)
        hp_ref[k * S5_PERM:(k + 1) * S5_PERM, :] = _dot(perm, blk).astype(BF16)

    ug_ref[...] = _dot(hp_ref[...], wug_ref[...])
    ub = ug_ref[:, 0:WIDTH].astype(BF16)
    hw = WIDTH // 2
    for hf in range(2):
        st_ref[:, hf * S5_HALF:(hf + 1) * S5_HALF] = _dot(ub[:, hf * hw:(hf + 1) * hw], wb_ref[hf])

    pairs_half = S5_HALF // 2
    for hf in range(2):
        cre = hf * S5_HALF
        cim = cre + pairs_half
        tc = hf * pairs_half
        ar = jnp.broadcast_to(abre_ref[layer:layer + 1, tc:tc + pairs_half], (nb, pairs_half))
        ai = jnp.broadcast_to(abim_ref[layer:layer + 1, tc:tc + pairs_half], (nb, pairs_half))

        def step(t, carry, cre=cre, cim=cim, ar=ar, ai=ai):
            sr, si = carry
            r0 = pl.multiple_of(t * nb, nb)
            nr = (ar * sr - ai * si) + st_ref[pl.ds(r0, nb), cre:cre + pairs_half]
            ni = (ar * si + ai * sr) + st_ref[pl.ds(r0, nb), cim:cim + pairs_half]
            st_ref[pl.ds(r0, nb), cre:cre + pairs_half] = nr
            st_ref[pl.ds(r0, nb), cim:cim + pairs_half] = ni
            return nr, ni

        sr, si = lax.fori_loop(
            0, steps, step,
            (carry_ref[:, cre:cre + pairs_half], carry_ref[:, cim:cim + pairs_half]), unroll=True)
        carry_ref[:, cre:cre + pairs_half] = sr
        carry_ref[:, cim:cim + pairs_half] = si

    ys = []
    for hf in range(2):
        ys.append(_dot(st_ref[:, hf * S5_HALF:(hf + 1) * S5_HALF].astype(BF16), cc_ref[hf]))
    y = jnp.concatenate(ys, axis=-1) + d_ref[layer:layer + 1, :] * ug_ref[:, 0:WIDTH]
    y = _gelu(y)
    y = y * _sigmoid(_dot(y.astype(BF16), wglu_ref[...]))
    y = (y * _silu(ug_ref[:, WIDTH:2 * WIDTH])).astype(BF16)
    for k in range(steps // tt):
        back = _dot_tn(perm, y[k * S5_PERM:(k + 1) * S5_PERM, :])
        o_ref[:, k * tt:(k + 1) * tt, :] = back.reshape(nb, tt, WIDTH).astype(o_ref.dtype)


def _s5_branch(h, win, wb, cc, d, wglu, ab_re, ab_im, layer):
    b, l, dm = h.shape
    assert S5_PERM % b == 0 and b % SUBLANES == 0
    steps = min(S5_STEPS, l)
    rows = steps * b
    return pl.pallas_call(
        functools.partial(_s5_kernel, steps=steps, nb=b, layer=layer),
        grid=(l // steps,),
        in_specs=[pl.BlockSpec((b, steps, dm), lambda c: (0, c, 0)), _in_cols(layer, IN_OFFS[0], 2 * WIDTH),
                  _of_layer(wb, layer), _of_layer(cc, layer), _full(d.shape), _of_layer(wglu, layer),
                  _full(ab_re.shape), _full(ab_im.shape)],
        out_specs=pl.BlockSpec((b, steps, WIDTH), lambda c: (0, c, 0)),
        out_shape=jax.ShapeDtypeStruct((b, l, WIDTH), BF16),
        scratch_shapes=[pltpu.VMEM((rows, dm), BF16),
                        pltpu.VMEM((rows, 2 * WIDTH), F32),
                        pltpu.VMEM((rows, 2 * S5_HALF), F32),
                        pltpu.VMEM((b, 2 * S5_HALF), F32)],
        compiler_params=_params("arbitrary"),
        name="s5_branch",
    )(h, win, wb, cc, d, wglu, ab_re, ab_im)


def _sgu_kernel(h_ref, wu_ref, wv_ref, wg_ref, lnw_ref, lnb_ref, ws_ref, bias_ref, o_ref, *, rows, layer):
    t = SGU_CHUNK
    causal = lax.broadcasted_iota(jnp.int32, (t, t), 0) >= lax.broadcasted_iota(jnp.int32, (t, t), 1)
    first_head = lax.broadcasted_iota(jnp.int32, (t, LANES), 1) < SGU_HEAD_DIM
    wm = [jnp.where(causal, ws_ref[hd], 0.0).astype(BF16) for hd in range(SGU_HEADS)]
    zero = jnp.zeros((), BF16)
    part = rows // SPLIT
    for sp in range(SPLIT):
        h = h_ref[sp * part:(sp + 1) * part, :]
        v = _gelu(_dot(h, wv_ref[...]))
        mu = jnp.mean(v, axis=-1, keepdims=True)
        vc = v - mu
        var = jnp.mean(vc * vc, axis=-1, keepdims=True)
        vn = (vc * lax.rsqrt(var + EPS) * lnw_ref[layer:layer + 1, :] + lnb_ref[layer:layer + 1, :]).astype(BF16)
        ug = _gelu(_dot(h, wu_ref[...])) * _silu(_dot(h, wg_ref[...]))
        for c in range(part // t):
            r = slice(c * t, (c + 1) * t)
            outs = []
            for j in range(WIDTH // LANES):
                blk = vn[r, j * LANES:(j + 1) * LANES]
                outs.append(_dot(wm[2 * j], jnp.where(first_head, blk, zero))
                            + _dot(wm[2 * j + 1], jnp.where(first_head, zero, blk)))
            s = jnp.concatenate(outs, axis=-1) + bias_ref[...]
            o_ref[sp * part + c * t:sp * part + (c + 1) * t, :] = (ug[r] * s).astype(o_ref.dtype)


def _sgu_branch(h, win, ln_w, ln_b, w_s, bias, layer):
    b, l, dm = h.shape
    rows = min(SGU_ROWS, l)
    return pl.pallas_call(
        functools.partial(_sgu_kernel, rows=rows, layer=layer),
        grid=(b, l // rows),
        in_specs=[_seq_spec(rows, dm)] + [_in_cols(layer, IN_OFFS[k], WIDTH) for k in (2, 3, 4)]
                 + [_full(ln_w.shape), _full(ln_b.shape), _of_layer(w_s, layer), _of_layer(bias, layer)],
        out_specs=_seq_spec(rows, WIDTH),
        out_shape=jax.ShapeDtypeStruct((b, l, WIDTH), BF16),
        compiler_params=_params("parallel", "parallel"),
        name="sgu_branch",
    )(h, win, win, win, ln_w, ln_b, w_s, bias)


SSD_QUANT = 3
SSD_PARTS = 3
SSD_LANE_GROUPS = SSD_QUANT * SSD_PARTS


def _ssd_expand_matrix():
    e = np.zeros((LANES, SSD_QUANT * WIDTH), np.float32)
    for grp in range(SSD_LANE_GROUPS):
        for hd in range(M2_HEADS):
            c0 = (grp // SSD_PARTS) * WIDTH + hd * M2_HEAD_DIM
            e[grp * M2_HEADS + hd, c0:c0 + M2_HEAD_DIM] = 1.0
    return jnp.asarray(e, BF16)


def _cumsum_rows(x):
    q = x.shape[0]
    row = lax.broadcasted_iota(jnp.int32, x.shape, 0)
    dist = 1
    while dist < q:
        x = x + jnp.where(row >= dist, pltpu.roll(x, dist, 0), 0.0)
        dist *= 2
    return x


def _ssd_kernel(h_ref, wz_ref, wxbc_ref, wdt_ref, cw_ref, cb_ref, dtb_ref, al_ref, e_ref, d_ref, nw_ref,
                o_ref, prev_ref, st_ref, *, rows, layer):
    @pl.when(pl.program_id(1) == 0)
    def _():
        prev_ref[...] = jnp.zeros(prev_ref.shape, F32)
        st_ref[...] = jnp.zeros(st_ref.shape, F32)

    h = h_ref[...]
    raw = _dot(h, wxbc_ref[...])
    cat = jnp.concatenate([prev_ref[...], raw], axis=0)
    acc = cb_ref[layer:layer + 1, :] + cw_ref[M2_CONV - 1:M2_CONV, :] * raw
    for j in range(1, M2_CONV):
        k = M2_CONV - 1 - j
        acc = acc + cw_ref[k:k + 1, :] * _shift_rows(cat, j)
    prev_ref[...] = raw[rows - SUBLANES:rows, :]
    xbc = _silu(acc)
    z = _dot(h, wz_ref[...])

    dt = _softplus(_dot(h, wdt_ref[...]) + dtb_ref[layer:layer + 1, :])
    da = dt * (-jnp.exp(al_ref[layer:layer + 1, :]))
    grp = lax.broadcasted_iota(jnp.int32, (1, LANES), 1) // M2_HEADS
    quant = grp // SSD_PARTS
    part = grp % SSD_PARTS

    q = M2_CHUNK
    hd_w = M2_HEAD_DIM
    causal = lax.broadcasted_iota(jnp.int32, (q, q), 0) >= lax.broadcasted_iota(jnp.int32, (q, q), 1)
    first_head = lax.broadcasted_iota(jnp.int32, (q, LANES), 1) < hd_w
    zero = jnp.zeros((), BF16)
    gw = WIDTH // M2_GROUPS

    x = xbc[:, 0:WIDTH]
    for s in range(rows // q):
        r = slice(s * q, (s + 1) * q)
        cs = _cumsum_rows(da[r])
        cs_row = cs.T[0:M2_HEADS, :]
        cs_last = cs[q - 1:q, :]
        val = jnp.where(quant == 0, dt[r], jnp.where(quant == 1, dt[r] * jnp.exp(cs_last - cs), jnp.exp(cs)))
        p0 = val.astype(BF16)
        r1 = val - p0.astype(F32)
        p1 = r1.astype(BF16)
        p2 = (r1 - p1.astype(F32)).astype(BF16)
        spread = _dot(jnp.where(part == 0, p0, jnp.where(part == 1, p1, p2)), e_ref[...])
        dt_e = spread[:, 0:WIDTH]
        dtdec_e = spread[:, WIDTH:2 * WIDTH]
        exp_cs = spread[:, 2 * WIDTH:3 * WIDTH]
        exp_last = exp_cs[q - 1:q, :]
        x_c = x[r]
        xdt_b = (x_c * dt_e).astype(BF16)
        xdec = (x_c * dtdec_e).astype(BF16)
        ys = []
        for g in range(M2_GROUPS):
            bo = WIDTH + g * M2_STATE
            co = WIDTH + M2_GROUPS * M2_STATE + g * M2_STATE
            bg = xbc[r, bo:bo + M2_STATE].astype(BF16)
            cg = xbc[r, co:co + M2_STATE].astype(BF16)
            scores = _dot_nt(cg, bg)
            state = st_ref[g]
            y_g = _dot(cg, state.astype(BF16)) * exp_cs[:, g * gw:(g + 1) * gw]
            diag = []
            for jp in range(gw // LANES):
                blk = xdt_b[:, g * gw + jp * LANES:g * gw + (jp + 1) * LANES]
                acc2 = None
                for qq in range(2):
                    hd = g * (M2_HEADS // M2_GROUPS) + jp * 2 + qq
                    seg = cs[:, hd:hd + 1] - cs_row[hd:hd + 1, :]
                    decay = jnp.exp(jnp.where(causal, seg, -jnp.inf))
                    m = (scores * decay).astype(BF16)
                    half = jnp.where(first_head, blk, zero) if qq == 0 else jnp.where(first_head, zero, blk)
                    term = _dot(m, half)
                    acc2 = term if acc2 is None else acc2 + term
                diag.append(acc2)
            ys.append(y_g + jnp.concatenate(diag, axis=-1))
            st_ref[g] = state * exp_last[:, g * gw:(g + 1) * gw] + _dot_tn(bg, xdec[:, g * gw:(g + 1) * gw])
        y = jnp.concatenate(ys, axis=-1) + d_ref[layer:layer + 1, :] * x_c
        y = y * _silu(z[r])
        o_ref[r, :] = _rms(y, nw_ref[layer:layer + 1, :]).astype(o_ref.dtype)


def _ssd_branch(h, win, wdt_groups, conv_w, conv_b, dtb_groups, al_groups, expand, d_lanes, norm_w, layer):
    b, l, dm = h.shape
    rows = min(SSD_ROWS, l)
    return pl.pallas_call(
        functools.partial(_ssd_kernel, rows=rows, layer=layer),
        grid=(b, l // rows),
        in_specs=[_seq_spec(rows, dm), _in_cols(layer, IN_OFFS[5], WIDTH), _in_cols(layer, IN_OFFS[6], M2_CONV_CH),
                  _of_layer(wdt_groups, layer), _of_layer(conv_w, layer), _full(conv_b.shape),
                  _full(dtb_groups.shape), _full(al_groups.shape), _full(expand.shape), _full(d_lanes.shape),
                  _full(norm_w.shape)],
        out_specs=_seq_spec(rows, WIDTH),
        out_shape=jax.ShapeDtypeStruct((b, l, WIDTH), BF16),
        scratch_shapes=[pltpu.VMEM((SUBLANES, M2_CONV_CH), F32),
                        pltpu.VMEM((M2_GROUPS, M2_STATE, WIDTH // M2_GROUPS), F32)],
        compiler_params=_params("parallel", "arbitrary"),
        name="ssd_branch",
    )(h, win, win, wdt_groups, conv_w, conv_b, dtb_groups, al_groups, expand, d_lanes, norm_w)


def _sc_kernel(h_ref, w_ref, cw_ref, o_ref, prev_ref, *, rows):
    @pl.when(pl.program_id(1) == 0)
    def _():
        prev_ref[...] = jnp.zeros(prev_ref.shape, F32)

    part = rows // SPLIT
    prev = prev_ref[...]
    projs = [_dot(h_ref[sp * part:(sp + 1) * part, :], w_ref[...]) for sp in range(SPLIT)]
    for sp in range(SPLIT):
        proj = projs[sp]
        v = proj[:, WIDTH:2 * WIDTH] * proj[:, 2 * WIDTH:3 * WIDTH]
        cat = jnp.concatenate([prev, v], axis=0)
        acc = cw_ref[SC_CONV - 1:SC_CONV, :] * v
        for j in range(1, SC_CONV):
            k = SC_CONV - 1 - j
            acc = acc + cw_ref[k:k + 1, :] * _shift_rows(cat, j)
        prev = v[part - SUBLANES:part, :]
        o_ref[sp * part:(sp + 1) * part, :] = (
            proj[:, 0:WIDTH] * acc * _silu(proj[:, 3 * WIDTH:4 * WIDTH])).astype(o_ref.dtype)
    prev_ref[...] = prev


def _sc_branch(h, w, conv_w, layer):
    b, l, dm = h.shape
    rows = min(SC_ROWS, l)
    return pl.pallas_call(
        functools.partial(_sc_kernel, rows=rows),
        grid=(b, l // rows),
        in_specs=[_seq_spec(rows, dm), _of_layer(w, layer), _of_layer(conv_w, layer)],
        out_specs=_seq_spec(rows, WIDTH),
        out_shape=jax.ShapeDtypeStruct((b, l, WIDTH), BF16),
        scratch_shapes=[pltpu.VMEM((SUBLANES, WIDTH), F32)],
        compiler_params=_params("parallel", "arbitrary"),
        name="sc_branch",
    )(h, w, conv_w)


def _merge_kernel(x_ref, h_ref, ya_ref, yb_ref, yc_ref, yd_ref, wm_ref, mb_ref, wbr_ref, wo_ref, nw_ref,
                  xo_ref, no_ref, *, norm_row, rows):
    d = D_MODEL
    part = rows // SPLIT
    for sp in range(SPLIT):
        r = slice(sp * part, (sp + 1) * part)
        h = h_ref[r, :]
        merged = None
        for k, y_ref in enumerate((ya_ref, yb_ref, yc_ref, yd_ref)):
            gate = _sigmoid(_dot(h, wm_ref[:, k * d:(k + 1) * d]) + mb_ref[k:k + 1, :])
            term = gate * _dot(y_ref[r, :], wbr_ref[k])
            merged = term if merged is None else merged + term
        xn = x_ref[r, :] + _dot(merged.astype(BF16), wo_ref[...])
        xo_ref[r, :] = xn
        no_ref[r, :] = _rms(xn, nw_ref[norm_row:norm_row + 1, :]).astype(no_ref.dtype)


def _merge(x2d, h2d, ys, wm, mb, wbr, wo, next_norm_w, norm_row, norm_dtype, layer):
    t, d = x2d.shape
    tile = lambda cols: pl.BlockSpec((MERGE_ROWS, cols), lambda i: (i, 0))
    return pl.pallas_call(
        functools.partial(_merge_kernel, norm_row=norm_row, rows=MERGE_ROWS),
        grid=(t // MERGE_ROWS,),
        in_specs=[tile(d), tile(d)] + [tile(WIDTH)] * N_BRANCH
                 + [_of_layer(wm, layer), _of_layer(mb, layer), _of_layer(wbr, layer), _of_layer(wo, layer),
                    _full(next_norm_w.shape)],
        out_specs=(tile(d), tile(d)),
        out_shape=(jax.ShapeDtypeStruct((t, d), F32), jax.ShapeDtypeStruct((t, d), norm_dtype)),
        compiler_params=_params("parallel"),
        name="merge",
    )(x2d, h2d, *ys, wm, mb, wbr, wo, next_norm_w)


def kernel(x, norm_w, w_in, s5_lambda_re, s5_lambda_im, s5_b_re, s5_b_im, s5_c_re, s5_c_im, s5_d,
           s5_log_step, s5_w_glu, sgu_ln_w, sgu_ln_b, sgu_w, sgu_b, m2_conv_w, m2_conv_b, m2_dt_bias,
           m2_a_log, m2_d, m2_norm_w, sc_conv_w, merge_b, w_branch, w_out, final_norm_w):
    b, l, d = x.shape
    t = b * l
    nl = w_in.shape[0]

    win = w_in.astype(BF16)
    w_sc = win[:, :, IN_OFFS[8]:IN_OFFS[12]]
    w_merge = win[:, :, IN_OFFS[12]:IN_OFFS[13]]
    used = SSD_LANE_GROUPS * M2_HEADS
    lane_groups = lambda v: jnp.pad(jnp.tile(v, (1,) * (v.ndim - 1) + (SSD_LANE_GROUPS,)),
                                    ((0, 0),) * (v.ndim - 1) + ((0, LANES - used),))
    wdt_groups = lane_groups(win[:, :, IN_OFFS[7]:IN_OFFS[8]])
    dtb_groups = lane_groups(m2_dt_bias)
    al_groups = lane_groups(m2_a_log)
    d_lanes = jnp.repeat(m2_d, M2_HEAD_DIM, axis=-1)
    expand = _ssd_expand_matrix()
    sgu_bias = jnp.repeat(sgu_b.transpose(0, 2, 1), SGU_HEAD_DIM, axis=-1)
    ab_re, ab_im, s5_wb, s5_cc = _s5_prep(s5_lambda_re, s5_lambda_im, s5_b_re, s5_b_im, s5_c_re, s5_c_im,
                                          s5_log_step)
    s5_dd = s5_d.reshape(nl, WIDTH)
    wglu = s5_w_glu.astype(BF16)
    wbr = w_branch.astype(BF16)
    wo = w_out.astype(BF16)
    norms = jnp.concatenate([norm_w, final_norm_w.reshape(1, d)], axis=0)

    x2d = x.reshape(t, d)
    h2d = _rmsnorm(x2d, norm_w)
    out = None
    for i in range(DEPTH):
        h = h2d.reshape(b, l, d)
        ya = _s5_branch(h, win, s5_wb, s5_cc, s5_dd, wglu, ab_re, ab_im, i)
        yb = _sgu_branch(h, win, sgu_ln_w, sgu_ln_b, sgu_w, sgu_bias, i)
        yc = _ssd_branch(h, win, wdt_groups, m2_conv_w, m2_conv_b, dtb_groups, al_groups, expand, d_lanes,
                         m2_norm_w, i)
        yd = _sc_branch(h, w_sc, sc_conv_w, i)
        last = i == DEPTH - 1
        x2d, h2d = _merge(x2d, h2d, [y.reshape(t, WIDTH) for y in (ya, yb, yc, yd)], w_merge, merge_b, wbr, wo,
                          norms, i + 1, F32 if last else BF16, i)
        out = h2d
    return out.reshape(b, l, d)
```

```python
import functools

import jax
import jax.numpy as jnp
import numpy as np
from jax import lax
from jax.experimental import pallas as pl
from jax.experimental.pallas import tpu as pltpu

F32 = jnp.float32
BF16 = jnp.bfloat16

D_MODEL = 1024
DEPTH = 2
WIDTH = 512
N_BRANCH = 4
EPS = 1e-6
S5_GROUP = 16
S5_GROUPS = WIDTH // S5_GROUP
S5_STATE = 64
S5_PAIRS = S5_GROUPS * S5_STATE
SGU_CHUNK = 128
SGU_HEADS = 8
SGU_HEAD_DIM = WIDTH // SGU_HEADS
M2_HEAD_DIM = 64
M2_HEADS = WIDTH // M2_HEAD_DIM
M2_GROUPS = 2
M2_STATE = 128
M2_CONV = 4
M2_CHUNK = 128
M2_CONV_CH = WIDTH + 2 * M2_GROUPS * M2_STATE
SC_CONV = 3

IN_SIZES = (WIDTH, WIDTH, WIDTH, WIDTH, WIDTH, WIDTH, M2_CONV_CH, M2_HEADS,
            WIDTH, WIDTH, WIDTH, WIDTH, N_BRANCH * D_MODEL)
IN_OFFS = tuple(int(v) for v in np.cumsum((0,) + IN_SIZES))

SUBLANES = 8
LANES = 128
MXU_DIM = 256
VMEM_LIMIT_BYTES = 56 * 1024 * 1024

NORM_ROWS = 1024
S5_STEPS = 128
SGU_ROWS = 1024
SSD_ROWS = 512
SC_ROWS = 1024
MERGE_ROWS = 1024
SPLIT = 2


def _dot(a, b):
    return jnp.dot(a, b, preferred_element_type=F32)


def _dot_nt(a, b):
    return lax.dot_general(a, b, (((1,), (1,)), ((), ())), preferred_element_type=F32)


def _dot_tn(a, b):
    return lax.dot_general(a, b, (((0,), (0,)), ((), ())), preferred_element_type=F32)


def _sigmoid(x):
    return 0.5 * (jnp.tanh(0.5 * x) + 1.0)


def _silu(x):
    hx = 0.5 * x
    return hx * jnp.tanh(hx) + hx


def _gelu(x):
    c = np.float32(np.sqrt(2.0 / np.pi))
    return x * (0.5 * (1.0 + jnp.tanh(c * (x + 0.044715 * (x * x * x)))))


def _softplus(x):
    return jnp.maximum(x, 0.0) + jnp.log1p(jnp.exp(-jnp.abs(x)))


def _rms(x, w):
    return x * lax.rsqrt(jnp.mean(x * x, axis=-1, keepdims=True) + EPS) * w


def _shift_rows(cat, j):
    n8, c = cat.shape
    rot = pltpu.roll(cat.reshape(n8 // SUBLANES, SUBLANES, c), j, 1)
    sub = lax.broadcasted_iota(jnp.int32, (1, SUBLANES, c), 1)
    out = jnp.where(sub >= j, rot[1:], rot[:-1])
    return out.reshape(n8 - SUBLANES, c)


def _params(*semantics):
    return pltpu.CompilerParams(dimension_semantics=semantics, vmem_limit_bytes=VMEM_LIMIT_BYTES)


RESIDENT = pl.Buffered(1)


def _full(shape):
    zeros = (0,) * len(shape)
    return pl.BlockSpec(tuple(shape), lambda *_: zeros, pipeline_mode=RESIDENT)


def _of_layer(arr, layer):
    zeros = (0,) * (arr.ndim - 1)
    return pl.BlockSpec((None,) + tuple(arr.shape[1:]), lambda *_: (layer,) + zeros, pipeline_mode=RESIDENT)


def _in_cols(layer, first, width):
    assert first % width == 0
    return pl.BlockSpec((None, D_MODEL, width), lambda *_: (layer, 0, first // width), pipeline_mode=RESIDENT)


def _seq_spec(rows, cols):
    return pl.BlockSpec((None, rows, cols), lambda b, c: (b, c, 0))


def _rmsnorm_kernel(x_ref, w_ref, o_ref):
    o_ref[...] = _rms(x_ref[...], w_ref[0:1, :]).astype(o_ref.dtype)


def _rmsnorm(x2d, norm_w):
    t, d = x2d.shape
    return pl.pallas_call(
        _rmsnorm_kernel,
        grid=(t // NORM_ROWS,),
        in_specs=[pl.BlockSpec((NORM_ROWS, d), lambda i: (i, 0)), _full(norm_w.shape)],
        out_specs=pl.BlockSpec((NORM_ROWS, d), lambda i: (i, 0)),
        out_shape=jax.ShapeDtypeStruct((t, d), BF16),
        compiler_params=_params("parallel"),
        name="rmsnorm",
    )(x2d, norm_w)


def _s5_prep_kernel(lr_ref, li_ref, ls_ref, bre_ref, bim_ref, abre_ref, abim_ref, bbre_ref, bbim_ref):
    lr, li = lr_ref[...], li_ref[...]
    step = jnp.exp(ls_ref[...])
    mag = jnp.exp(lr * step)
    ang = li * step
    ab_re, ab_im = mag * jnp.cos(ang), mag * jnp.sin(ang)
    abre_ref[...] = ab_re
    abim_ref[...] = ab_im
    den = lr * lr + li * li
    nr = ab_re - 1.0
    coef_re = (nr * lr + ab_im * li) / den
    coef_im = (ab_im * lr - nr * li) / den
    for layer in range(lr.shape[0]):
        cr, ci = coef_re[layer:layer + 1, :], coef_im[layer:layer + 1, :]
        br, bi = bre_ref[layer], bim_ref[layer]
        bbre_ref[layer] = cr * br - ci * bi
        bbim_ref[layer] = cr * bi + ci * br


def _s5_prep(lam_re, lam_im, b_re, b_im, c_re, c_im, log_step):
    nl = lam_re.shape[0]
    g, n, p = S5_GROUPS, S5_STATE, S5_GROUP
    ls = jnp.repeat(log_step, n, axis=-1)
    row = lambda v: v.reshape(nl, g * n)
    chan_minor = lambda v: v.transpose(0, 3, 1, 2).reshape(nl, p, g * n)
    ab_re, ab_im, bb_re, bb_im = pl.pallas_call(
        _s5_prep_kernel,
        out_shape=(jax.ShapeDtypeStruct((nl, g * n), F32), jax.ShapeDtypeStruct((nl, g * n), F32),
                   jax.ShapeDtypeStruct((nl, p, g * n), F32), jax.ShapeDtypeStruct((nl, p, g * n), F32)),
        name="s5_prep",
    )(row(lam_re), row(lam_im), row(ls), chan_minor(b_re), chan_minor(b_im))

    gh = g // 2
    in_mask = (np.arange(gh * p)[:, None] // p) == (np.arange(gh * n)[None, :] // n)
    out_mask = in_mask.T

    def in_half(bb):
        t = bb.reshape(nl, p, 2, gh, n).transpose(0, 2, 3, 1, 4).reshape(nl, 2, gh * p, n)
        return jnp.where(in_mask, jnp.tile(t, (1, 1, 1, gh)), 0.0)

    def out_half(c):
        t = c.reshape(nl, 2, gh, p, n).transpose(0, 1, 2, 4, 3).reshape(nl, 2, gh * n, p)
        return jnp.where(out_mask, jnp.tile(t, (1, 1, 1, gh)), 0.0)

    wb = jnp.concatenate([in_half(bb_re), in_half(bb_im)], axis=3).astype(BF16)
    cc = jnp.concatenate([out_half(c_re), -out_half(c_im)], axis=2).astype(BF16)
    return ab_re, ab_im, wb, cc


S5_HALF = S5_PAIRS
S5_PERM = MXU_DIM


def _perm_time_major(nb):
    tt = S5_PERM // nb
    i = lax.broadcasted_iota(jnp.int32, (S5_PERM, S5_PERM), 0)
    j = lax.broadcasted_iota(jnp.int32, (S5_PERM, S5_PERM), 1)
    return jnp.where(j == (i % nb) * tt + i // nb, 1.0, 0.0).astype(BF16)


def _s5_kernel(h_ref, wug_ref, wb_ref, cc_ref, d_ref, wglu_ref, abre_ref, abim_ref, o_ref,
               hp_ref, ug_ref, st_ref, carry_ref, *, steps, nb, layer):
    @pl.when(pl.program_id(0) == 0)
    def _():
        carry_ref[...] = jnp.zeros(carry_ref.shape, F32)

    tt = S5_PERM // nb
    perm = _perm_time_major(nb)
    for k in range(steps // tt):
        blk = h_ref[:, k * tt:(k + 1) * tt, :].reshape(S5_PERM, D_MODEL)
        hp_ref[k * S5_PERM:(k + 1) * S5_PERM, :] = _dot(perm, blk).astype(BF16)

    ug_ref[...] = _dot(hp_ref[...], wug_ref[...])
    ub = ug_ref[:, 0:WIDTH].astype(BF16)
    hw = WIDTH // 2
    for hf in range(2):
        st_ref[:, hf * S5_HALF:(hf + 1) * S5_HALF] = _dot(ub[:, hf * hw:(hf + 1) * hw], wb_ref[hf])

    pairs_half = S5_HALF // 2
    for hf in range(2):
        cre = hf * S5_HALF
        cim = cre + pairs_half
        tc = hf * pairs_half
        ar = jnp.broadcast_to(abre_ref[layer:layer + 1, tc:tc + pairs_half], (nb, pairs_half))
        ai = jnp.broadcast_to(abim_ref[layer:layer + 1, tc:tc + pairs_half], (nb, pairs_half))

        def step(t, carry, cre=cre, cim=cim, ar=ar, ai=ai):
            sr, si = carry
            r0 = pl.multiple_of(t * nb, nb)
            nr = (ar * sr - ai * si) + st_ref[pl.ds(r0, nb), cre:cre + pairs_half]
            ni = (ar * si + ai * sr) + st_ref[pl.ds(r0, nb), cim:cim + pairs_half]
            st_ref[pl.ds(r0, nb), cre:cre + pairs_half] = nr
            st_ref[pl.ds(r0, nb), cim:cim + pairs_half] = ni
            return nr, ni

        sr, si = lax.fori_loop(
            0, steps, step,
            (carry_ref[:, cre:cre + pairs_half], carry_ref[:, cim:cim + pairs_half]), unroll=True)
        carry_ref[:, cre:cre + pairs_half] = sr
        carry_ref[:, cim:cim + pairs_half] = si

    ys = []
    for hf in range(2):
        ys.append(_dot(st_ref[:, hf * S5_HALF:(hf + 1) * S5_HALF].astype(BF16), cc_ref[hf]))
    y = jnp.concatenate(ys, axis=-1) + d_ref[layer:layer + 1, :] * ug_ref[:, 0:WIDTH]
    y = _gelu(y)
    y = y * _sigmoid(_dot(y.astype(BF16), wglu_ref[...]))
    y = (y * _silu(ug_ref[:, WIDTH:2 * WIDTH])).astype(BF16)
    for k in range(steps // tt):
        back = _dot_tn(perm, y[k * S5_PERM:(k + 1) * S5_PERM, :])
        o_ref[:, k * tt:(k + 1) * tt, :] = back.reshape(nb, tt, WIDTH).astype(o_ref.dtype)


def _s5_branch(h, win, wb, cc, d, wglu, ab_re, ab_im, layer):
    b, l, dm = h.shape
    assert S5_PERM % b == 0 and b % SUBLANES == 0
    steps = min(S5_STEPS, l)
    rows = steps * b
    return pl.pallas_call(
        functools.partial(_s5_kernel, steps=steps, nb=b, layer=layer),
        grid=(l // steps,),
        in_specs=[pl.BlockSpec((b, steps, dm), lambda c: (0, c, 0)), _in_cols(layer, IN_OFFS[0], 2 * WIDTH),
                  _of_layer(wb, layer), _of_layer(cc, layer), _full(d.shape), _of_layer(wglu, layer),
                  _full(ab_re.shape), _full(ab_im.shape)],
        out_specs=pl.BlockSpec((b, steps, WIDTH), lambda c: (0, c, 0)),
        out_shape=jax.ShapeDtypeStruct((b, l, WIDTH), BF16),
        scratch_shapes=[pltpu.VMEM((rows, dm), BF16),
                        pltpu.VMEM((rows, 2 * WIDTH), F32),
                        pltpu.VMEM((rows, 2 * S5_HALF), F32),
                        pltpu.VMEM((b, 2 * S5_HALF), F32)],
        compiler_params=_params("arbitrary"),
        name="s5_branch",
    )(h, win, wb, cc, d, wglu, ab_re, ab_im)


def _sgu_kernel(h_ref, wu_ref, wv_ref, wg_ref, lnw_ref, lnb_ref, ws_ref, bias_ref, o_ref, *, rows, layer):
    t = SGU_CHUNK
    causal = lax.broadcasted_iota(jnp.int32, (t, t), 0) >= lax.broadcasted_iota(jnp.int32, (t, t), 1)
    first_head = lax.broadcasted_iota(jnp.int32, (t, LANES), 1) < SGU_HEAD_DIM
    wm = [jnp.where(causal, ws_ref[hd], 0.0).astype(BF16) for hd in range(SGU_HEADS)]
    zero = jnp.zeros((), BF16)
    part = rows // SPLIT
    for sp in range(SPLIT):
        h = h_ref[sp * part:(sp + 1) * part, :]
        v = _gelu(_dot(h, wv_ref[...]))
        mu = jnp.mean(v, axis=-1, keepdims=True)
        vc = v - mu
        var = jnp.mean(vc * vc, axis=-1, keepdims=True)
        vn = (vc * lax.rsqrt(var + EPS) * lnw_ref[layer:layer + 1, :] + lnb_ref[layer:layer + 1, :]).astype(BF16)
        ug = _gelu(_dot(h, wu_ref[...])) * _silu(_dot(h, wg_ref[...]))
        for c in range(part // t):
            r = slice(c * t, (c + 1) * t)
            outs = []
            for j in range(WIDTH // LANES):
                blk = vn[r, j * LANES:(j + 1) * LANES]
                outs.append(_dot(wm[2 * j], jnp.where(first_head, blk, zero))
                            + _dot(wm[2 * j + 1], jnp.where(first_head, zero, blk)))
            s = jnp.concatenate(outs, axis=-1) + bias_ref[...]
            o_ref[sp * part + c * t:sp * part + (c + 1) * t, :] = (ug[r] * s).astype(o_ref.dtype)


def _sgu_branch(h, win, ln_w, ln_b, w_s, bias, layer):
    b, l, dm = h.shape
    rows = min(SGU_ROWS, l)
    return pl.pallas_call(
        functools.partial(_sgu_kernel, rows=rows, layer=layer),
        grid=(b, l // rows),
        in_specs=[_seq_spec(rows, dm)] + [_in_cols(layer, IN_OFFS[k], WIDTH) for k in (2, 3, 4)]
                 + [_full(ln_w.shape), _full(ln_b.shape), _of_layer(w_s, layer), _of_layer(bias, layer)],
        out_specs=_seq_spec(rows, WIDTH),
        out_shape=jax.ShapeDtypeStruct((b, l, WIDTH), BF16),
        compiler_params=_params("parallel", "parallel"),
        name="sgu_branch",
    )(h, win, win, win, ln_w, ln_b, w_s, bias)


SSD_QUANT = 3
SSD_PARTS = 3
SSD_LANE_GROUPS = SSD_QUANT * SSD_PARTS


def _ssd_expand_matrix():
    e = np.zeros((LANES, SSD_QUANT * WIDTH), np.float32)
    for grp in range(SSD_LANE_GROUPS):
        for hd in range(M2_HEADS):
            c0 = (grp // SSD_PARTS) * WIDTH + hd * M2_HEAD_DIM
            e[grp * M2_HEADS + hd, c0:c0 + M2_HEAD_DIM] = 1.0
    return jnp.asarray(e, BF16)


def _cumsum_rows(x):
    q = x.shape[0]
    row = lax.broadcasted_iota(jnp.int32, x.shape, 0)
    dist = 1
    while dist < q:
        x = x + jnp.where(row >= dist, pltpu.roll(x, dist, 0), 0.0)
        dist *= 2
    return x


def _ssd_kernel(h_ref, wz_ref, wxbc_ref, wdt_ref, cw_ref, cb_ref, dtb_ref, al_ref, e_ref, d_ref, nw_ref,
                o_ref, prev_ref, st_ref, *, rows, layer):
    @pl.when(pl.program_id(1) == 0)
    def _():
        prev_ref[...] = jnp.zeros(prev_ref.shape, F32)
        st_ref[...] = jnp.zeros(st_ref.shape, F32)

    rpart = rows // SPLIT
    hs = [h_ref[sp * rpart:(sp + 1) * rpart, :] for sp in range(SPLIT)]
    raws = [_dot(hh, wxbc_ref[...]) for hh in hs]
    zs = [_dot(hh, wz_ref[...]) for hh in hs]
    dtrs = [_dot(hh, wdt_ref[...]) for hh in hs]
    prev = prev_ref[...]
    xbcs, dts, das = [], [], []
    neg_a = -jnp.exp(al_ref[layer:layer + 1, :])
    for sp in range(SPLIT):
        raw = raws[sp]
        cat = jnp.concatenate([prev, raw], axis=0)
        acc = cb_ref[layer:layer + 1, :] + cw_ref[M2_CONV - 1:M2_CONV, :] * raw
        for j in range(1, M2_CONV):
            k = M2_CONV - 1 - j
            acc = acc + cw_ref[k:k + 1, :] * _shift_rows(cat, j)
        prev = raw[rpart - SUBLANES:rpart, :]
        xbcs.append(_silu(acc))
        dts.append(_softplus(dtrs[sp] + dtb_ref[layer:layer + 1, :]))
        das.append(dts[sp] * neg_a)
    prev_ref[...] = prev
    grp = lax.broadcasted_iota(jnp.int32, (1, LANES), 1) // M2_HEADS
    quant = grp // SSD_PARTS
    part = grp % SSD_PARTS

    q = M2_CHUNK
    hd_w = M2_HEAD_DIM
    causal = lax.broadcasted_iota(jnp.int32, (q, q), 0) >= lax.broadcasted_iota(jnp.int32, (q, q), 1)
    first_head = lax.broadcasted_iota(jnp.int32, (q, LANES), 1) < hd_w
    zero = jnp.zeros((), BF16)
    gw = WIDTH // M2_GROUPS

    for s in range(rows // q):
        sp = (s * q) // rpart
        xbc, z, dt, da = xbcs[sp], zs[sp], dts[sp], das[sp]
        x = xbc[:, 0:WIDTH]
        r = slice(s * q - sp * rpart, (s + 1) * q - sp * rpart)
        cs = _cumsum_rows(da[r])
        cs_row = cs.T[0:M2_HEADS, :]
        cs_last = cs[q - 1:q, :]
        val = jnp.where(quant == 0, dt[r], jnp.where(quant == 1, dt[r] * jnp.exp(cs_last - cs), jnp.exp(cs)))
        p0 = val.astype(BF16)
        r1 = val - p0.astype(F32)
        p1 = r1.astype(BF16)
        p2 = (r1 - p1.astype(F32)).astype(BF16)
        spread = _dot(jnp.where(part == 0, p0, jnp.where(part == 1, p1, p2)), e_ref[...])
        dt_e = spread[:, 0:WIDTH]
        dtdec_e = spread[:, WIDTH:2 * WIDTH]
        exp_cs = spread[:, 2 * WIDTH:3 * WIDTH]
        exp_last = exp_cs[q - 1:q, :]
        x_c = x[r]
        xdt_b = (x_c * dt_e).astype(BF16)
        xdec = (x_c * dtdec_e).astype(BF16)
        ys = []
        for g in range(M2_GROUPS):
            bo = WIDTH + g * M2_STATE
            co = WIDTH + M2_GROUPS * M2_STATE + g * M2_STATE
            bg = xbc[r, bo:bo + M2_STATE].astype(BF16)
            cg = xbc[r, co:co + M2_STATE].astype(BF16)
            scores = _dot_nt(cg, bg)
            state = st_ref[g]
            y_g = _dot(cg, state.astype(BF16)) * exp_cs[:, g * gw:(g + 1) * gw]
            diag = []
            for jp in range(gw // LANES):
                blk = xdt_b[:, g * gw + jp * LANES:g * gw + (jp + 1) * LANES]
                acc2 = None
                for qq in range(2):
                    hd = g * (M2_HEADS // M2_GROUPS) + jp * 2 + qq
                    seg = cs[:, hd:hd + 1] - cs_row[hd:hd + 1, :]
                    decay = jnp.exp(jnp.where(causal, seg, -jnp.inf))
                    m = (scores * decay).astype(BF16)
                    half = jnp.where(first_head, blk, zero) if qq == 0 else jnp.where(first_head, zero, blk)
                    term = _dot(m, half)
                    acc2 = term if acc2 is None else acc2 + term
                diag.append(acc2)
            ys.append(y_g + jnp.concatenate(diag, axis=-1))
            st_ref[g] = state * exp_last[:, g * gw:(g + 1) * gw] + _dot_tn(bg, xdec[:, g * gw:(g + 1) * gw])
        y = jnp.concatenate(ys, axis=-1) + d_ref[layer:layer + 1, :] * x_c
        y = y * _silu(z[r])
        o_ref[s * q:(s + 1) * q, :] = _rms(y, nw_ref[layer:layer + 1, :]).astype(o_ref.dtype)


def _ssd_branch(h, win, wdt_groups, conv_w, conv_b, dtb_groups, al_groups, expand, d_lanes, norm_w, layer):
    b, l, dm = h.shape
    rows = min(SSD_ROWS, l)
    return pl.pallas_call(
        functools.partial(_ssd_kernel, rows=rows, layer=layer),
        grid=(b, l // rows),
        in_specs=[_seq_spec(rows, dm), _in_cols(layer, IN_OFFS[5], WIDTH), _in_cols(layer, IN_OFFS[6], M2_CONV_CH),
                  _of_layer(wdt_groups, layer), _of_layer(conv_w, layer), _full(conv_b.shape),
                  _full(dtb_groups.shape), _full(al_groups.shape), _full(expand.shape), _full(d_lanes.shape),
                  _full(norm_w.shape)],
        out_specs=_seq_spec(rows, WIDTH),
        out_shape=jax.ShapeDtypeStruct((b, l, WIDTH), BF16),
        scratch_shapes=[pltpu.VMEM((SUBLANES, M2_CONV_CH), F32),
                        pltpu.VMEM((M2_GROUPS, M2_STATE, WIDTH // M2_GROUPS), F32)],
        compiler_params=_params("parallel", "arbitrary"),
        name="ssd_branch",
    )(h, win, win, wdt_groups, conv_w, conv_b, dtb_groups, al_groups, expand, d_lanes, norm_w)


def _sc_kernel(h_ref, w_ref, cw_ref, o_ref, prev_ref, *, rows):
    @pl.when(pl.program_id(1) == 0)
    def _():
        prev_ref[...] = jnp.zeros(prev_ref.shape, F32)

    part = rows // SPLIT
    prev = prev_ref[...]
    projs = [_dot(h_ref[sp * part:(sp + 1) * part, :], w_ref[...]) for sp in range(SPLIT)]
    for sp in range(SPLIT):
        proj = projs[sp]
        v = proj[:, WIDTH:2 * WIDTH] * proj[:, 2 * WIDTH:3 * WIDTH]
        cat = jnp.concatenate([prev, v], axis=0)
        acc = cw_ref[SC_CONV - 1:SC_CONV, :] * v
        for j in range(1, SC_CONV):
            k = SC_CONV - 1 - j
            acc = acc + cw_ref[k:k + 1, :] * _shift_rows(cat, j)
        prev = v[part - SUBLANES:part, :]
        o_ref[sp * part:(sp + 1) * part, :] = (
            proj[:, 0:WIDTH] * acc * _silu(proj[:, 3 * WIDTH:4 * WIDTH])).astype(o_ref.dtype)
    prev_ref[...] = prev


def _sc_branch(h, w, conv_w, layer):
    b, l, dm = h.shape
    rows = min(SC_ROWS, l)
    return pl.pallas_call(
        functools.partial(_sc_kernel, rows=rows),
        grid=(b, l // rows),
        in_specs=[_seq_spec(rows, dm), _of_layer(w, layer), _of_layer(conv_w, layer)],
        out_specs=_seq_spec(rows, WIDTH),
        out_shape=jax.ShapeDtypeStruct((b, l, WIDTH), BF16),
        scratch_shapes=[pltpu.VMEM((SUBLANES, WIDTH), F32)],
        compiler_params=_params("parallel", "arbitrary"),
        name="sc_branch",
    )(h, w, conv_w)


def _merge_kernel(x_ref, h_ref, ya_ref, yb_ref, yc_ref, yd_ref, wm_ref, mb_ref, wbr_ref, wo_ref, nw_ref,
                  xo_ref, no_ref, *, norm_row, rows):
    d = D_MODEL
    part = rows // SPLIT
    for sp in range(SPLIT):
        r = slice(sp * part, (sp + 1) * part)
        h = h_ref[r, :]
        merged = None
        for k, y_ref in enumerate((ya_ref, yb_ref, yc_ref, yd_ref)):
            gate = _sigmoid(_dot(h, wm_ref[:, k * d:(k + 1) * d]) + mb_ref[k:k + 1, :])
            term = gate * _dot(y_ref[r, :], wbr_ref[k])
            merged = term if merged is None else merged + term
        xn = x_ref[r, :] + _dot(merged.astype(BF16), wo_ref[...])
        xo_ref[r, :] = xn
        no_ref[r, :] = _rms(xn, nw_ref[norm_row:norm_row + 1, :]).astype(no_ref.dtype)


def _merge(x2d, h2d, ys, wm, mb, wbr, wo, next_norm_w, norm_row, norm_dtype, layer):
    t, d = x2d.shape
    tile = lambda cols: pl.BlockSpec((MERGE_ROWS, cols), lambda i: (i, 0))
    return pl.pallas_call(
        functools.partial(_merge_kernel, norm_row=norm_row, rows=MERGE_ROWS),
        grid=(t // MERGE_ROWS,),
        in_specs=[tile(d), tile(d)] + [tile(WIDTH)] * N_BRANCH
                 + [_of_layer(wm, layer), _of_layer(mb, layer), _of_layer(wbr, layer), _of_layer(wo, layer),
                    _full(next_norm_w.shape)],
        out_specs=(tile(d), tile(d)),
        out_shape=(jax.ShapeDtypeStruct((t, d), F32), jax.ShapeDtypeStruct((t, d), norm_dtype)),
        compiler_params=_params("parallel"),
        name="merge",
    )(x2d, h2d, *ys, wm, mb, wbr, wo, next_norm_w)


def kernel(x, norm_w, w_in, s5_lambda_re, s5_lambda_im, s5_b_re, s5_b_im, s5_c_re, s5_c_im, s5_d,
           s5_log_step, s5_w_glu, sgu_ln_w, sgu_ln_b, sgu_w, sgu_b, m2_conv_w, m2_conv_b, m2_dt_bias,
           m2_a_log, m2_d, m2_norm_w, sc_conv_w, merge_b, w_branch, w_out, final_norm_w):
    b, l, d = x.shape
    t = b * l
    nl = w_in.shape[0]

    win = w_in.astype(BF16)
    w_sc = win[:, :, IN_OFFS[8]:IN_OFFS[12]]
    w_merge = win[:, :, IN_OFFS[12]:IN_OFFS[13]]
    used = SSD_LANE_GROUPS * M2_HEADS
    lane_groups = lambda v: jnp.pad(jnp.tile(v, (1,) * (v.ndim - 1) + (SSD_LANE_GROUPS,)),
                                    ((0, 0),) * (v.ndim - 1) + ((0, LANES - used),))
    wdt_groups = lane_groups(win[:, :, IN_OFFS[7]:IN_OFFS[8]])
    dtb_groups = lane_groups(m2_dt_bias)
    al_groups = lane_groups(m2_a_log)
    d_lanes = jnp.repeat(m2_d, M2_HEAD_DIM, axis=-1)
    expand = _ssd_expand_matrix()
    sgu_bias = jnp.repeat(sgu_b.transpose(0, 2, 1), SGU_HEAD_DIM, axis=-1)
    ab_re, ab_im, s5_wb, s5_cc = _s5_prep(s5_lambda_re, s5_lambda_im, s5_b_re, s5_b_im, s5_c_re, s5_c_im,
                                          s5_log_step)
    s5_dd = s5_d.reshape(nl, WIDTH)
    wglu = s5_w_glu.astype(BF16)
    wbr = w_branch.astype(BF16)
    wo = w_out.astype(BF16)
    norms = jnp.concatenate([norm_w, final_norm_w.reshape(1, d)], axis=0)

    x2d = x.reshape(t, d)
    h2d = _rmsnorm(x2d, norm_w)
    out = None
    for i in range(DEPTH):
        h = h2d.reshape(b, l, d)
        ya = _s5_branch(h, win, s5_wb, s5_cc, s5_dd, wglu, ab_re, ab_im, i)
        yb = _sgu_branch(h, win, sgu_ln_w, sgu_ln_b, sgu_w, sgu_bias, i)
        yc = _ssd_branch(h, win, wdt_groups, m2_conv_w, m2_conv_b, dtb_groups, al_groups, expand, d_lanes,
                         m2_norm_w, i)
        yd = _sc_branch(h, w_sc, sc_conv_w, i)
        last = i == DEPTH - 1
        x2d, h2d = _merge(x2d, h2d, [y.reshape(t, WIDTH) for y in (ya, yb, yc, yd)], w_merge, merge_b, wbr, wo,
                          norms, i + 1, F32 if last else BF16, i)
        out = h2d
    return out.reshape(b, l, d)
```

```python
import functools

import jax
import jax.numpy as jnp
import numpy as np
from jax import lax
from jax.experimental import pallas as pl
from jax.experimental.pallas import tpu as pltpu

F32 = jnp.float32
BF16 = jnp.bfloat16

D_MODEL = 1024
DEPTH = 2
WIDTH = 512
N_BRANCH = 4
EPS = 1e-6
S5_GROUP = 16
S5_GROUPS = WIDTH // S5_GROUP
S5_STATE = 64
S5_PAIRS = S5_GROUPS * S5_STATE
SGU_CHUNK = 128
SGU_HEADS = 8
SGU_HEAD_DIM = WIDTH // SGU_HEADS
M2_HEAD_DIM = 64
M2_HEADS = WIDTH // M2_HEAD_DIM
M2_GROUPS = 2
M2_STATE = 128
M2_CONV = 4
M2_CHUNK = 128
M2_CONV_CH = WIDTH + 2 * M2_GROUPS * M2_STATE
SC_CONV = 3

IN_SIZES = (WIDTH, WIDTH, WIDTH, WIDTH, WIDTH, WIDTH, M2_CONV_CH, M2_HEADS,
            WIDTH, WIDTH, WIDTH, WIDTH, N_BRANCH * D_MODEL)
IN_OFFS = tuple(int(v) for v in np.cumsum((0,) + IN_SIZES))

SUBLANES = 8
LANES = 128
MXU_DIM = 256
VMEM_LIMIT_BYTES = 56 * 1024 * 1024

NORM_ROWS = 1024
S5_STEPS = 128
SGU_ROWS = 1024
SSD_ROWS = 512
MERGE_ROWS = 1024
SPLIT = 2


def _dot(a, b):
    return jnp.dot(a, b, preferred_element_type=F32)


def _dot_nt(a, b):
    return lax.dot_general(a, b, (((1,), (1,)), ((), ())), preferred_element_type=F32)


def _dot_tn(a, b):
    return lax.dot_general(a, b, (((0,), (0,)), ((), ())), preferred_element_type=F32)


def _sigmoid(x):
    return 0.5 * (jnp.tanh(0.5 * x) + 1.0)


def _silu(x):
    hx = 0.5 * x
    return hx * jnp.tanh(hx) + hx


def _gelu(x):
    c = np.float32(np.sqrt(2.0 / np.pi))
    return x * (0.5 * (1.0 + jnp.tanh(c * (x + 0.044715 * (x * x * x)))))


def _softplus(x):
    return jnp.maximum(x, 0.0) + jnp.log1p(jnp.exp(-jnp.abs(x)))


def _rms(x, w):
    return x * lax.rsqrt(jnp.mean(x * x, axis=-1, keepdims=True) + EPS) * w


def _shift_rows(cat, j):
    n8, c = cat.shape
    rot = pltpu.roll(cat.reshape(n8 // SUBLANES, SUBLANES, c), j, 1)
    sub = lax.broadcasted_iota(jnp.int32, (1, SUBLANES, c), 1)
    out = jnp.where(sub >= j, rot[1:], rot[:-1])
    return out.reshape(n8 - SUBLANES, c)


def _params(*semantics):
    return pltpu.CompilerParams(dimension_semantics=semantics, vmem_limit_bytes=VMEM_LIMIT_BYTES)


RESIDENT = pl.Buffered(1)


def _full(shape):
    zeros = (0,) * len(shape)
    return pl.BlockSpec(tuple(shape), lambda *_: zeros, pipeline_mode=RESIDENT)


def _of_layer(arr, layer):
    zeros = (0,) * (arr.ndim - 1)
    return pl.BlockSpec((None,) + tuple(arr.shape[1:]), lambda *_: (layer,) + zeros, pipeline_mode=RESIDENT)


def _in_cols(layer, first, width):
    assert first % width == 0
    return pl.BlockSpec((None, D_MODEL, width), lambda *_: (layer, 0, first // width), pipeline_mode=RESIDENT)


def _seq_spec(rows, cols):
    return pl.BlockSpec((None, rows, cols), lambda b, c: (b, c, 0))


def _rmsnorm_kernel(x_ref, w_ref, o_ref):
    o_ref[...] = _rms(x_ref[...], w_ref[0:1, :]).astype(o_ref.dtype)


def _rmsnorm(x2d, norm_w):
    t, d = x2d.shape
    return pl.pallas_call(
        _rmsnorm_kernel,
        grid=(t // NORM_ROWS,),
        in_specs=[pl.BlockSpec((NORM_ROWS, d), lambda i: (i, 0)), _full(norm_w.shape)],
        out_specs=pl.BlockSpec((NORM_ROWS, d), lambda i: (i, 0)),
        out_shape=jax.ShapeDtypeStruct((t, d), BF16),
        compiler_params=_params("parallel"),
        name="rmsnorm",
    )(x2d, norm_w)


def _s5_prep_kernel(lr_ref, li_ref, ls_ref, bre_ref, bim_ref, abre_ref, abim_ref, bbre_ref, bbim_ref):
    lr, li = lr_ref[...], li_ref[...]
    step = jnp.exp(ls_ref[...])
    mag = jnp.exp(lr * step)
    ang = li * step
    ab_re, ab_im = mag * jnp.cos(ang), mag * jnp.sin(ang)
    abre_ref[...] = ab_re
    abim_ref[...] = ab_im
    den = lr * lr + li * li
    nr = ab_re - 1.0
    coef_re = (nr * lr + ab_im * li) / den
    coef_im = (ab_im * lr - nr * li) / den
    for layer in range(lr.shape[0]):
        cr, ci = coef_re[layer:layer + 1, :], coef_im[layer:layer + 1, :]
        br, bi = bre_ref[layer], bim_ref[layer]
        bbre_ref[layer] = cr * br - ci * bi
        bbim_ref[layer] = cr * bi + ci * br


def _s5_prep(lam_re, lam_im, b_re, b_im, c_re, c_im, log_step):
    nl = lam_re.shape[0]
    g, n, p = S5_GROUPS, S5_STATE, S5_GROUP
    ls = jnp.repeat(log_step, n, axis=-1)
    row = lambda v: v.reshape(nl, g * n)
    chan_minor = lambda v: v.transpose(0, 3, 1, 2).reshape(nl, p, g * n)
    ab_re, ab_im, bb_re, bb_im = pl.pallas_call(
        _s5_prep_kernel,
        out_shape=(jax.ShapeDtypeStruct((nl, g * n), F32), jax.ShapeDtypeStruct((nl, g * n), F32),
                   jax.ShapeDtypeStruct((nl, p, g * n), F32), jax.ShapeDtypeStruct((nl, p, g * n), F32)),
        name="s5_prep",
    )(row(lam_re), row(lam_im), row(ls), chan_minor(b_re), chan_minor(b_im))

    gh = g // 2
    in_mask = (np.arange(gh * p)[:, None] // p) == (np.arange(gh * n)[None, :] // n)
    out_mask = in_mask.T

    def in_half(bb):
        t = bb.reshape(nl, p, 2, gh, n).transpose(0, 2, 3, 1, 4).reshape(nl, 2, gh * p, n)
        return jnp.where(in_mask, jnp.tile(t, (1, 1, 1, gh)), 0.0)

    def out_half(c):
        t = c.reshape(nl, 2, gh, p, n).transpose(0, 1, 2, 4, 3).reshape(nl, 2, gh * n, p)
        return jnp.where(out_mask, jnp.tile(t, (1, 1, 1, gh)), 0.0)

    wb = jnp.concatenate([in_half(bb_re), in_half(bb_im)], axis=3).astype(BF16)
    cc = jnp.concatenate([out_half(c_re), -out_half(c_im)], axis=2).astype(BF16)
    return ab_re, ab_im, wb, cc


S5_HALF = S5_PAIRS
S5_PERM = MXU_DIM


def _perm_time_major(nb):
    tt = S5_PERM // nb
    i = lax.broadcasted_iota(jnp.int32, (S5_PERM, S5_PERM), 0)
    j = lax.broadcasted_iota(jnp.int32, (S5_PERM, S5_PERM), 1)
    return jnp.where(j == (i % nb) * tt + i // nb, 1.0, 0.0).astype(BF16)


def _s5_kernel(h_ref, wug_ref, wb_ref, cc_ref, d_ref, wglu_ref, abre_ref, abim_ref, o_ref,
               hp_ref, ug_ref, st_ref, carry_ref, *, steps, nb, layer):
    @pl.when(pl.program_id(0) == 0)
    def _():
        carry_ref[...] = jnp.zeros(carry_ref.shape, F32)

    tt = S5_PERM // nb
    perm = _perm_time_major(nb)
    for k in range(steps // tt):
        blk = h_ref[:, k * tt:(k + 1) * tt, :].reshape(S5_PERM, D_MODEL)
        hp_ref[k * S5_PERM:(k + 1) * S5_PERM, :] = _dot(perm, blk).astype(BF16)

    ug_ref[...] = _dot(hp_ref[...], wug_ref[...])
    ub = ug_ref[:, 0:WIDTH].astype(BF16)
    hw = WIDTH // 2
    for hf in range(2):
        st_ref[:, hf * S5_HALF:(hf + 1) * S5_HALF] = _dot(ub[:, hf * hw:(hf + 1) * hw], wb_ref[hf])

    pairs_half = S5_HALF // 2
    for hf in range(2):
        cre = hf * S5_HALF
        cim = cre + pairs_half
        tc = hf * pairs_half
        ar = jnp.broadcast_to(abre_ref[layer:layer + 1, tc:tc + pairs_half], (nb, pairs_half))
        ai = jnp.broadcast_to(abim_ref[layer:layer + 1, tc:tc + pairs_half], (nb, pairs_half))

        def step(t, carry, cre=cre, cim=cim, ar=ar, ai=ai):
            sr, si = carry
            r0 = pl.multiple_of(t * nb, nb)
            nr = (ar * sr - ai * si) + st_ref[pl.ds(r0, nb), cre:cre + pairs_half]
            ni = (ar * si + ai * sr) + st_ref[pl.ds(r0, nb), cim:cim + pairs_half]
            st_ref[pl.ds(r0, nb), cre:cre + pairs_half] = nr
            st_ref[pl.ds(r0, nb), cim:cim + pairs_half] = ni
            return nr, ni

        sr, si = lax.fori_loop(
            0, steps, step,
            (carry_ref[:, cre:cre + pairs_half], carry_ref[:, cim:cim + pairs_half]), unroll=True)
        carry_ref[:, cre:cre + pairs_half] = sr
        carry_ref[:, cim:cim + pairs_half] = si

    ys = []
    for hf in range(2):
        ys.append(_dot(st_ref[:, hf * S5_HALF:(hf + 1) * S5_HALF].astype(BF16), cc_ref[hf]))
    y = jnp.concatenate(ys, axis=-1) + d_ref[layer:layer + 1, :] * ug_ref[:, 0:WIDTH]
    y = _gelu(y)
    y = y * _sigmoid(_dot(y.astype(BF16), wglu_ref[...]))
    y = (y * _silu(ug_ref[:, WIDTH:2 * WIDTH])).astype(BF16)
    for k in range(steps // tt):
        back = _dot_tn(perm, y[k * S5_PERM:(k + 1) * S5_PERM, :])
        o_ref[:, k * tt:(k + 1) * tt, :] = back.reshape(nb, tt, WIDTH).astype(o_ref.dtype)


def _s5_branch(h, win, wb, cc, d, wglu, ab_re, ab_im, layer):
    b, l, dm = h.shape
    assert S5_PERM % b == 0 and b % SUBLANES == 0
    steps = min(S5_STEPS, l)
    rows = steps * b
    return pl.pallas_call(
        functools.partial(_s5_kernel, steps=steps, nb=b, layer=layer),
        grid=(l // steps,),
        in_specs=[pl.BlockSpec((b, steps, dm), lambda c: (0, c, 0)), _in_cols(layer, IN_OFFS[0], 2 * WIDTH),
                  _of_layer(wb, layer), _of_layer(cc, layer), _full(d.shape), _of_layer(wglu, layer),
                  _full(ab_re.shape), _full(ab_im.shape)],
        out_specs=pl.BlockSpec((b, steps, WIDTH), lambda c: (0, c, 0)),
        out_shape=jax.ShapeDtypeStruct((b, l, WIDTH), BF16),
        scratch_shapes=[pltpu.VMEM((rows, dm), BF16),
                        pltpu.VMEM((rows, 2 * WIDTH), F32),
                        pltpu.VMEM((rows, 2 * S5_HALF), F32),
                        pltpu.VMEM((b, 2 * S5_HALF), F32)],
        compiler_params=_params("arbitrary"),
        name="s5_branch",
    )(h, win, wb, cc, d, wglu, ab_re, ab_im)


def _sgu_kernel(h_ref, wu_ref, wv_ref, wg_ref, lnw_ref, lnb_ref, ws_ref, bias_ref, o_ref, *, rows, layer):
    t = SGU_CHUNK
    causal = lax.broadcasted_iota(jnp.int32, (t, t), 0) >= lax.broadcasted_iota(jnp.int32, (t, t), 1)
    first_head = lax.broadcasted_iota(jnp.int32, (t, LANES), 1) < SGU_HEAD_DIM
    wm = [jnp.where(causal, ws_ref[hd], 0.0).astype(BF16) for hd in range(SGU_HEADS)]
    zero = jnp.zeros((), BF16)
    part = rows // SPLIT
    for sp in range(SPLIT):
        h = h_ref[sp * part:(sp + 1) * part, :]
        v = _gelu(_dot(h, wv_ref[...]))
        mu = jnp.mean(v, axis=-1, keepdims=True)
        vc = v - mu
        var = jnp.mean(vc * vc, axis=-1, keepdims=True)
        vn = (vc * lax.rsqrt(var + EPS) * lnw_ref[layer:layer + 1, :] + lnb_ref[layer:layer + 1, :]).astype(BF16)
        ug = _gelu(_dot(h, wu_ref[...])) * _silu(_dot(h, wg_ref[...]))
        for c in range(part // t):
            r = slice(c * t, (c + 1) * t)
            outs = []
            for j in range(WIDTH // LANES):
                blk = vn[r, j * LANES:(j + 1) * LANES]
                outs.append(_dot(wm[2 * j], jnp.where(first_head, blk, zero))
                            + _dot(wm[2 * j + 1], jnp.where(first_head, zero, blk)))
            s = jnp.concatenate(outs, axis=-1) + bias_ref[...]
            o_ref[sp * part + c * t:sp * part + (c + 1) * t, :] = (ug[r] * s).astype(o_ref.dtype)


def _sgu_branch(h, win, ln_w, ln_b, w_s, bias, layer):
    b, l, dm = h.shape
    rows = min(SGU_ROWS, l)
    return pl.pallas_call(
        functools.partial(_sgu_kernel, rows=rows, layer=layer),
        grid=(b, l // rows),
        in_specs=[_seq_spec(rows, dm)] + [_in_cols(layer, IN_OFFS[k], WIDTH) for k in (2, 3, 4)]
                 + [_full(ln_w.shape), _full(ln_b.shape), _of_layer(w_s, layer), _of_layer(bias, layer)],
        out_specs=_seq_spec(rows, WIDTH),
        out_shape=jax.ShapeDtypeStruct((b, l, WIDTH), BF16),
        compiler_params=_params("parallel", "parallel"),
        name="sgu_branch",
    )(h, win, win, win, ln_w, ln_b, w_s, bias)


SSD_QUANT = 3
SSD_PARTS = 3
SSD_LANE_GROUPS = SSD_QUANT * SSD_PARTS


def _ssd_expand_matrix():
    e = np.zeros((LANES, SSD_QUANT * WIDTH), np.float32)
    for grp in range(SSD_LANE_GROUPS):
        for hd in range(M2_HEADS):
            c0 = (grp // SSD_PARTS) * WIDTH + hd * M2_HEAD_DIM
            e[grp * M2_HEADS + hd, c0:c0 + M2_HEAD_DIM] = 1.0
    return jnp.asarray(e, BF16)


def _cumsum_rows(x):
    q = x.shape[0]
    row = lax.broadcasted_iota(jnp.int32, x.shape, 0)
    dist = 1
    while dist < q:
        x = x + jnp.where(row >= dist, pltpu.roll(x, dist, 0), 0.0)
        dist *= 2
    return x


def _ssd_sc_kernel(h_ref, wz_ref, wxbc_ref, wdt_ref, cw_ref, cb_ref, dtb_ref, al_ref, e_ref, d_ref, nw_ref,
                   wsc_ref, scw_ref, o_ref, osc_ref, prev_ref, st_ref, scprev_ref, *, rows, layer):
    @pl.when(pl.program_id(1) == 0)
    def _():
        prev_ref[...] = jnp.zeros(prev_ref.shape, F32)
        st_ref[...] = jnp.zeros(st_ref.shape, F32)
        scprev_ref[...] = jnp.zeros(scprev_ref.shape, F32)

    sc_proj = lambda j: _dot(h_ref[...], wsc_ref[:, j * WIDTH:(j + 1) * WIDTH])

    rpart = rows // SPLIT
    hs = [h_ref[sp * rpart:(sp + 1) * rpart, :] for sp in range(SPLIT)]
    raws = [_dot(hh, wxbc_ref[...]) for hh in hs]
    zs = [_dot(hh, wz_ref[...]) for hh in hs]
    dtrs = [_dot(hh, wdt_ref[...]) for hh in hs]
    sc_v = sc_proj(1) * sc_proj(2)
    prev = prev_ref[...]
    xbcs, dts, das = [], [], []
    neg_a = -jnp.exp(al_ref[layer:layer + 1, :])
    for sp in range(SPLIT):
        raw = raws[sp]
        cat = jnp.concatenate([prev, raw], axis=0)
        acc = cb_ref[layer:layer + 1, :] + cw_ref[M2_CONV - 1:M2_CONV, :] * raw
        for j in range(1, M2_CONV):
            k = M2_CONV - 1 - j
            acc = acc + cw_ref[k:k + 1, :] * _shift_rows(cat, j)
        prev = raw[rpart - SUBLANES:rpart, :]
        xbcs.append(_silu(acc))
        dts.append(_softplus(dtrs[sp] + dtb_ref[layer:layer + 1, :]))
        das.append(dts[sp] * neg_a)
    prev_ref[...] = prev
    grp = lax.broadcasted_iota(jnp.int32, (1, LANES), 1) // M2_HEADS
    quant = grp // SSD_PARTS
    part = grp % SSD_PARTS

    q = M2_CHUNK
    hd_w = M2_HEAD_DIM
    causal = lax.broadcasted_iota(jnp.int32, (q, q), 0) >= lax.broadcasted_iota(jnp.int32, (q, q), 1)
    first_head = lax.broadcasted_iota(jnp.int32, (q, LANES), 1) < hd_w
    zero = jnp.zeros((), BF16)
    gw = WIDTH // M2_GROUPS

    def chunk(s):
        sp = (s * q) // rpart
        xbc, z, dt, da = xbcs[sp], zs[sp], dts[sp], das[sp]
        x = xbc[:, 0:WIDTH]
        r = slice(s * q - sp * rpart, (s + 1) * q - sp * rpart)
        cs = _cumsum_rows(da[r])
        cs_row = cs.T[0:M2_HEADS, :]
        cs_last = cs[q - 1:q, :]
        val = jnp.where(quant == 0, dt[r], jnp.where(quant == 1, dt[r] * jnp.exp(cs_last - cs), jnp.exp(cs)))
        p0 = val.astype(BF16)
        r1 = val - p0.astype(F32)
        p1 = r1.astype(BF16)
        p2 = (r1 - p1.astype(F32)).astype(BF16)
        spread = _dot(jnp.where(part == 0, p0, jnp.where(part == 1, p1, p2)), e_ref[...])
        dt_e = spread[:, 0:WIDTH]
        dtdec_e = spread[:, WIDTH:2 * WIDTH]
        exp_cs = spread[:, 2 * WIDTH:3 * WIDTH]
        exp_last = exp_cs[q - 1:q, :]
        x_c = x[r]
        xdt_b = (x_c * dt_e).astype(BF16)
        xdec = (x_c * dtdec_e).astype(BF16)
        ys = []
        for g in range(M2_GROUPS):
            bo = WIDTH + g * M2_STATE
            co = WIDTH + M2_GROUPS * M2_STATE + g * M2_STATE
            bg = xbc[r, bo:bo + M2_STATE].astype(BF16)
            cg = xbc[r, co:co + M2_STATE].astype(BF16)
            scores = _dot_nt(cg, bg)
            state = st_ref[g]
            y_g = _dot(cg, state.astype(BF16)) * exp_cs[:, g * gw:(g + 1) * gw]
            diag = []
            for jp in range(gw // LANES):
                blk = xdt_b[:, g * gw + jp * LANES:g * gw + (jp + 1) * LANES]
                acc2 = None
                for qq in range(2):
                    hd = g * (M2_HEADS // M2_GROUPS) + jp * 2 + qq
                    seg = cs[:, hd:hd + 1] - cs_row[hd:hd + 1, :]
                    decay = jnp.exp(jnp.where(causal, seg, -jnp.inf))
                    m = (scores * decay).astype(BF16)
                    half = jnp.where(first_head, blk, zero) if qq == 0 else jnp.where(first_head, zero, blk)
                    term = _dot(m, half)
                    acc2 = term if acc2 is None else acc2 + term
                diag.append(acc2)
            ys.append(y_g + jnp.concatenate(diag, axis=-1))
            st_ref[g] = state * exp_last[:, g * gw:(g + 1) * gw] + _dot_tn(bg, xdec[:, g * gw:(g + 1) * gw])
        y = jnp.concatenate(ys, axis=-1) + d_ref[layer:layer + 1, :] * x_c
        y = y * _silu(z[r])
        o_ref[s * q:(s + 1) * q, :] = _rms(y, nw_ref[layer:layer + 1, :]).astype(o_ref.dtype)

    def sc_tail(sc_b, sc_gate):
        cat = jnp.concatenate([scprev_ref[...], sc_v], axis=0)
        acc = scw_ref[SC_CONV - 1:SC_CONV, :] * sc_v
        for j in range(1, SC_CONV):
            k = SC_CONV - 1 - j
            acc = acc + scw_ref[k:k + 1, :] * _shift_rows(cat, j)
        scprev_ref[...] = sc_v[rows - SUBLANES:rows, :]
        osc_ref[...] = (sc_b * acc * _silu(sc_gate)).astype(osc_ref.dtype)

    n_chunks = rows // q
    sc_b = sc_gate = None
    for s in range(n_chunks):
        chunk(s)
        if s == 0:
            sc_b = sc_proj(0)
        if s == min(1, n_chunks - 1):
            sc_gate = sc_proj(3)
        if s == min(2, n_chunks - 1):
            sc_tail(sc_b, sc_gate)


def _ssd_sc_branch(h, win, wdt_groups, conv_w, conv_b, dtb_groups, al_groups, expand, d_lanes, norm_w,
                   w_sc, sc_conv_w, layer):
    b, l, dm = h.shape
    rows = min(SSD_ROWS, l)
    out = jax.ShapeDtypeStruct((b, l, WIDTH), BF16)
    return pl.pallas_call(
        functools.partial(_ssd_sc_kernel, rows=rows, layer=layer),
        grid=(b, l // rows),
        in_specs=[_seq_spec(rows, dm), _in_cols(layer, IN_OFFS[5], WIDTH), _in_cols(layer, IN_OFFS[6], M2_CONV_CH),
                  _of_layer(wdt_groups, layer), _of_layer(conv_w, layer), _full(conv_b.shape),
                  _full(dtb_groups.shape), _full(al_groups.shape), _full(expand.shape), _full(d_lanes.shape),
                  _full(norm_w.shape), _of_layer(w_sc, layer), _of_layer(sc_conv_w, layer)],
        out_specs=(_seq_spec(rows, WIDTH), _seq_spec(rows, WIDTH)),
        out_shape=(out, out),
        scratch_shapes=[pltpu.VMEM((SUBLANES, M2_CONV_CH), F32),
                        pltpu.VMEM((M2_GROUPS, M2_STATE, WIDTH // M2_GROUPS), F32),
                        pltpu.VMEM((SUBLANES, WIDTH), F32)],
        compiler_params=_params("parallel", "arbitrary"),
        name="ssd_sc_branch",
    )(h, win, win, wdt_groups, conv_w, conv_b, dtb_groups, al_groups, expand, d_lanes, norm_w, w_sc, sc_conv_w)


def _merge_kernel(x_ref, h_ref, ya_ref, yb_ref, yc_ref, yd_ref, wm_ref, mb_ref, wbr_ref, wo_ref, nw_ref,
                  xo_ref, no_ref, *, norm_row, rows):
    d = D_MODEL
    part = rows // SPLIT
    for sp in range(SPLIT):
        r = slice(sp * part, (sp + 1) * part)
        h = h_ref[r, :]
        merged = None
        for k, y_ref in enumerate((ya_ref, yb_ref, yc_ref, yd_ref)):
            gate = _sigmoid(_dot(h, wm_ref[:, k * d:(k + 1) * d]) + mb_ref[k:k + 1, :])
            term = gate * _dot(y_ref[r, :], wbr_ref[k])
            merged = term if merged is None else merged + term
        xn = x_ref[r, :] + _dot(merged.astype(BF16), wo_ref[...])
        xo_ref[r, :] = xn
        no_ref[r, :] = _rms(xn, nw_ref[norm_row:norm_row + 1, :]).astype(no_ref.dtype)


def _merge(x2d, h2d, ys, wm, mb, wbr, wo, next_norm_w, norm_row, norm_dtype, layer):
    t, d = x2d.shape
    tile = lambda cols: pl.BlockSpec((MERGE_ROWS, cols), lambda i: (i, 0))
    return pl.pallas_call(
        functools.partial(_merge_kernel, norm_row=norm_row, rows=MERGE_ROWS),
        grid=(t // MERGE_ROWS,),
        in_specs=[tile(d), tile(d)] + [tile(WIDTH)] * N_BRANCH
                 + [_of_layer(wm, layer), _of_layer(mb, layer), _of_layer(wbr, layer), _of_layer(wo, layer),
                    _full(next_norm_w.shape)],
        out_specs=(tile(d), tile(d)),
        out_shape=(jax.ShapeDtypeStruct((t, d), F32), jax.ShapeDtypeStruct((t, d), norm_dtype)),
        compiler_params=_params("parallel"),
        name="merge",
    )(x2d, h2d, *ys, wm, mb, wbr, wo, next_norm_w)


def kernel(x, norm_w, w_in, s5_lambda_re, s5_lambda_im, s5_b_re, s5_b_im, s5_c_re, s5_c_im, s5_d,
           s5_log_step, s5_w_glu, sgu_ln_w, sgu_ln_b, sgu_w, sgu_b, m2_conv_w, m2_conv_b, m2_dt_bias,
           m2_a_log, m2_d, m2_norm_w, sc_conv_w, merge_b, w_branch, w_out, final_norm_w):
    b, l, d = x.shape
    t = b * l
    nl = w_in.shape[0]

    win = w_in.astype(BF16)
    w_sc = win[:, :, IN_OFFS[8]:IN_OFFS[12]]
    w_merge = win[:, :, IN_OFFS[12]:IN_OFFS[13]]
    used = SSD_LANE_GROUPS * M2_HEADS
    lane_groups = lambda v: jnp.pad(jnp.tile(v, (1,) * (v.ndim - 1) + (SSD_LANE_GROUPS,)),
                                    ((0, 0),) * (v.ndim - 1) + ((0, LANES - used),))
    wdt_groups = lane_groups(win[:, :, IN_OFFS[7]:IN_OFFS[8]])
    dtb_groups = lane_groups(m2_dt_bias)
    al_groups = lane_groups(m2_a_log)
    d_lanes = jnp.repeat(m2_d, M2_HEAD_DIM, axis=-1)
    expand = _ssd_expand_matrix()
    sgu_bias = jnp.repeat(sgu_b.transpose(0, 2, 1), SGU_HEAD_DIM, axis=-1)
    ab_re, ab_im, s5_wb, s5_cc = _s5_prep(s5_lambda_re, s5_lambda_im, s5_b_re, s5_b_im, s5_c_re, s5_c_im,
                                          s5_log_step)
    s5_dd = s5_d.reshape(nl, WIDTH)
    wglu = s5_w_glu.astype(BF16)
    wbr = w_branch.astype(BF16)
    wo = w_out.astype(BF16)
    norms = jnp.concatenate([norm_w, final_norm_w.reshape(1, d)], axis=0)

    x2d = x.reshape(t, d)
    h2d = _rmsnorm(x2d, norm_w)
    out = None
    for i in range(DEPTH):
        h = h2d.reshape(b, l, d)
        ya = _s5_branch(h, win, s5_wb, s5_cc, s5_dd, wglu, ab_re, ab_im, i)
        yb = _sgu_branch(h, win, sgu_ln_w, sgu_ln_b, sgu_w, sgu_bias, i)
        yc, yd = _ssd_sc_branch(h, win, wdt_groups, m2_conv_w, m2_conv_b, dtb_groups, al_groups, expand, d_lanes,
                                m2_norm_w, w_sc, sc_conv_w, i)
        last = i == DEPTH - 1
        x2d, h2d = _merge(x2d, h2d, [y.reshape(t, WIDTH) for y in (ya, yb, yc, yd)], w_merge, merge_b, wbr, wo,
                          norms, i + 1, F32 if last else BF16, i)
        out = h2d
    return out.reshape(b, l, d)
```

```python
import functools

import jax
import jax.numpy as jnp
import numpy as np
from jax import lax
from jax.experimental import pallas as pl
from jax.experimental.pallas import tpu as pltpu

F32 = jnp.float32
BF16 = jnp.bfloat16

D_MODEL = 1024
DEPTH = 2
WIDTH = 512
N_BRANCH = 4
EPS = 1e-6
S5_GROUP = 16
S5_GROUPS = WIDTH // S5_GROUP
S5_STATE = 64
S5_PAIRS = S5_GROUPS * S5_STATE
SGU_CHUNK = 128
SGU_HEADS = 8
SGU_HEAD_DIM = WIDTH // SGU_HEADS
M2_HEAD_DIM = 64
M2_HEADS = WIDTH // M2_HEAD_DIM
M2_GROUPS = 2
M2_STATE = 128
M2_CONV = 4
M2_CHUNK = 128
M2_CONV_CH = WIDTH + 2 * M2_GROUPS * M2_STATE
SC_CONV = 3

IN_SIZES = (WIDTH, WIDTH, WIDTH, WIDTH, WIDTH, WIDTH, M2_CONV_CH, M2_HEADS,
            WIDTH, WIDTH, WIDTH, WIDTH, N_BRANCH * D_MODEL)
IN_OFFS = tuple(int(v) for v in np.cumsum((0,) + IN_SIZES))

SUBLANES = 8
LANES = 128
MXU_DIM = 256
VMEM_LIMIT_BYTES = 56 * 1024 * 1024

NORM_ROWS = 1024
S5_STEPS = 128
SGU_ROWS = 1024
SSD_ROWS = 1024
MERGE_ROWS = 1024
SPLIT = 2


def _dot(a, b):
    return jnp.dot(a, b, preferred_element_type=F32)


def _dot_nt(a, b):
    return lax.dot_general(a, b, (((1,), (1,)), ((), ())), preferred_element_type=F32)


def _dot_tn(a, b):
    return lax.dot_general(a, b, (((0,), (0,)), ((), ())), preferred_element_type=F32)


def _sigmoid(x):
    return 0.5 * (jnp.tanh(0.5 * x) + 1.0)


def _silu(x):
    hx = 0.5 * x
    return hx * jnp.tanh(hx) + hx


def _gelu(x):
    c = np.float32(np.sqrt(2.0 / np.pi))
    return x * (0.5 * (1.0 + jnp.tanh(c * (x + 0.044715 * (x * x * x)))))


def _softplus(x):
    return jnp.maximum(x, 0.0) + jnp.log1p(jnp.exp(-jnp.abs(x)))


def _rms(x, w):
    return x * lax.rsqrt(jnp.mean(x * x, axis=-1, keepdims=True) + EPS) * w


def _shift_rows(cat, j):
    n8, c = cat.shape
    rot = pltpu.roll(cat.reshape(n8 // SUBLANES, SUBLANES, c), j, 1)
    sub = lax.broadcasted_iota(jnp.int32, (1, SUBLANES, c), 1)
    out = jnp.where(sub >= j, rot[1:], rot[:-1])
    return out.reshape(n8 - SUBLANES, c)


def _params(*semantics):
    return pltpu.CompilerParams(dimension_semantics=semantics, vmem_limit_bytes=VMEM_LIMIT_BYTES)


RESIDENT = pl.Buffered(1)


def _full(shape):
    zeros = (0,) * len(shape)
    return pl.BlockSpec(tuple(shape), lambda *_: zeros, pipeline_mode=RESIDENT)


def _of_layer(arr, layer):
    zeros = (0,) * (arr.ndim - 1)
    return pl.BlockSpec((None,) + tuple(arr.shape[1:]), lambda *_: (layer,) + zeros, pipeline_mode=RESIDENT)


def _in_cols(layer, first, width):
    assert first % width == 0
    return pl.BlockSpec((None, D_MODEL, width), lambda *_: (layer, 0, first // width), pipeline_mode=RESIDENT)


def _seq_spec(rows, cols):
    return pl.BlockSpec((None, rows, cols), lambda b, c: (b, c, 0))


def _rmsnorm_kernel(x_ref, w_ref, o_ref):
    o_ref[...] = _rms(x_ref[...], w_ref[0:1, :]).astype(o_ref.dtype)


def _rmsnorm(x2d, norm_w):
    t, d = x2d.shape
    return pl.pallas_call(
        _rmsnorm_kernel,
        grid=(t // NORM_ROWS,),
        in_specs=[pl.BlockSpec((NORM_ROWS, d), lambda i: (i, 0)), _full(norm_w.shape)],
        out_specs=pl.BlockSpec((NORM_ROWS, d), lambda i: (i, 0)),
        out_shape=jax.ShapeDtypeStruct((t, d), BF16),
        compiler_params=_params("parallel"),
        name="rmsnorm",
    )(x2d, norm_w)


def _s5_prep_kernel(lr_ref, li_ref, ls_ref, bre_ref, bim_ref, abre_ref, abim_ref, bbre_ref, bbim_ref):
    lr, li = lr_ref[...], li_ref[...]
    step = jnp.exp(ls_ref[...])
    mag = jnp.exp(lr * step)
    ang = li * step
    ab_re, ab_im = mag * jnp.cos(ang), mag * jnp.sin(ang)
    abre_ref[...] = ab_re
    abim_ref[...] = ab_im
    den = lr * lr + li * li
    nr = ab_re - 1.0
    coef_re = (nr * lr + ab_im * li) / den
    coef_im = (ab_im * lr - nr * li) / den
    for layer in range(lr.shape[0]):
        cr, ci = coef_re[layer:layer + 1, :], coef_im[layer:layer + 1, :]
        br, bi = bre_ref[layer], bim_ref[layer]
        bbre_ref[layer] = cr * br - ci * bi
        bbim_ref[layer] = cr * bi + ci * br


def _s5_prep(lam_re, lam_im, b_re, b_im, c_re, c_im, log_step):
    nl = lam_re.shape[0]
    g, n, p = S5_GROUPS, S5_STATE, S5_GROUP
    ls = jnp.repeat(log_step, n, axis=-1)
    row = lambda v: v.reshape(nl, g * n)
    chan_minor = lambda v: v.transpose(0, 3, 1, 2).reshape(nl, p, g * n)
    ab_re, ab_im, bb_re, bb_im = pl.pallas_call(
        _s5_prep_kernel,
        out_shape=(jax.ShapeDtypeStruct((nl, g * n), F32), jax.ShapeDtypeStruct((nl, g * n), F32),
                   jax.ShapeDtypeStruct((nl, p, g * n), F32), jax.ShapeDtypeStruct((nl, p, g * n), F32)),
        name="s5_prep",
    )(row(lam_re), row(lam_im), row(ls), chan_minor(b_re), chan_minor(b_im))

    gh = g // 2
    in_mask = (np.arange(gh * p)[:, None] // p) == (np.arange(gh * n)[None, :] // n)
    out_mask = in_mask.T

    def in_half(bb):
        t = bb.reshape(nl, p, 2, gh, n).transpose(0, 2, 3, 1, 4).reshape(nl, 2, gh * p, n)
        return jnp.where(in_mask, jnp.tile(t, (1, 1, 1, gh)), 0.0)

    def out_half(c):
        t = c.reshape(nl, 2, gh, p, n).transpose(0, 1, 2, 4, 3).reshape(nl, 2, gh * n, p)
        return jnp.where(out_mask, jnp.tile(t, (1, 1, 1, gh)), 0.0)

    wb = jnp.concatenate([in_half(bb_re), in_half(bb_im)], axis=3).astype(BF16)
    cc = jnp.concatenate([out_half(c_re), -out_half(c_im)], axis=2).astype(BF16)
    return ab_re, ab_im, wb, cc


S5_HALF = S5_PAIRS
S5_PERM = MXU_DIM


def _perm_time_major(nb):
    tt = S5_PERM // nb
    i = lax.broadcasted_iota(jnp.int32, (S5_PERM, S5_PERM), 0)
    j = lax.broadcasted_iota(jnp.int32, (S5_PERM, S5_PERM), 1)
    return jnp.where(j == (i % nb) * tt + i // nb, 1.0, 0.0).astype(BF16)


def _s5_kernel(h_ref, wug_ref, wb_ref, cc_ref, d_ref, wglu_ref, abre_ref, abim_ref, o_ref,
               hp_ref, ug_ref, st_ref, carry_ref, *, steps, nb, layer):
    @pl.when(pl.program_id(0) == 0)
    def _():
        carry_ref[...] = jnp.zeros(carry_ref.shape, F32)

    tt = S5_PERM // nb
    perm = _perm_time_major(nb)
    for k in range(steps // tt):
        blk = h_ref[:, k * tt:(k + 1) * tt, :].reshape(S5_PERM, D_MODEL)
        hp_ref[k * S5_PERM:(k + 1) * S5_PERM, :] = _dot(perm, blk).astype(BF16)

    ug_ref[:, 0:WIDTH] = _dot(hp_ref[...], wug_ref[:, 0:WIDTH])
    ub = ug_ref[:, 0:WIDTH].astype(BF16)
    hw = WIDTH // 2
    for hf in range(2):
        st_ref[:, hf * S5_HALF:(hf + 1) * S5_HALF] = _dot(ub[:, hf * hw:(hf + 1) * hw], wb_ref[hf])
    ug_ref[:, WIDTH:2 * WIDTH] = _dot(hp_ref[...], wug_ref[:, WIDTH:2 * WIDTH])

    pairs_half = S5_HALF // 2
    for hf in range(2):
        cre = hf * S5_HALF
        cim = cre + pairs_half
        tc = hf * pairs_half
        ar = jnp.broadcast_to(abre_ref[layer:layer + 1, tc:tc + pairs_half], (nb, pairs_half))
        ai = jnp.broadcast_to(abim_ref[layer:layer + 1, tc:tc + pairs_half], (nb, pairs_half))

        def step(t, carry, cre=cre, cim=cim, ar=ar, ai=ai):
            sr, si = carry
            r0 = pl.multiple_of(t * nb, nb)
            nr = (ar * sr - ai * si) + st_ref[pl.ds(r0, nb), cre:cre + pairs_half]
            ni = (ar * si + ai * sr) + st_ref[pl.ds(r0, nb), cim:cim + pairs_half]
            st_ref[pl.ds(r0, nb), cre:cre + pairs_half] = nr
            st_ref[pl.ds(r0, nb), cim:cim + pairs_half] = ni
            return nr, ni

        sr, si = lax.fori_loop(
            0, steps, step,
            (carry_ref[:, cre:cre + pairs_half], carry_ref[:, cim:cim + pairs_half]), unroll=True)
        carry_ref[:, cre:cre + pairs_half] = sr
        carry_ref[:, cim:cim + pairs_half] = si

    ys = []
    for hf in range(2):
        ys.append(_dot(st_ref[:, hf * S5_HALF:(hf + 1) * S5_HALF].astype(BF16), cc_ref[hf]))
    y = jnp.concatenate(ys, axis=-1) + d_ref[layer:layer + 1, :] * ug_ref[:, 0:WIDTH]
    y = _gelu(y)
    y = y * _sigmoid(_dot(y.astype(BF16), wglu_ref[...]))
    y = (y * _silu(ug_ref[:, WIDTH:2 * WIDTH])).astype(BF16)
    for k in range(steps // tt):
        back = _dot_tn(perm, y[k * S5_PERM:(k + 1) * S5_PERM, :])
        o_ref[:, k * tt:(k + 1) * tt, :] = back.reshape(nb, tt, WIDTH).astype(o_ref.dtype)


def _s5_branch(h, win, wb, cc, d, wglu, ab_re, ab_im, layer):
    b, l, dm = h.shape
    assert S5_PERM % b == 0 and b % SUBLANES == 0
    steps = min(S5_STEPS, l)
    rows = steps * b
    return pl.pallas_call(
        functools.partial(_s5_kernel, steps=steps, nb=b, layer=layer),
        grid=(l // steps,),
        in_specs=[pl.BlockSpec((b, steps, dm), lambda c: (0, c, 0)), _in_cols(layer, IN_OFFS[0], 2 * WIDTH),
                  _of_layer(wb, layer), _of_layer(cc, layer), _full(d.shape), _of_layer(wglu, layer),
                  _full(ab_re.shape), _full(ab_im.shape)],
        out_specs=pl.BlockSpec((b, steps, WIDTH), lambda c: (0, c, 0)),
        out_shape=jax.ShapeDtypeStruct((b, l, WIDTH), BF16),
        scratch_shapes=[pltpu.VMEM((rows, dm), BF16),
                        pltpu.VMEM((rows, 2 * WIDTH), F32),
                        pltpu.VMEM((rows, 2 * S5_HALF), F32),
                        pltpu.VMEM((b, 2 * S5_HALF), F32)],
        compiler_params=_params("arbitrary"),
        name="s5_branch",
    )(h, win, wb, cc, d, wglu, ab_re, ab_im)


def _sgu_kernel(h_ref, wu_ref, wv_ref, wg_ref, lnw_ref, lnb_ref, ws_ref, bias_ref, o_ref, *, rows, layer):
    t = SGU_CHUNK
    causal = lax.broadcasted_iota(jnp.int32, (t, t), 0) >= lax.broadcasted_iota(jnp.int32, (t, t), 1)
    first_head = lax.broadcasted_iota(jnp.int32, (t, LANES), 1) < SGU_HEAD_DIM
    wm = [jnp.where(causal, ws_ref[hd], 0.0).astype(BF16) for hd in range(SGU_HEADS)]
    zero = jnp.zeros((), BF16)
    part = rows // SPLIT
    for sp in range(SPLIT):
        h = h_ref[sp * part:(sp + 1) * part, :]
        v = _gelu(_dot(h, wv_ref[...]))
        mu = jnp.mean(v, axis=-1, keepdims=True)
        vc = v - mu
        var = jnp.mean(vc * vc, axis=-1, keepdims=True)
        vn = (vc * lax.rsqrt(var + EPS) * lnw_ref[layer:layer + 1, :] + lnb_ref[layer:layer + 1, :]).astype(BF16)
        ug = _gelu(_dot(h, wu_ref[...])) * _silu(_dot(h, wg_ref[...]))
        for c in range(part // t):
            r = slice(c * t, (c + 1) * t)
            outs = []
            for j in range(WIDTH // LANES):
                blk = vn[r, j * LANES:(j + 1) * LANES]
                outs.append(_dot(wm[2 * j], jnp.where(first_head, blk, zero))
                            + _dot(wm[2 * j + 1], jnp.where(first_head, zero, blk)))
            s = jnp.concatenate(outs, axis=-1) + bias_ref[...]
            o_ref[sp * part + c * t:sp * part + (c + 1) * t, :] = (ug[r] * s).astype(o_ref.dtype)


def _sgu_branch(h, win, ln_w, ln_b, w_s, bias, layer):
    b, l, dm = h.shape
    rows = min(SGU_ROWS, l)
    return pl.pallas_call(
        functools.partial(_sgu_kernel, rows=rows, layer=layer),
        grid=(b, l // rows),
        in_specs=[_seq_spec(rows, dm)] + [_in_cols(layer, IN_OFFS[k], WIDTH) for k in (2, 3, 4)]
                 + [_full(ln_w.shape), _full(ln_b.shape), _of_layer(w_s, layer), _of_layer(bias, layer)],
        out_specs=_seq_spec(rows, WIDTH),
        out_shape=jax.ShapeDtypeStruct((b, l, WIDTH), BF16),
        compiler_params=_params("parallel", "parallel"),
        name="sgu_branch",
    )(h, win, win, win, ln_w, ln_b, w_s, bias)


SSD_QUANT = 3
SSD_PARTS = 3
SSD_LANE_GROUPS = SSD_QUANT * SSD_PARTS


def _ssd_expand_matrix():
    e = np.zeros((LANES, SSD_QUANT * WIDTH), np.float32)
    for grp in range(SSD_LANE_GROUPS):
        for hd in range(M2_HEADS):
            c0 = (grp // SSD_PARTS) * WIDTH + hd * M2_HEAD_DIM
            e[grp * M2_HEADS + hd, c0:c0 + M2_HEAD_DIM] = 1.0
    return jnp.asarray(e, BF16)


def _cumsum_rows(x):
    q = x.shape[0]
    row = lax.broadcasted_iota(jnp.int32, x.shape, 0)
    dist = 1
    while dist < q:
        x = x + jnp.where(row >= dist, pltpu.roll(x, dist, 0), 0.0)
        dist *= 2
    return x


def _ssd_sc_kernel(h_ref, wz_ref, wxbc_ref, wdt_ref, cw_ref, cb_ref, dtb_ref, al_ref, e_ref, d_ref, nw_ref,
                   wsc_ref, scw_ref, o_ref, osc_ref, prev_ref, st_ref, scprev_ref, *, rows, layer):
    @pl.when(pl.program_id(1) == 0)
    def _():
        prev_ref[...] = jnp.zeros(prev_ref.shape, F32)
        st_ref[...] = jnp.zeros(st_ref.shape, F32)
        scprev_ref[...] = jnp.zeros(scprev_ref.shape, F32)

    sc_proj = lambda j: _dot(h_ref[...], wsc_ref[:, j * WIDTH:(j + 1) * WIDTH])

    rpart = rows // SPLIT
    hs = [h_ref[sp * rpart:(sp + 1) * rpart, :] for sp in range(SPLIT)]
    raws = [_dot(hh, wxbc_ref[...]) for hh in hs]
    zs = [_dot(hh, wz_ref[...]) for hh in hs]
    dtrs = [_dot(hh, wdt_ref[...]) for hh in hs]
    sc_v = sc_proj(1) * sc_proj(2)
    prev = prev_ref[...]
    xbcs, dts, das = [], [], []
    neg_a = -jnp.exp(al_ref[layer:layer + 1, :])
    for sp in range(SPLIT):
        raw = raws[sp]
        cat = jnp.concatenate([prev, raw], axis=0)
        acc = cb_ref[layer:layer + 1, :] + cw_ref[M2_CONV - 1:M2_CONV, :] * raw
        for j in range(1, M2_CONV):
            k = M2_CONV - 1 - j
            acc = acc + cw_ref[k:k + 1, :] * _shift_rows(cat, j)
        prev = raw[rpart - SUBLANES:rpart, :]
        xbcs.append(_silu(acc))
        dts.append(_softplus(dtrs[sp] + dtb_ref[layer:layer + 1, :]))
        das.append(dts[sp] * neg_a)
    prev_ref[...] = prev
    grp = lax.broadcasted_iota(jnp.int32, (1, LANES), 1) // M2_HEADS
    quant = grp // SSD_PARTS
    part = grp % SSD_PARTS

    q = M2_CHUNK
    hd_w = M2_HEAD_DIM
    causal = lax.broadcasted_iota(jnp.int32, (q, q), 0) >= lax.broadcasted_iota(jnp.int32, (q, q), 1)
    first_head = lax.broadcasted_iota(jnp.int32, (q, LANES), 1) < hd_w
    zero = jnp.zeros((), BF16)
    gw = WIDTH // M2_GROUPS

    def chunk(s):
        sp = (s * q) // rpart
        xbc, z, dt, da = xbcs[sp], zs[sp], dts[sp], das[sp]
        x = xbc[:, 0:WIDTH]
        r = slice(s * q - sp * rpart, (s + 1) * q - sp * rpart)
        cs = _cumsum_rows(da[r])
        cs_row = cs.T[0:M2_HEADS, :]
        cs_last = cs[q - 1:q, :]
        val = jnp.where(quant == 0, dt[r], jnp.where(quant == 1, dt[r] * jnp.exp(cs_last - cs), jnp.exp(cs)))
        p0 = val.astype(BF16)
        r1 = val - p0.astype(F32)
        p1 = r1.astype(BF16)
        p2 = (r1 - p1.astype(F32)).astype(BF16)
        spread = _dot(jnp.where(part == 0, p0, jnp.where(part == 1, p1, p2)), e_ref[...])
        dt_e = spread[:, 0:WIDTH]
        dtdec_e = spread[:, WIDTH:2 * WIDTH]
        exp_cs = spread[:, 2 * WIDTH:3 * WIDTH]
        exp_last = exp_cs[q - 1:q, :]
        x_c = x[r]
        xdt_b = (x_c * dt_e).astype(BF16)
        xdec = (x_c * dtdec_e).astype(BF16)
        ys = []
        for g in range(M2_GROUPS):
            bo = WIDTH + g * M2_STATE
            co = WIDTH + M2_GROUPS * M2_STATE + g * M2_STATE
            bg = xbc[r, bo:bo + M2_STATE].astype(BF16)
            cg = xbc[r, co:co + M2_STATE].astype(BF16)
            scores = _dot_nt(cg, bg)
            state = st_ref[g]
            y_g = _dot(cg, state.astype(BF16)) * exp_cs[:, g * gw:(g + 1) * gw]
            diag = []
            for jp in range(gw // LANES):
                blk = xdt_b[:, g * gw + jp * LANES:g * gw + (jp + 1) * LANES]
                acc2 = None
                for qq in range(2):
                    hd = g * (M2_HEADS // M2_GROUPS) + jp * 2 + qq
                    seg = cs[:, hd:hd + 1] - cs_row[hd:hd + 1, :]
                    decay = jnp.exp(jnp.where(causal, seg, -jnp.inf))
                    m = (scores * decay).astype(BF16)
                    half = jnp.where(first_head, blk, zero) if qq == 0 else jnp.where(first_head, zero, blk)
                    term = _dot(m, half)
                    acc2 = term if acc2 is None else acc2 + term
                diag.append(acc2)
            ys.append(y_g + jnp.concatenate(diag, axis=-1))
            st_ref[g] = state * exp_last[:, g * gw:(g + 1) * gw] + _dot_tn(bg, xdec[:, g * gw:(g + 1) * gw])
        y = jnp.concatenate(ys, axis=-1) + d_ref[layer:layer + 1, :] * x_c
        y = y * _silu(z[r])
        o_ref[s * q:(s + 1) * q, :] = _rms(y, nw_ref[layer:layer + 1, :]).astype(o_ref.dtype)

    def sc_tail(sc_b, sc_gate):
        cat = jnp.concatenate([scprev_ref[...], sc_v], axis=0)
        acc = scw_ref[SC_CONV - 1:SC_CONV, :] * sc_v
        for j in range(1, SC_CONV):
            k = SC_CONV - 1 - j
            acc = acc + scw_ref[k:k + 1, :] * _shift_rows(cat, j)
        scprev_ref[...] = sc_v[rows - SUBLANES:rows, :]
        osc_ref[...] = (sc_b * acc * _silu(sc_gate)).astype(osc_ref.dtype)

    n_chunks = rows // q
    sc_b = sc_gate = None
    for s in range(n_chunks):
        chunk(s)
        if s == 0:
            sc_b = sc_proj(0)
        if s == min(1, n_chunks - 1):
            sc_gate = sc_proj(3)
        if s == min(2, n_chunks - 1):
            sc_tail(sc_b, sc_gate)


def _ssd_sc_branch(h, win, wdt_groups, conv_w, conv_b, dtb_groups, al_groups, expand, d_lanes, norm_w,
                   w_sc, sc_conv_w, layer):
    b, l, dm = h.shape
    rows = min(SSD_ROWS, l)
    out = jax.ShapeDtypeStruct((b, l, WIDTH), BF16)
    return pl.pallas_call(
        functools.partial(_ssd_sc_kernel, rows=rows, layer=layer),
        grid=(b, l // rows),
        in_specs=[_seq_spec(rows, dm), _in_cols(layer, IN_OFFS[5], WIDTH), _in_cols(layer, IN_OFFS[6], M2_CONV_CH),
                  _of_layer(wdt_groups, layer), _of_layer(conv_w, layer), _full(conv_b.shape),
                  _full(dtb_groups.shape), _full(al_groups.shape), _full(expand.shape), _full(d_lanes.shape),
                  _full(norm_w.shape), _of_layer(w_sc, layer), _of_layer(sc_conv_w, layer)],
        out_specs=(_seq_spec(rows, WIDTH), _seq_spec(rows, WIDTH)),
        out_shape=(out, out),
        scratch_shapes=[pltpu.VMEM((SUBLANES, M2_CONV_CH), F32),
                        pltpu.VMEM((M2_GROUPS, M2_STATE, WIDTH // M2_GROUPS), F32),
                        pltpu.VMEM((SUBLANES, WIDTH), F32)],
        compiler_params=_params("parallel", "arbitrary"),
        name="ssd_sc_branch",
    )(h, win, win, wdt_groups, conv_w, conv_b, dtb_groups, al_groups, expand, d_lanes, norm_w, w_sc, sc_conv_w)


def _merge_kernel(x_ref, h_ref, ya_ref, yb_ref, yc_ref, yd_ref, wm_ref, mb_ref, wbr_ref, wo_ref, nw_ref,
                  xo_ref, no_ref, *, norm_row, rows):
    d = D_MODEL
    part = rows // SPLIT
    for sp in range(SPLIT):
        r = slice(sp * part, (sp + 1) * part)
        h = h_ref[r, :]
        merged = None
        for k, y_ref in enumerate((ya_ref, yb_ref, yc_ref, yd_ref)):
            gate = _sigmoid(_dot(h, wm_ref[:, k * d:(k + 1) * d]) + mb_ref[k:k + 1, :])
            term = gate * _dot(y_ref[r, :], wbr_ref[k])
            merged = term if merged is None else merged + term
        xn = x_ref[r, :] + _dot(merged.astype(BF16), wo_ref[...])
        xo_ref[r, :] = xn
        no_ref[r, :] = _rms(xn, nw_ref[norm_row:norm_row + 1, :]).astype(no_ref.dtype)


def _merge(x2d, h2d, ys, wm, mb, wbr, wo, next_norm_w, norm_row, norm_dtype, layer):
    t, d = x2d.shape
    tile = lambda cols: pl.BlockSpec((MERGE_ROWS, cols), lambda i: (i, 0))
    return pl.pallas_call(
        functools.partial(_merge_kernel, norm_row=norm_row, rows=MERGE_ROWS),
        grid=(t // MERGE_ROWS,),
        in_specs=[tile(d), tile(d)] + [tile(WIDTH)] * N_BRANCH
                 + [_of_layer(wm, layer), _of_layer(mb, layer), _of_layer(wbr, layer), _of_layer(wo, layer),
                    _full(next_norm_w.shape)],
        out_specs=(tile(d), tile(d)),
        out_shape=(jax.ShapeDtypeStruct((t, d), F32), jax.ShapeDtypeStruct((t, d), norm_dtype)),
        compiler_params=_params("parallel"),
        name="merge",
    )(x2d, h2d, *ys, wm, mb, wbr, wo, next_norm_w)


def kernel(x, norm_w, w_in, s5_lambda_re, s5_lambda_im, s5_b_re, s5_b_im, s5_c_re, s5_c_im, s5_d,
           s5_log_step, s5_w_glu, sgu_ln_w, sgu_ln_b, sgu_w, sgu_b, m2_conv_w, m2_conv_b, m2_dt_bias,
           m2_a_log, m2_d, m2_norm_w, sc_conv_w, merge_b, w_branch, w_out, final_norm_w):
    b, l, d = x.shape
    t = b * l
    nl = w_in.shape[0]

    win = w_in.astype(BF16)
    w_sc = win[:, :, IN_OFFS[8]:IN_OFFS[12]]
    w_merge = win[:, :, IN_OFFS[12]:IN_OFFS[13]]
    used = SSD_LANE_GROUPS * M2_HEADS
    lane_groups = lambda v: jnp.pad(jnp.tile(v, (1,) * (v.ndim - 1) + (SSD_LANE_GROUPS,)),
                                    ((0, 0),) * (v.ndim - 1) + ((0, LANES - used),))
    wdt_groups = lane_groups(win[:, :, IN_OFFS[7]:IN_OFFS[8]])
    dtb_groups = lane_groups(m2_dt_bias)
    al_groups = lane_groups(m2_a_log)
    d_lanes = jnp.repeat(m2_d, M2_HEAD_DIM, axis=-1)
    expand = _ssd_expand_matrix()
    sgu_bias = jnp.repeat(sgu_b.transpose(0, 2, 1), SGU_HEAD_DIM, axis=-1)
    ab_re, ab_im, s5_wb, s5_cc = _s5_prep(s5_lambda_re, s5_lambda_im, s5_b_re, s5_b_im, s5_c_re, s5_c_im,
                                          s5_log_step)
    s5_dd = s5_d.reshape(nl, WIDTH)
    wglu = s5_w_glu.astype(BF16)
    wbr = w_branch.astype(BF16)
    wo = w_out.astype(BF16)
    norms = jnp.concatenate([norm_w, final_norm_w.reshape(1, d)], axis=0)

    x2d = x.reshape(t, d)
    h2d = _rmsnorm(x2d, norm_w)
    out = None
    for i in range(DEPTH):
        h = h2d.reshape(b, l, d)
        ya = _s5_branch(h, win, s5_wb, s5_cc, s5_dd, wglu, ab_re, ab_im, i)
        yb = _sgu_branch(h, win, sgu_ln_w, sgu_ln_b, sgu_w, sgu_bias, i)
        yc, yd = _ssd_sc_branch(h, win, wdt_groups, m2_conv_w, m2_conv_b, dtb_groups, al_groups, expand, d_lanes,
                                m2_norm_w, w_sc, sc_conv_w, i)
        last = i == DEPTH - 1
        x2d, h2d = _merge(x2d, h2d, [y.reshape(t, WIDTH) for y in (ya, yb, yc, yd)], w_merge, merge_b, wbr, wo,
                          norms, i + 1, F32 if last else BF16, i)
        out = h2d
    return out.reshape(b, l, d)
```

```python
import functools

import jax
import jax.numpy as jnp
import numpy as np
from jax import lax
from jax.experimental import pallas as pl
from jax.experimental.pallas import tpu as pltpu

F32 = jnp.float32
BF16 = jnp.bfloat16

D_MODEL = 1024
DEPTH = 2
WIDTH = 512
N_BRANCH = 4
EPS = 1e-6
S5_GROUP = 16
S5_GROUPS = WIDTH // S5_GROUP
S5_STATE = 64
S5_PAIRS = S5_GROUPS * S5_STATE
SGU_CHUNK = 128
SGU_HEADS = 8
SGU_HEAD_DIM = WIDTH // SGU_HEADS
M2_HEAD_DIM = 64
M2_HEADS = WIDTH // M2_HEAD_DIM
M2_GROUPS = 2
M2_STATE = 128
M2_CONV = 4
M2_CHUNK = 128
M2_CONV_CH = WIDTH + 2 * M2_GROUPS * M2_STATE
SC_CONV = 3

IN_SIZES = (WIDTH, WIDTH, WIDTH, WIDTH, WIDTH, WIDTH, M2_CONV_CH, M2_HEADS,
            WIDTH, WIDTH, WIDTH, WIDTH, N_BRANCH * D_MODEL)
IN_OFFS = tuple(int(v) for v in np.cumsum((0,) + IN_SIZES))

SUBLANES = 8
LANES = 128
MXU_DIM = 256
VMEM_LIMIT_BYTES = 56 * 1024 * 1024

NORM_ROWS = 1024
S5_STEPS = 128
SGU_ROWS = 1024
SSD_ROWS = 1024
MERGE_ROWS = 1024
SPLIT = 2


def _dot(a, b):
    return jnp.dot(a, b, preferred_element_type=F32)


def _dot_nt(a, b):
    return lax.dot_general(a, b, (((1,), (1,)), ((), ())), preferred_element_type=F32)


def _dot_tn(a, b):
    return lax.dot_general(a, b, (((0,), (0,)), ((), ())), preferred_element_type=F32)


def _sigmoid(x):
    return 0.5 * (jnp.tanh(0.5 * x) + 1.0)


def _silu(x):
    hx = 0.5 * x
    return hx * jnp.tanh(hx) + hx


def _gelu(x):
    c = np.float32(np.sqrt(2.0 / np.pi))
    return x * (0.5 * (1.0 + jnp.tanh(c * (x + 0.044715 * (x * x * x)))))


def _softplus(x):
    return jnp.maximum(x, 0.0) + jnp.log1p(jnp.exp(-jnp.abs(x)))


def _rms(x, w):
    return x * lax.rsqrt(jnp.mean(x * x, axis=-1, keepdims=True) + EPS) * w


def _shift_rows(cat, j):
    n8, c = cat.shape
    rot = pltpu.roll(cat.reshape(n8 // SUBLANES, SUBLANES, c), j, 1)
    sub = lax.broadcasted_iota(jnp.int32, (1, SUBLANES, c), 1)
    out = jnp.where(sub >= j, rot[1:], rot[:-1])
    return out.reshape(n8 - SUBLANES, c)


def _params(*semantics):
    return pltpu.CompilerParams(dimension_semantics=semantics, vmem_limit_bytes=VMEM_LIMIT_BYTES)


RESIDENT = pl.Buffered(1)


def _full(shape):
    zeros = (0,) * len(shape)
    return pl.BlockSpec(tuple(shape), lambda *_: zeros, pipeline_mode=RESIDENT)


def _of_layer(arr, layer):
    zeros = (0,) * (arr.ndim - 1)
    return pl.BlockSpec((None,) + tuple(arr.shape[1:]), lambda *_: (layer,) + zeros, pipeline_mode=RESIDENT)


def _in_cols(layer, first, width):
    assert first % width == 0
    return pl.BlockSpec((None, D_MODEL, width), lambda *_: (layer, 0, first // width), pipeline_mode=RESIDENT)


def _seq_spec(rows, cols):
    return pl.BlockSpec((None, rows, cols), lambda b, c: (b, c, 0))


def _rmsnorm_kernel(x_ref, w_ref, o_ref):
    o_ref[...] = _rms(x_ref[...], w_ref[0:1, :]).astype(o_ref.dtype)


def _rmsnorm(x2d, norm_w):
    t, d = x2d.shape
    return pl.pallas_call(
        _rmsnorm_kernel,
        grid=(t // NORM_ROWS,),
        in_specs=[pl.BlockSpec((NORM_ROWS, d), lambda i: (i, 0)), _full(norm_w.shape)],
        out_specs=pl.BlockSpec((NORM_ROWS, d), lambda i: (i, 0)),
        out_shape=jax.ShapeDtypeStruct((t, d), BF16),
        compiler_params=_params("parallel"),
        name="rmsnorm",
    )(x2d, norm_w)


def _s5_prep_kernel(lr_ref, li_ref, ls_ref, bre_ref, bim_ref, abre_ref, abim_ref, bbre_ref, bbim_ref):
    lr, li = lr_ref[...], li_ref[...]
    step = jnp.exp(ls_ref[...])
    mag = jnp.exp(lr * step)
    ang = li * step
    ab_re, ab_im = mag * jnp.cos(ang), mag * jnp.sin(ang)
    abre_ref[...] = ab_re
    abim_ref[...] = ab_im
    den = lr * lr + li * li
    nr = ab_re - 1.0
    coef_re = (nr * lr + ab_im * li) / den
    coef_im = (ab_im * lr - nr * li) / den
    for layer in range(lr.shape[0]):
        cr, ci = coef_re[layer:layer + 1, :], coef_im[layer:layer + 1, :]
        br, bi = bre_ref[layer], bim_ref[layer]
        bbre_ref[layer] = cr * br - ci * bi
        bbim_ref[layer] = cr * bi + ci * br


def _s5_prep(lam_re, lam_im, b_re, b_im, c_re, c_im, log_step):
    nl = lam_re.shape[0]
    g, n, p = S5_GROUPS, S5_STATE, S5_GROUP
    ls = jnp.repeat(log_step, n, axis=-1)
    row = lambda v: v.reshape(nl, g * n)
    chan_minor = lambda v: v.transpose(0, 3, 1, 2).reshape(nl, p, g * n)
    ab_re, ab_im, bb_re, bb_im = pl.pallas_call(
        _s5_prep_kernel,
        out_shape=(jax.ShapeDtypeStruct((nl, g * n), F32), jax.ShapeDtypeStruct((nl, g * n), F32),
                   jax.ShapeDtypeStruct((nl, p, g * n), F32), jax.ShapeDtypeStruct((nl, p, g * n), F32)),
        name="s5_prep",
    )(row(lam_re), row(lam_im), row(ls), chan_minor(b_re), chan_minor(b_im))

    gh = g // 2
    in_mask = (np.arange(gh * p)[:, None] // p) == (np.arange(gh * n)[None, :] // n)
    out_mask = in_mask.T

    def in_half(bb):
        t = bb.reshape(nl, p, 2, gh, n).transpose(0, 2, 3, 1, 4).reshape(nl, 2, gh * p, n)
        return jnp.where(in_mask, jnp.tile(t, (1, 1, 1, gh)), 0.0)

    def out_half(c):
        t = c.reshape(nl, 2, gh, p, n).transpose(0, 1, 2, 4, 3).reshape(nl, 2, gh * n, p)
        return jnp.where(out_mask, jnp.tile(t, (1, 1, 1, gh)), 0.0)

    wb = jnp.concatenate([in_half(bb_re), in_half(bb_im)], axis=3).astype(BF16)
    cc = jnp.concatenate([out_half(c_re), -out_half(c_im)], axis=2).astype(BF16)
    return ab_re, ab_im, wb, cc


S5_HALF = S5_PAIRS
S5_PERM = MXU_DIM


def _perm_time_major(nb):
    tt = S5_PERM // nb
    i = lax.broadcasted_iota(jnp.int32, (S5_PERM, S5_PERM), 0)
    j = lax.broadcasted_iota(jnp.int32, (S5_PERM, S5_PERM), 1)
    return jnp.where(j == (i % nb) * tt + i // nb, 1.0, 0.0).astype(BF16)


def _s5_kernel(h_ref, wug_ref, wb_ref, cc_ref, d_ref, wglu_ref, abre_ref, abim_ref, o_ref,
               hp_ref, ug_ref, st_ref, carry_ref, *, steps, nb, layer):
    @pl.when(pl.program_id(0) == 0)
    def _():
        carry_ref[...] = jnp.zeros(carry_ref.shape, F32)

    tt = S5_PERM // nb
    perm = _perm_time_major(nb)
    for k in range(steps // tt):
        blk = h_ref[:, k * tt:(k + 1) * tt, :].reshape(S5_PERM, D_MODEL)
        hp_ref[k * S5_PERM:(k + 1) * S5_PERM, :] = _dot(perm, blk).astype(BF16)

    ug_ref[:, 0:WIDTH] = _dot(hp_ref[...], wug_ref[:, 0:WIDTH])
    ub = ug_ref[:, 0:WIDTH].astype(BF16)
    hw = WIDTH // 2
    for hf in range(2):
        st_ref[:, hf * S5_HALF:(hf + 1) * S5_HALF] = _dot(ub[:, hf * hw:(hf + 1) * hw], wb_ref[hf])
    ug_ref[:, WIDTH:2 * WIDTH] = _dot(hp_ref[...], wug_ref[:, WIDTH:2 * WIDTH])

    pairs_half = S5_HALF // 2
    for hf in range(2):
        cre = hf * S5_HALF
        cim = cre + pairs_half
        tc = hf * pairs_half
        ar = jnp.broadcast_to(abre_ref[layer:layer + 1, tc:tc + pairs_half], (nb, pairs_half))
        ai = jnp.broadcast_to(abim_ref[layer:layer + 1, tc:tc + pairs_half], (nb, pairs_half))

        def step(t, carry, cre=cre, cim=cim, ar=ar, ai=ai):
            sr, si = carry
            r0 = pl.multiple_of(t * nb, nb)
            nr = (ar * sr - ai * si) + st_ref[pl.ds(r0, nb), cre:cre + pairs_half]
            ni = (ar * si + ai * sr) + st_ref[pl.ds(r0, nb), cim:cim + pairs_half]
            st_ref[pl.ds(r0, nb), cre:cre + pairs_half] = nr
            st_ref[pl.ds(r0, nb), cim:cim + pairs_half] = ni
            return nr, ni

        sr, si = lax.fori_loop(
            0, steps, step,
            (carry_ref[:, cre:cre + pairs_half], carry_ref[:, cim:cim + pairs_half]), unroll=True)
        carry_ref[:, cre:cre + pairs_half] = sr
        carry_ref[:, cim:cim + pairs_half] = si

    ys = []
    for hf in range(2):
        ys.append(_dot(st_ref[:, hf * S5_HALF:(hf + 1) * S5_HALF].astype(BF16), cc_ref[hf]))
    y = jnp.concatenate(ys, axis=-1) + d_ref[layer:layer + 1, :] * ug_ref[:, 0:WIDTH]
    y = _gelu(y)
    y = y * _sigmoid(_dot(y.astype(BF16), wglu_ref[...]))
    y = (y * _silu(ug_ref[:, WIDTH:2 * WIDTH])).astype(BF16)
    for k in range(steps // tt):
        back = _dot_tn(perm, y[k * S5_PERM:(k + 1) * S5_PERM, :])
        o_ref[:, k * tt:(k + 1) * tt, :] = back.reshape(nb, tt, WIDTH).astype(o_ref.dtype)


def _s5_branch(h, win, wb, cc, d, wglu, ab_re, ab_im, layer):
    b, l, dm = h.shape
    assert S5_PERM % b == 0 and b % SUBLANES == 0
    steps = min(S5_STEPS, l)
    rows = steps * b
    return pl.pallas_call(
        functools.partial(_s5_kernel, steps=steps, nb=b, layer=layer),
        grid=(l // steps,),
        in_specs=[pl.BlockSpec((b, steps, dm), lambda c: (0, c, 0)), _in_cols(layer, IN_OFFS[0], 2 * WIDTH),
                  _of_layer(wb, layer), _of_layer(cc, layer), _full(d.shape), _of_layer(wglu, layer),
                  _full(ab_re.shape), _full(ab_im.shape)],
        out_specs=pl.BlockSpec((b, steps, WIDTH), lambda c: (0, c, 0)),
        out_shape=jax.ShapeDtypeStruct((b, l, WIDTH), BF16),
        scratch_shapes=[pltpu.VMEM((rows, dm), BF16),
                        pltpu.VMEM((rows, 2 * WIDTH), F32),
                        pltpu.VMEM((rows, 2 * S5_HALF), F32),
                        pltpu.VMEM((b, 2 * S5_HALF), F32)],
        compiler_params=_params("arbitrary"),
        name="s5_branch",
    )(h, win, wb, cc, d, wglu, ab_re, ab_im)


def _sgu_kernel(h_ref, wu_ref, wv_ref, wg_ref, lnw_ref, lnb_ref, ws_ref, bias_ref, o_ref, *, rows, layer):
    t = SGU_CHUNK
    causal = lax.broadcasted_iota(jnp.int32, (t, t), 0) >= lax.broadcasted_iota(jnp.int32, (t, t), 1)
    first_head = lax.broadcasted_iota(jnp.int32, (t, LANES), 1) < SGU_HEAD_DIM
    wm = [jnp.where(causal, ws_ref[hd], 0.0).astype(BF16) for hd in range(SGU_HEADS)]
    zero = jnp.zeros((), BF16)
    part = rows // SPLIT
    for sp in range(SPLIT):
        h = h_ref[sp * part:(sp + 1) * part, :]
        v = _gelu(_dot(h, wv_ref[...]))
        mu = jnp.mean(v, axis=-1, keepdims=True)
        vc = v - mu
        var = jnp.mean(vc * vc, axis=-1, keepdims=True)
        vn = (vc * lax.rsqrt(var + EPS) * lnw_ref[layer:layer + 1, :] + lnb_ref[layer:layer + 1, :]).astype(BF16)
        ug = _gelu(_dot(h, wu_ref[...])) * _silu(_dot(h, wg_ref[...]))
        for c in range(part // t):
            r = slice(c * t, (c + 1) * t)
            outs = []
            for j in range(WIDTH // LANES):
                blk = vn[r, j * LANES:(j + 1) * LANES]
                outs.append(_dot(wm[2 * j], jnp.where(first_head, blk, zero))
                            + _dot(wm[2 * j + 1], jnp.where(first_head, zero, blk)))
            s = jnp.concatenate(outs, axis=-1) + bias_ref[...]
            o_ref[sp * part + c * t:sp * part + (c + 1) * t, :] = (ug[r] * s).astype(o_ref.dtype)


def _sgu_branch(h, win, ln_w, ln_b, w_s, bias, layer):
    b, l, dm = h.shape
    rows = min(SGU_ROWS, l)
    return pl.pallas_call(
        functools.partial(_sgu_kernel, rows=rows, layer=layer),
        grid=(b, l // rows),
        in_specs=[_seq_spec(rows, dm)] + [_in_cols(layer, IN_OFFS[k], WIDTH) for k in (2, 3, 4)]
                 + [_full(ln_w.shape), _full(ln_b.shape), _of_layer(w_s, layer), _of_layer(bias, layer)],
        out_specs=_seq_spec(rows, WIDTH),
        out_shape=jax.ShapeDtypeStruct((b, l, WIDTH), BF16),
        compiler_params=_params("parallel", "parallel"),
        name="sgu_branch",
    )(h, win, win, win, ln_w, ln_b, w_s, bias)


SSD_QUANT = 3
SSD_PARTS = 3
SSD_LANE_GROUPS = SSD_QUANT * SSD_PARTS


def _ssd_expand_matrix():
    e = np.zeros((LANES, SSD_QUANT * WIDTH), np.float32)
    for grp in range(SSD_LANE_GROUPS):
        for hd in range(M2_HEADS):
            c0 = (grp // SSD_PARTS) * WIDTH + hd * M2_HEAD_DIM
            e[grp * M2_HEADS + hd, c0:c0 + M2_HEAD_DIM] = 1.0
    return jnp.asarray(e, BF16)


def _cumsum_rows(x):
    q = x.shape[0]
    row = lax.broadcasted_iota(jnp.int32, x.shape, 0)
    dist = 1
    while dist < q:
        x = x + jnp.where(row >= dist, pltpu.roll(x, dist, 0), 0.0)
        dist *= 2
    return x


def _ssd_sc_kernel(h_ref, wz_ref, wxbc_ref, wdt_ref, cw_ref, cb_ref, dtb_ref, al_ref, e_ref, d_ref, nw_ref,
                   wsc_ref, scw_ref, o_ref, osc_ref, prev_ref, st_ref, scprev_ref, *, rows, layer):
    @pl.when(pl.program_id(1) == 0)
    def _():
        prev_ref[...] = jnp.zeros(prev_ref.shape, F32)
        st_ref[...] = jnp.zeros(st_ref.shape, F32)
        scprev_ref[...] = jnp.zeros(scprev_ref.shape, F32)

    sc_proj = lambda j: _dot(h_ref[...], wsc_ref[:, j * WIDTH:(j + 1) * WIDTH])

    rpart = rows // SPLIT
    hs = [h_ref[sp * rpart:(sp + 1) * rpart, :] for sp in range(SPLIT)]
    raws = [_dot(hh, wxbc_ref[...]) for hh in hs]
    zs = [_dot(hh, wz_ref[...]) for hh in hs]
    dtrs = [_dot(hh, wdt_ref[...]) for hh in hs]
    sc_v = sc_proj(1) * sc_proj(2)
    prev = prev_ref[...]
    xbcs, dts, das = [], [], []
    neg_a = -jnp.exp(al_ref[layer:layer + 1, :])
    for sp in range(SPLIT):
        raw = raws[sp]
        cat = jnp.concatenate([prev, raw], axis=0)
        acc = cb_ref[layer:layer + 1, :] + cw_ref[M2_CONV - 1:M2_CONV, :] * raw
        for j in range(1, M2_CONV):
            k = M2_CONV - 1 - j
            acc = acc + cw_ref[k:k + 1, :] * _shift_rows(cat, j)
        prev = raw[rpart - SUBLANES:rpart, :]
        xbcs.append(_silu(acc))
        dts.append(_softplus(dtrs[sp] + dtb_ref[layer:layer + 1, :]))
        das.append(dts[sp] * neg_a)
    prev_ref[...] = prev
    grp = lax.broadcasted_iota(jnp.int32, (1, LANES), 1) // M2_HEADS
    quant = grp // SSD_PARTS
    part = grp % SSD_PARTS

    q = M2_CHUNK
    hd_w = M2_HEAD_DIM
    causal = lax.broadcasted_iota(jnp.int32, (q, q), 0) >= lax.broadcasted_iota(jnp.int32, (q, q), 1)
    first_head = lax.broadcasted_iota(jnp.int32, (q, LANES), 1) < hd_w
    zero = jnp.zeros((), BF16)
    gw = WIDTH // M2_GROUPS

    def chunk(s):
        sp = (s * q) // rpart
        xbc, z, dt, da = xbcs[sp], zs[sp], dts[sp], das[sp]
        x = xbc[:, 0:WIDTH]
        r = slice(s * q - sp * rpart, (s + 1) * q - sp * rpart)
        cs = _cumsum_rows(da[r])
        cs_row = cs.T[0:M2_HEADS, :]
        cs_last = cs[q - 1:q, :]
        val = jnp.where(quant == 0, dt[r], jnp.where(quant == 1, dt[r] * jnp.exp(cs_last - cs), jnp.exp(cs)))
        p0 = val.astype(BF16)
        r1 = val - p0.astype(F32)
        p1 = r1.astype(BF16)
        p2 = (r1 - p1.astype(F32)).astype(BF16)
        spread = _dot(jnp.where(part == 0, p0, jnp.where(part == 1, p1, p2)), e_ref[...])
        dt_e = spread[:, 0:WIDTH]
        dtdec_e = spread[:, WIDTH:2 * WIDTH]
        exp_cs = spread[:, 2 * WIDTH:3 * WIDTH]
        exp_last = exp_cs[q - 1:q, :]
        x_c = x[r]
        xdt_b = (x_c * dt_e).astype(BF16)
        xdec = (x_c * dtdec_e).astype(BF16)
        ys = []
        for g in range(M2_GROUPS):
            bo = WIDTH + g * M2_STATE
            co = WIDTH + M2_GROUPS * M2_STATE + g * M2_STATE
            bg = xbc[r, bo:bo + M2_STATE].astype(BF16)
            cg = xbc[r, co:co + M2_STATE].astype(BF16)
            scores = _dot_nt(cg, bg)
            state = st_ref[g]
            y_g = _dot(cg, state.astype(BF16)) * exp_cs[:, g * gw:(g + 1) * gw]
            diag = []
            for jp in range(gw // LANES):
                blk = xdt_b[:, g * gw + jp * LANES:g * gw + (jp + 1) * LANES]
                acc2 = None
                for qq in range(2):
                    hd = g * (M2_HEADS // M2_GROUPS) + jp * 2 + qq
                    seg = cs[:, hd:hd + 1] - cs_row[hd:hd + 1, :]
                    decay = jnp.exp(jnp.where(causal, seg, -jnp.inf))
                    m = (scores * decay).astype(BF16)
                    half = jnp.where(first_head, blk, zero) if qq == 0 else jnp.where(first_head, zero, blk)
                    term = _dot(m, half)
                    acc2 = term if acc2 is None else acc2 + term
                diag.append(acc2)
            ys.append(y_g + jnp.concatenate(diag, axis=-1))
            st_ref[g] = state * exp_last[:, g * gw:(g + 1) * gw] + _dot_tn(bg, xdec[:, g * gw:(g + 1) * gw])
        y = jnp.concatenate(ys, axis=-1) + d_ref[layer:layer + 1, :] * x_c
        y = y * _silu(z[r])
        o_ref[s * q:(s + 1) * q, :] = _rms(y, nw_ref[layer:layer + 1, :]).astype(o_ref.dtype)

    def sc_tail(sc_b, sc_gate):
        cat = jnp.concatenate([scprev_ref[...], sc_v], axis=0)
        acc = scw_ref[SC_CONV - 1:SC_CONV, :] * sc_v
        for j in range(1, SC_CONV):
            k = SC_CONV - 1 - j
            acc = acc + scw_ref[k:k + 1, :] * _shift_rows(cat, j)
        scprev_ref[...] = sc_v[rows - SUBLANES:rows, :]
        osc_ref[...] = (sc_b * acc * _silu(sc_gate)).astype(osc_ref.dtype)

    n_chunks = rows // q
    sc_b = sc_gate = None
    for s in range(n_chunks):
        chunk(s)
        if s == 0:
            sc_b = sc_proj(0)
        if s == min(1, n_chunks - 1):
            sc_gate = sc_proj(3)
        if s == min(2, n_chunks - 1):
            sc_tail(sc_b, sc_gate)


def _ssd_sc_branch(h, win, wdt_groups, conv_w, conv_b, dtb_groups, al_groups, expand, d_lanes, norm_w,
                   w_sc, sc_conv_w, layer):
    b, l, dm = h.shape
    rows = min(SSD_ROWS, l)
    out = jax.ShapeDtypeStruct((b, l, WIDTH), BF16)
    return pl.pallas_call(
        functools.partial(_ssd_sc_kernel, rows=rows, layer=layer),
        grid=(b, l // rows),
        in_specs=[_seq_spec(rows, dm), _in_cols(layer, IN_OFFS[5], WIDTH), _in_cols(layer, IN_OFFS[6], M2_CONV_CH),
                  _of_layer(wdt_groups, layer), _of_layer(conv_w, layer), _full(conv_b.shape),
                  _full(dtb_groups.shape), _full(al_groups.shape), _full(expand.shape), _full(d_lanes.shape),
                  _full(norm_w.shape), _of_layer(w_sc, layer), _of_layer(sc_conv_w, layer)],
        out_specs=(_seq_spec(rows, WIDTH), _seq_spec(rows, WIDTH)),
        out_shape=(out, out),
        scratch_shapes=[pltpu.VMEM((SUBLANES, M2_CONV_CH), F32),
                        pltpu.VMEM((M2_GROUPS, M2_STATE, WIDTH // M2_GROUPS), F32),
                        pltpu.VMEM((SUBLANES, WIDTH), F32)],
        compiler_params=_params("parallel", "arbitrary"),
        name="ssd_sc_branch",
    )(h, win, win, wdt_groups, conv_w, conv_b, dtb_groups, al_groups, expand, d_lanes, norm_w, w_sc, sc_conv_w)


def _merge_kernel(x_ref, h_ref, ya_ref, yb_ref, yc_ref, yd_ref, wm_ref, mb_ref, wbr_ref, wo_ref, nw_ref,
                  xo_ref, no_ref, *, norm_row, rows):
    d = D_MODEL
    part = rows // SPLIT
    for sp in range(SPLIT):
        r = slice(sp * part, (sp + 1) * part)
        h = h_ref[r, :]
        merged = None
        for k, y_ref in enumerate((ya_ref, yb_ref, yc_ref, yd_ref)):
            gate = _sigmoid(_dot_nt(h, wm_ref[k * d:(k + 1) * d, :]) + mb_ref[k:k + 1, :])
            term = gate * _dot(y_ref[r, :], wbr_ref[k])
            merged = term if merged is None else merged + term
        xn = x_ref[r, :] + _dot(merged.astype(BF16), wo_ref[...])
        xo_ref[r, :] = xn
        no_ref[r, :] = _rms(xn, nw_ref[norm_row:norm_row + 1, :]).astype(no_ref.dtype)


def _merge(x2d, h2d, ys, wm, mb, wbr, wo, next_norm_w, norm_row, norm_dtype, layer):
    t, d = x2d.shape
    tile = lambda cols: pl.BlockSpec((MERGE_ROWS, cols), lambda i: (i, 0))
    return pl.pallas_call(
        functools.partial(_merge_kernel, norm_row=norm_row, rows=MERGE_ROWS),
        grid=(t // MERGE_ROWS,),
        in_specs=[tile(d), tile(d)] + [tile(WIDTH)] * N_BRANCH
                 + [_of_layer(wm, layer), _of_layer(mb, layer), _of_layer(wbr, layer), _of_layer(wo, layer),
                    _full(next_norm_w.shape)],
        out_specs=(tile(d), tile(d)),
        out_shape=(jax.ShapeDtypeStruct((t, d), F32), jax.ShapeDtypeStruct((t, d), norm_dtype)),
        compiler_params=_params("parallel"),
        name="merge",
    )(x2d, h2d, *ys, wm, mb, wbr, wo, next_norm_w)


def kernel(x, norm_w, w_in, s5_lambda_re, s5_lambda_im, s5_b_re, s5_b_im, s5_c_re, s5_c_im, s5_d,
           s5_log_step, s5_w_glu, sgu_ln_w, sgu_ln_b, sgu_w, sgu_b, m2_conv_w, m2_conv_b, m2_dt_bias,
           m2_a_log, m2_d, m2_norm_w, sc_conv_w, merge_b, w_branch, w_out, final_norm_w):
    b, l, d = x.shape
    t = b * l
    nl = w_in.shape[0]

    win = w_in.astype(BF16)
    w_sc = win[:, :, IN_OFFS[8]:IN_OFFS[12]]
    w_merge = jnp.swapaxes(win[:, :, IN_OFFS[12]:IN_OFFS[13]], 1, 2)
    used = SSD_LANE_GROUPS * M2_HEADS
    lane_groups = lambda v: jnp.pad(jnp.tile(v, (1,) * (v.ndim - 1) + (SSD_LANE_GROUPS,)),
                                    ((0, 0),) * (v.ndim - 1) + ((0, LANES - used),))
    wdt_groups = lane_groups(win[:, :, IN_OFFS[7]:IN_OFFS[8]])
    dtb_groups = lane_groups(m2_dt_bias)
    al_groups = lane_groups(m2_a_log)
    d_lanes = jnp.repeat(m2_d, M2_HEAD_DIM, axis=-1)
    expand = _ssd_expand_matrix()
    sgu_bias = jnp.repeat(sgu_b.transpose(0, 2, 1), SGU_HEAD_DIM, axis=-1)
    ab_re, ab_im, s5_wb, s5_cc = _s5_prep(s5_lambda_re, s5_lambda_im, s5_b_re, s5_b_im, s5_c_re, s5_c_im,
                                          s5_log_step)
    s5_dd = s5_d.reshape(nl, WIDTH)
    wglu = s5_w_glu.astype(BF16)
    wbr = w_branch.astype(BF16)
    wo = w_out.astype(BF16)
    norms = jnp.concatenate([norm_w, final_norm_w.reshape(1, d)], axis=0)

    x2d = x.reshape(t, d)
    h2d = _rmsnorm(x2d, norm_w)
    out = None
    for i in range(DEPTH):
        h = h2d.reshape(b, l, d)
        ya = _s5_branch(h, win, s5_wb, s5_cc, s5_dd, wglu, ab_re, ab_im, i)
        yb = _sgu_branch(h, win, sgu_ln_w, sgu_ln_b, sgu_w, sgu_bias, i)
        yc, yd = _ssd_sc_branch(h, win, wdt_groups, m2_conv_w, m2_conv_b, dtb_groups, al_groups, expand, d_lanes,
                                m2_norm_w, w_sc, sc_conv_w, i)
        last = i == DEPTH - 1
        x2d, h2d = _merge(x2d, h2d, [y.reshape(t, WIDTH) for y in (ya, yb, yc, yd)], w_merge, merge_b, wbr, wo,
                          norms, i + 1, F32 if last else BF16, i)
        out = h2d
    return out.reshape(b, l, d)
```

```python
import functools

import jax
import jax.numpy as jnp
import numpy as np
from jax import lax
from jax.experimental import pallas as pl
from jax.experimental.pallas import tpu as pltpu

F32 = jnp.float32
BF16 = jnp.bfloat16

D_MODEL = 1024
DEPTH = 2
WIDTH = 512
N_BRANCH = 4
EPS = 1e-6
S5_GROUP = 16
S5_GROUPS = WIDTH // S5_GROUP
S5_STATE = 64
S5_PAIRS = S5_GROUPS * S5_STATE
SGU_CHUNK = 128
SGU_HEADS = 8
SGU_HEAD_DIM = WIDTH // SGU_HEADS
M2_HEAD_DIM = 64
M2_HEADS = WIDTH // M2_HEAD_DIM
M2_GROUPS = 2
M2_STATE = 128
M2_CONV = 4
M2_CHUNK = 128
M2_CONV_CH = WIDTH + 2 * M2_GROUPS * M2_STATE
SC_CONV = 3

IN_SIZES = (WIDTH, WIDTH, WIDTH, WIDTH, WIDTH, WIDTH, M2_CONV_CH, M2_HEADS,
            WIDTH, WIDTH, WIDTH, WIDTH, N_BRANCH * D_MODEL)
IN_OFFS = tuple(int(v) for v in np.cumsum((0,) + IN_SIZES))

SUBLANES = 8
LANES = 128
MXU_DIM = 256
VMEM_LIMIT_BYTES = 56 * 1024 * 1024

S5_STEPS = 128
SGU_ROWS = 1024
SSD_ROWS = 1024
MERGE_ROWS = 1024
SPLIT = 2


def _dot(a, b):
    return jnp.dot(a, b, preferred_element_type=F32)


def _dot_nt(a, b):
    return lax.dot_general(a, b, (((1,), (1,)), ((), ())), preferred_element_type=F32)


def _dot_tn(a, b):
    return lax.dot_general(a, b, (((0,), (0,)), ((), ())), preferred_element_type=F32)


def _sigmoid(x):
    return 0.5 * (jnp.tanh(0.5 * x) + 1.0)


def _silu(x):
    hx = 0.5 * x
    return hx * jnp.tanh(hx) + hx


def _gelu(x):
    c = np.float32(np.sqrt(2.0 / np.pi))
    return x * (0.5 * (1.0 + jnp.tanh(c * (x + 0.044715 * (x * x * x)))))


def _softplus(x):
    return jnp.maximum(x, 0.0) + jnp.log1p(jnp.exp(-jnp.abs(x)))


def _rms(x, w):
    return x * lax.rsqrt(jnp.mean(x * x, axis=-1, keepdims=True) + EPS) * w


def _shift_rows(cat, j):
    n8, c = cat.shape
    rot = pltpu.roll(cat.reshape(n8 // SUBLANES, SUBLANES, c), j, 1)
    sub = lax.broadcasted_iota(jnp.int32, (1, SUBLANES, c), 1)
    out = jnp.where(sub >= j, rot[1:], rot[:-1])
    return out.reshape(n8 - SUBLANES, c)


def _params(*semantics):
    return pltpu.CompilerParams(dimension_semantics=semantics, vmem_limit_bytes=VMEM_LIMIT_BYTES)


RESIDENT = pl.Buffered(1)


def _full(shape):
    zeros = (0,) * len(shape)
    return pl.BlockSpec(tuple(shape), lambda *_: zeros, pipeline_mode=RESIDENT)


def _of_layer(arr, layer):
    zeros = (0,) * (arr.ndim - 1)
    return pl.BlockSpec((None,) + tuple(arr.shape[1:]), lambda *_: (layer,) + zeros, pipeline_mode=RESIDENT)


def _in_cols(layer, first, width):
    assert first % width == 0
    return pl.BlockSpec((None, D_MODEL, width), lambda *_: (layer, 0, first // width), pipeline_mode=RESIDENT)


def _seq_spec(rows, cols):
    return pl.BlockSpec((None, rows, cols), lambda b, c: (b, c, 0))


def _s5_prep_kernel(lr_ref, li_ref, ls_ref, bre_ref, bim_ref, abre_ref, abim_ref, bbre_ref, bbim_ref):
    lr, li = lr_ref[...], li_ref[...]
    step = jnp.exp(ls_ref[...])
    mag = jnp.exp(lr * step)
    ang = li * step
    ab_re, ab_im = mag * jnp.cos(ang), mag * jnp.sin(ang)
    abre_ref[...] = ab_re
    abim_ref[...] = ab_im
    den = lr * lr + li * li
    nr = ab_re - 1.0
    coef_re = (nr * lr + ab_im * li) / den
    coef_im = (ab_im * lr - nr * li) / den
    for layer in range(lr.shape[0]):
        cr, ci = coef_re[layer:layer + 1, :], coef_im[layer:layer + 1, :]
        br, bi = bre_ref[layer], bim_ref[layer]
        bbre_ref[layer] = cr * br - ci * bi
        bbim_ref[layer] = cr * bi + ci * br


def _s5_prep(lam_re, lam_im, b_re, b_im, c_re, c_im, log_step):
    nl = lam_re.shape[0]
    g, n, p = S5_GROUPS, S5_STATE, S5_GROUP
    ls = jnp.repeat(log_step, n, axis=-1)
    row = lambda v: v.reshape(nl, g * n)
    chan_minor = lambda v: v.transpose(0, 3, 1, 2).reshape(nl, p, g * n)
    ab_re, ab_im, bb_re, bb_im = pl.pallas_call(
        _s5_prep_kernel,
        out_shape=(jax.ShapeDtypeStruct((nl, g * n), F32), jax.ShapeDtypeStruct((nl, g * n), F32),
                   jax.ShapeDtypeStruct((nl, p, g * n), F32), jax.ShapeDtypeStruct((nl, p, g * n), F32)),
        name="s5_prep",
    )(row(lam_re), row(lam_im), row(ls), chan_minor(b_re), chan_minor(b_im))

    gh = g // 2
    in_mask = (np.arange(gh * p)[:, None] // p) == (np.arange(gh * n)[None, :] // n)
    out_mask = in_mask.T

    def in_half(bb):
        t = bb.reshape(nl, p, 2, gh, n).transpose(0, 2, 3, 1, 4).reshape(nl, 2, gh * p, n)
        return jnp.where(in_mask, jnp.tile(t, (1, 1, 1, gh)), 0.0)

    def out_half(c):
        t = c.reshape(nl, 2, gh, p, n).transpose(0, 1, 2, 4, 3).reshape(nl, 2, gh * n, p)
        return jnp.where(out_mask, jnp.tile(t, (1, 1, 1, gh)), 0.0)

    wb = jnp.concatenate([in_half(bb_re), in_half(bb_im)], axis=3).astype(BF16)
    cc = jnp.concatenate([out_half(c_re), -out_half(c_im)], axis=2).astype(BF16)
    return ab_re, ab_im, wb, cc


S5_HALF = S5_PAIRS
S5_PERM = MXU_DIM


def _perm_time_major(nb):
    tt = S5_PERM // nb
    i = lax.broadcasted_iota(jnp.int32, (S5_PERM, S5_PERM), 0)
    j = lax.broadcasted_iota(jnp.int32, (S5_PERM, S5_PERM), 1)
    return jnp.where(j == (i % nb) * tt + i // nb, 1.0, 0.0).astype(BF16)


def _s5_kernel(x_ref, nw_ref, wug_ref, wb_ref, cc_ref, d_ref, wglu_ref, abre_ref, abim_ref, o_ref, h_ref,
               hp_ref, ug_ref, st_ref, carry_ref, *, steps, nb, layer):
    @pl.when(pl.program_id(0) == 0)
    def _():
        carry_ref[...] = jnp.zeros(carry_ref.shape, F32)

    tt = S5_PERM // nb
    perm = _perm_time_major(nb)
    for k in range(steps // tt):
        hk = _rms(x_ref[:, k * tt:(k + 1) * tt, :], nw_ref[layer:layer + 1, :]).astype(BF16)
        h_ref[:, k * tt:(k + 1) * tt, :] = hk
        blk = hk.reshape(S5_PERM, D_MODEL)
        hp_ref[k * S5_PERM:(k + 1) * S5_PERM, :] = _dot(perm, blk).astype(BF16)

    ug_ref[:, 0:WIDTH] = _dot(hp_ref[...], wug_ref[:, 0:WIDTH])
    ub = ug_ref[:, 0:WIDTH].astype(BF16)
    hw = WIDTH // 2
    for hf in range(2):
        st_ref[:, hf * S5_HALF:(hf + 1) * S5_HALF] = _dot(ub[:, hf * hw:(hf + 1) * hw], wb_ref[hf])
    ug_ref[:, WIDTH:2 * WIDTH] = _dot(hp_ref[...], wug_ref[:, WIDTH:2 * WIDTH])

    pairs_half = S5_HALF // 2
    for hf in range(2):
        cre = hf * S5_HALF
        cim = cre + pairs_half
        tc = hf * pairs_half
        ar = jnp.broadcast_to(abre_ref[layer:layer + 1, tc:tc + pairs_half], (nb, pairs_half))
        ai = jnp.broadcast_to(abim_ref[layer:layer + 1, tc:tc + pairs_half], (nb, pairs_half))

        def step(t, carry, cre=cre, cim=cim, ar=ar, ai=ai):
            sr, si = carry
            r0 = pl.multiple_of(t * nb, nb)
            nr = (ar * sr - ai * si) + st_ref[pl.ds(r0, nb), cre:cre + pairs_half]
            ni = (ar * si + ai * sr) + st_ref[pl.ds(r0, nb), cim:cim + pairs_half]
            st_ref[pl.ds(r0, nb), cre:cre + pairs_half] = nr
            st_ref[pl.ds(r0, nb), cim:cim + pairs_half] = ni
            return nr, ni

        sr, si = lax.fori_loop(
            0, steps, step,
            (carry_ref[:, cre:cre + pairs_half], carry_ref[:, cim:cim + pairs_half]), unroll=True)
        carry_ref[:, cre:cre + pairs_half] = sr
        carry_ref[:, cim:cim + pairs_half] = si

    ys = []
    for hf in range(2):
        ys.append(_dot(st_ref[:, hf * S5_HALF:(hf + 1) * S5_HALF].astype(BF16), cc_ref[hf]))
    y = jnp.concatenate(ys, axis=-1) + d_ref[layer:layer + 1, :] * ug_ref[:, 0:WIDTH]
    y = _gelu(y)
    y = y * _sigmoid(_dot(y.astype(BF16), wglu_ref[...]))
    y = (y * _silu(ug_ref[:, WIDTH:2 * WIDTH])).astype(BF16)
    for k in range(steps // tt):
        back = _dot_tn(perm, y[k * S5_PERM:(k + 1) * S5_PERM, :])
        o_ref[:, k * tt:(k + 1) * tt, :] = back.reshape(nb, tt, WIDTH).astype(o_ref.dtype)


def _s5_branch(x, norm_w, win, wb, cc, d, wglu, ab_re, ab_im, layer):
    b, l, dm = x.shape
    assert S5_PERM % b == 0 and b % SUBLANES == 0
    steps = min(S5_STEPS, l)
    rows = steps * b
    tile = lambda cols: pl.BlockSpec((b, steps, cols), lambda c: (0, c, 0))
    return pl.pallas_call(
        functools.partial(_s5_kernel, steps=steps, nb=b, layer=layer),
        grid=(l // steps,),
        in_specs=[tile(dm), _full(norm_w.shape), _in_cols(layer, IN_OFFS[0], 2 * WIDTH),
                  _of_layer(wb, layer), _of_layer(cc, layer), _full(d.shape), _of_layer(wglu, layer),
                  _full(ab_re.shape), _full(ab_im.shape)],
        out_specs=(tile(WIDTH), tile(dm)),
        out_shape=(jax.ShapeDtypeStruct((b, l, WIDTH), BF16), jax.ShapeDtypeStruct((b, l, dm), BF16)),
        scratch_shapes=[pltpu.VMEM((rows, dm), BF16),
                        pltpu.VMEM((rows, 2 * WIDTH), F32),
                        pltpu.VMEM((rows, 2 * S5_HALF), F32),
                        pltpu.VMEM((b, 2 * S5_HALF), F32)],
        compiler_params=_params("arbitrary"),
        name="s5_branch",
    )(x, norm_w, win, wb, cc, d, wglu, ab_re, ab_im)


def _sgu_kernel(h_ref, wu_ref, wv_ref, wg_ref, lnw_ref, lnb_ref, ws_ref, bias_ref, o_ref, *, rows, layer):
    t = SGU_CHUNK
    causal = lax.broadcasted_iota(jnp.int32, (t, t), 0) >= lax.broadcasted_iota(jnp.int32, (t, t), 1)
    first_head = lax.broadcasted_iota(jnp.int32, (t, LANES), 1) < SGU_HEAD_DIM
    wm = [jnp.where(causal, ws_ref[hd], 0.0).astype(BF16) for hd in range(SGU_HEADS)]
    zero = jnp.zeros((), BF16)
    part = rows // SPLIT
    for sp in range(SPLIT):
        h = h_ref[sp * part:(sp + 1) * part, :]
        v = _gelu(_dot(h, wv_ref[...]))
        mu = jnp.mean(v, axis=-1, keepdims=True)
        vc = v - mu
        var = jnp.mean(vc * vc, axis=-1, keepdims=True)
        vn = (vc * lax.rsqrt(var + EPS) * lnw_ref[layer:layer + 1, :] + lnb_ref[layer:layer + 1, :]).astype(BF16)
        ug = _gelu(_dot(h, wu_ref[...])) * _silu(_dot(h, wg_ref[...]))
        for c in range(part // t):
            r = slice(c * t, (c + 1) * t)
            outs = []
            for j in range(WIDTH // LANES):
                blk = vn[r, j * LANES:(j + 1) * LANES]
                outs.append(_dot(wm[2 * j], jnp.where(first_head, blk, zero))
                            + _dot(wm[2 * j + 1], jnp.where(first_head, zero, blk)))
            s = jnp.concatenate(outs, axis=-1) + bias_ref[...]
            o_ref[sp * part + c * t:sp * part + (c + 1) * t, :] = (ug[r] * s).astype(o_ref.dtype)


def _sgu_branch(h, win, ln_w, ln_b, w_s, bias, layer):
    b, l, dm = h.shape
    rows = min(SGU_ROWS, l)
    return pl.pallas_call(
        functools.partial(_sgu_kernel, rows=rows, layer=layer),
        grid=(b, l // rows),
        in_specs=[_seq_spec(rows, dm)] + [_in_cols(layer, IN_OFFS[k], WIDTH) for k in (2, 3, 4)]
                 + [_full(ln_w.shape), _full(ln_b.shape), _of_layer(w_s, layer), _of_layer(bias, layer)],
        out_specs=_seq_spec(rows, WIDTH),
        out_shape=jax.ShapeDtypeStruct((b, l, WIDTH), BF16),
        compiler_params=_params("parallel", "parallel"),
        name="sgu_branch",
    )(h, win, win, win, ln_w, ln_b, w_s, bias)


SSD_QUANT = 3
SSD_PARTS = 3
SSD_LANE_GROUPS = SSD_QUANT * SSD_PARTS


def _ssd_expand_matrix():
    e = np.zeros((LANES, SSD_QUANT * WIDTH), np.float32)
    for grp in range(SSD_LANE_GROUPS):
        for hd in range(M2_HEADS):
            c0 = (grp // SSD_PARTS) * WIDTH + hd * M2_HEAD_DIM
            e[grp * M2_HEADS + hd, c0:c0 + M2_HEAD_DIM] = 1.0
    return jnp.asarray(e, BF16)


def _cumsum_rows(x):
    q = x.shape[0]
    row = lax.broadcasted_iota(jnp.int32, x.shape, 0)
    dist = 1
    while dist < q:
        x = x + jnp.where(row >= dist, pltpu.roll(x, dist, 0), 0.0)
        dist *= 2
    return x


def _ssd_sc_kernel(h_ref, wz_ref, wxbc_ref, wdt_ref, cw_ref, cb_ref, dtb_ref, al_ref, e_ref, d_ref, nw_ref,
                   wsc_ref, scw_ref, o_ref, osc_ref, prev_ref, st_ref, scprev_ref, *, rows, layer):
    @pl.when(pl.program_id(1) == 0)
    def _():
        prev_ref[...] = jnp.zeros(prev_ref.shape, F32)
        st_ref[...] = jnp.zeros(st_ref.shape, F32)
        scprev_ref[...] = jnp.zeros(scprev_ref.shape, F32)

    sc_proj = lambda j: _dot(h_ref[...], wsc_ref[:, j * WIDTH:(j + 1) * WIDTH])

    rpart = rows // SPLIT
    hs = [h_ref[sp * rpart:(sp + 1) * rpart, :] for sp in range(SPLIT)]
    raws = [_dot(hh, wxbc_ref[...]) for hh in hs]
    zs = [_dot(hh, wz_ref[...]) for hh in hs]
    dtrs = [_dot(hh, wdt_ref[...]) for hh in hs]
    sc_v = sc_proj(1) * sc_proj(2)
    prev = prev_ref[...]
    xbcs, dts, das = [], [], []
    neg_a = -jnp.exp(al_ref[layer:layer + 1, :])
    for sp in range(SPLIT):
        raw = raws[sp]
        cat = jnp.concatenate([prev, raw], axis=0)
        acc = cb_ref[layer:layer + 1, :] + cw_ref[M2_CONV - 1:M2_CONV, :] * raw
        for j in range(1, M2_CONV):
            k = M2_CONV - 1 - j
            acc = acc + cw_ref[k:k + 1, :] * _shift_rows(cat, j)
        prev = raw[rpart - SUBLANES:rpart, :]
        xbcs.append(_silu(acc))
        dts.append(_softplus(dtrs[sp] + dtb_ref[layer:layer + 1, :]))
        das.append(dts[sp] * neg_a)
    prev_ref[...] = prev
    grp = lax.broadcasted_iota(jnp.int32, (1, LANES), 1) // M2_HEADS
    quant = grp // SSD_PARTS
    part = grp % SSD_PARTS

    q = M2_CHUNK
    hd_w = M2_HEAD_DIM
    causal = lax.broadcasted_iota(jnp.int32, (q, q), 0) >= lax.broadcasted_iota(jnp.int32, (q, q), 1)
    first_head = lax.broadcasted_iota(jnp.int32, (q, LANES), 1) < hd_w
    zero = jnp.zeros((), BF16)
    gw = WIDTH // M2_GROUPS

    def chunk(s):
        sp = (s * q) // rpart
        xbc, z, dt, da = xbcs[sp], zs[sp], dts[sp], das[sp]
        x = xbc[:, 0:WIDTH]
        r = slice(s * q - sp * rpart, (s + 1) * q - sp * rpart)
        cs = _cumsum_rows(da[r])
        cs_row = cs.T[0:M2_HEADS, :]
        cs_last = cs[q - 1:q, :]
        val = jnp.where(quant == 0, dt[r], jnp.where(quant == 1, dt[r] * jnp.exp(cs_last - cs), jnp.exp(cs)))
        p0 = val.astype(BF16)
        r1 = val - p0.astype(F32)
        p1 = r1.astype(BF16)
        p2 = (r1 - p1.astype(F32)).astype(BF16)
        spread = _dot(jnp.where(part == 0, p0, jnp.where(part == 1, p1, p2)), e_ref[...])
        dt_e = spread[:, 0:WIDTH]
        dtdec_e = spread[:, WIDTH:2 * WIDTH]
        exp_cs = spread[:, 2 * WIDTH:3 * WIDTH]
        exp_last = exp_cs[q - 1:q, :]
        x_c = x[r]
        xdt_b = (x_c * dt_e).astype(BF16)
        xdec = (x_c * dtdec_e).astype(BF16)
        ys = []
        for g in range(M2_GROUPS):
            bo = WIDTH + g * M2_STATE
            co = WIDTH + M2_GROUPS * M2_STATE + g * M2_STATE
            bg = xbc[r, bo:bo + M2_STATE].astype(BF16)
            cg = xbc[r, co:co + M2_STATE].astype(BF16)
            scores = _dot_nt(cg, bg)
            state = st_ref[g]
            y_g = _dot(cg, state.astype(BF16)) * exp_cs[:, g * gw:(g + 1) * gw]
            diag = []
            for jp in range(gw // LANES):
                blk = xdt_b[:, g * gw + jp * LANES:g * gw + (jp + 1) * LANES]
                acc2 = None
                for qq in range(2):
                    hd = g * (M2_HEADS // M2_GROUPS) + jp * 2 + qq
                    seg = cs[:, hd:hd + 1] - cs_row[hd:hd + 1, :]
                    decay = jnp.exp(jnp.where(causal, seg, -jnp.inf))
                    m = (scores * decay).astype(BF16)
                    half = jnp.where(first_head, blk, zero) if qq == 0 else jnp.where(first_head, zero, blk)
                    term = _dot(m, half)
                    acc2 = term if acc2 is None else acc2 + term
                diag.append(acc2)
            ys.append(y_g + jnp.concatenate(diag, axis=-1))
            st_ref[g] = state * exp_last[:, g * gw:(g + 1) * gw] + _dot_tn(bg, xdec[:, g * gw:(g + 1) * gw])
        y = jnp.concatenate(ys, axis=-1) + d_ref[layer:layer + 1, :] * x_c
        y = y * _silu(z[r])
        o_ref[s * q:(s + 1) * q, :] = _rms(y, nw_ref[layer:layer + 1, :]).astype(o_ref.dtype)

    def sc_tail(sc_b, sc_gate):
        cat = jnp.concatenate([scprev_ref[...], sc_v], axis=0)
        acc = scw_ref[SC_CONV - 1:SC_CONV, :] * sc_v
        for j in range(1, SC_CONV):
            k = SC_CONV - 1 - j
            acc = acc + scw_ref[k:k + 1, :] * _shift_rows(cat, j)
        scprev_ref[...] = sc_v[rows - SUBLANES:rows, :]
        osc_ref[...] = (sc_b * acc * _silu(sc_gate)).astype(osc_ref.dtype)

    n_chunks = rows // q
    sc_b = sc_gate = None
    for s in range(n_chunks):
        chunk(s)
        if s == 0:
            sc_b = sc_proj(0)
        if s == min(1, n_chunks - 1):
            sc_gate = sc_proj(3)
        if s == min(2, n_chunks - 1):
            sc_tail(sc_b, sc_gate)


def _ssd_sc_branch(h, win, wdt_groups, conv_w, conv_b, dtb_groups, al_groups, expand, d_lanes, norm_w,
                   w_sc, sc_conv_w, layer):
    b, l, dm = h.shape
    rows = min(SSD_ROWS, l)
    out = jax.ShapeDtypeStruct((b, l, WIDTH), BF16)
    return pl.pallas_call(
        functools.partial(_ssd_sc_kernel, rows=rows, layer=layer),
        grid=(b, l // rows),
        in_specs=[_seq_spec(rows, dm), _in_cols(layer, IN_OFFS[5], WIDTH), _in_cols(layer, IN_OFFS[6], M2_CONV_CH),
                  _of_layer(wdt_groups, layer), _of_layer(conv_w, layer), _full(conv_b.shape),
                  _full(dtb_groups.shape), _full(al_groups.shape), _full(expand.shape), _full(d_lanes.shape),
                  _full(norm_w.shape), _of_layer(w_sc, layer), _of_layer(sc_conv_w, layer)],
        out_specs=(_seq_spec(rows, WIDTH), _seq_spec(rows, WIDTH)),
        out_shape=(out, out),
        scratch_shapes=[pltpu.VMEM((SUBLANES, M2_CONV_CH), F32),
                        pltpu.VMEM((M2_GROUPS, M2_STATE, WIDTH // M2_GROUPS), F32),
                        pltpu.VMEM((SUBLANES, WIDTH), F32)],
        compiler_params=_params("parallel", "arbitrary"),
        name="ssd_sc_branch",
    )(h, win, win, wdt_groups, conv_w, conv_b, dtb_groups, al_groups, expand, d_lanes, norm_w, w_sc, sc_conv_w)


def _merge_kernel(x_ref, h_ref, ya_ref, yb_ref, yc_ref, yd_ref, wm_ref, mb_ref, wbr_ref, wo_ref, nw_ref,
                  o_ref, *, final, rows):
    d = D_MODEL
    part = rows // SPLIT
    for sp in range(SPLIT):
        r = slice(sp * part, (sp + 1) * part)
        h = h_ref[r, :]
        merged = None
        for k, y_ref in enumerate((ya_ref, yb_ref, yc_ref, yd_ref)):
            gate = _sigmoid(_dot(h, wm_ref[:, k * d:(k + 1) * d]) + mb_ref[k:k + 1, :])
            term = gate * _dot(y_ref[r, :], wbr_ref[k])
            merged = term if merged is None else merged + term
        xn = x_ref[r, :] + _dot(merged.astype(BF16), wo_ref[...])
        o_ref[r, :] = _rms(xn, nw_ref[...]) if final else xn


def _merge(x2d, h2d, ys, wm, mb, wbr, wo, final_norm_w, layer, final):
    t, d = x2d.shape
    tile = lambda cols: pl.BlockSpec((MERGE_ROWS, cols), lambda i: (i, 0))
    return pl.pallas_call(
        functools.partial(_merge_kernel, final=final, rows=MERGE_ROWS),
        grid=(t // MERGE_ROWS,),
        in_specs=[tile(d), tile(d)] + [tile(WIDTH)] * N_BRANCH
                 + [_of_layer(wm, layer), _of_layer(mb, layer), _of_layer(wbr, layer), _of_layer(wo, layer),
                    _full((1, d))],
        out_specs=tile(d),
        out_shape=jax.ShapeDtypeStruct((t, d), F32),
        compiler_params=_params("parallel"),
        name="merge",
    )(x2d, h2d, *ys, wm, mb, wbr, wo, final_norm_w.reshape(1, d))


def kernel(x, norm_w, w_in, s5_lambda_re, s5_lambda_im, s5_b_re, s5_b_im, s5_c_re, s5_c_im, s5_d,
           s5_log_step, s5_w_glu, sgu_ln_w, sgu_ln_b, sgu_w, sgu_b, m2_conv_w, m2_conv_b, m2_dt_bias,
           m2_a_log, m2_d, m2_norm_w, sc_conv_w, merge_b, w_branch, w_out, final_norm_w):
    b, l, d = x.shape
    t = b * l
    nl = w_in.shape[0]

    win = w_in.astype(BF16)
    w_sc = win[:, :, IN_OFFS[8]:IN_OFFS[12]]
    w_merge = win[:, :, IN_OFFS[12]:IN_OFFS[13]]
    used = SSD_LANE_GROUPS * M2_HEADS
    lane_groups = lambda v: jnp.pad(jnp.tile(v, (1,) * (v.ndim - 1) + (SSD_LANE_GROUPS,)),
                                    ((0, 0),) * (v.ndim - 1) + ((0, LANES - used),))
    wdt_groups = lane_groups(win[:, :, IN_OFFS[7]:IN_OFFS[8]])
    dtb_groups = lane_groups(m2_dt_bias)
    al_groups = lane_groups(m2_a_log)
    d_lanes = jnp.repeat(m2_d, M2_HEAD_DIM, axis=-1)
    expand = _ssd_expand_matrix()
    sgu_bias = jnp.repeat(sgu_b.transpose(0, 2, 1), SGU_HEAD_DIM, axis=-1)
    ab_re, ab_im, s5_wb, s5_cc = _s5_prep(s5_lambda_re, s5_lambda_im, s5_b_re, s5_b_im, s5_c_re, s5_c_im,
                                          s5_log_step)
    s5_dd = s5_d.reshape(nl, WIDTH)
    wglu = s5_w_glu.astype(BF16)
    wbr = w_branch.astype(BF16)
    wo = w_out.astype(BF16)

    x2d = x.reshape(t, d)
    for i in range(DEPTH):
        ya, h = _s5_branch(x2d.reshape(b, l, d), norm_w, win, s5_wb, s5_cc, s5_dd, wglu, ab_re, ab_im, i)
        yb = _sgu_branch(h, win, sgu_ln_w, sgu_ln_b, sgu_w, sgu_bias, i)
        yc, yd = _ssd_sc_branch(h, win, wdt_groups, m2_conv_w, m2_conv_b, dtb_groups, al_groups, expand, d_lanes,
                                m2_norm_w, w_sc, sc_conv_w, i)
        x2d = _merge(x2d, h.reshape(t, d), [y.reshape(t, WIDTH) for y in (ya, yb, yc, yd)], w_merge, merge_b,
                     wbr, wo, final_norm_w, i, i == DEPTH - 1)
    return x2d.reshape(b, l, d)
```

```python
import functools

import jax
import jax.numpy as jnp
import numpy as np
from jax import lax
from jax.experimental import pallas as pl
from jax.experimental.pallas import tpu as pltpu

F32 = jnp.float32
BF16 = jnp.bfloat16

D_MODEL = 1024
DEPTH = 2
WIDTH = 512
N_BRANCH = 4
EPS = 1e-6
S5_GROUP = 16
S5_GROUPS = WIDTH // S5_GROUP
S5_STATE = 64
S5_PAIRS = S5_GROUPS * S5_STATE
SGU_CHUNK = 128
SGU_HEADS = 8
SGU_HEAD_DIM = WIDTH // SGU_HEADS
M2_HEAD_DIM = 64
M2_HEADS = WIDTH // M2_HEAD_DIM
M2_GROUPS = 2
M2_STATE = 128
M2_CONV = 4
M2_CHUNK = 128
M2_CONV_CH = WIDTH + 2 * M2_GROUPS * M2_STATE
SC_CONV = 3

IN_SIZES = (WIDTH, WIDTH, WIDTH, WIDTH, WIDTH, WIDTH, M2_CONV_CH, M2_HEADS,
            WIDTH, WIDTH, WIDTH, WIDTH, N_BRANCH * D_MODEL)
IN_OFFS = tuple(int(v) for v in np.cumsum((0,) + IN_SIZES))

SUBLANES = 8
LANES = 128
MXU_DIM = 256
VMEM_LIMIT_BYTES = 56 * 1024 * 1024

S5_STEPS = 128
SGU_ROWS = 1024
SSD_ROWS = 1024
MERGE_ROWS = 1024
SPLIT = 2


def _dot(a, b):
    return jnp.dot(a, b, preferred_element_type=F32)


def _dot_nt(a, b):
    return lax.dot_general(a, b, (((1,), (1,)), ((), ())), preferred_element_type=F32)


def _dot_tn(a, b):
    return lax.dot_general(a, b, (((0,), (0,)), ((), ())), preferred_element_type=F32)


def _sigmoid(x):
    return 0.5 * (jnp.tanh(0.5 * x) + 1.0)


def _silu(x):
    hx = 0.5 * x
    return hx * jnp.tanh(hx) + hx


def _gelu(x):
    c = np.float32(np.sqrt(2.0 / np.pi))
    return x * (0.5 * (1.0 + jnp.tanh(c * (x + 0.044715 * (x * x * x)))))


def _softplus(x):
    return jnp.maximum(x, 0.0) + jnp.log1p(jnp.exp(-jnp.abs(x)))


def _rms(x, w):
    return x * lax.rsqrt(jnp.mean(x * x, axis=-1, keepdims=True) + EPS) * w


def _shift_rows(cat, j):
    n8, c = cat.shape
    rot = pltpu.roll(cat.reshape(n8 // SUBLANES, SUBLANES, c), j, 1)
    sub = lax.broadcasted_iota(jnp.int32, (1, SUBLANES, c), 1)
    out = jnp.where(sub >= j, rot[1:], rot[:-1])
    return out.reshape(n8 - SUBLANES, c)


def _params(*semantics):
    return pltpu.CompilerParams(dimension_semantics=semantics, vmem_limit_bytes=VMEM_LIMIT_BYTES)


RESIDENT = pl.Buffered(1)


def _full(shape):
    zeros = (0,) * len(shape)
    return pl.BlockSpec(tuple(shape), lambda *_: zeros, pipeline_mode=RESIDENT)


def _of_layer(arr, layer):
    zeros = (0,) * (arr.ndim - 1)
    return pl.BlockSpec((None,) + tuple(arr.shape[1:]), lambda *_: (layer,) + zeros, pipeline_mode=RESIDENT)


def _in_cols(layer, first, width):
    assert first % width == 0
    return pl.BlockSpec((None, D_MODEL, width), lambda *_: (layer, 0, first // width), pipeline_mode=RESIDENT)


def _seq_spec(rows, cols):
    return pl.BlockSpec((None, rows, cols), lambda b, c: (b, c, 0))


def _s5_prep_kernel(lr_ref, li_ref, ls_ref, bre_ref, bim_ref, cre_ref, cim_ref,
                    abre_ref, abim_ref, wb_ref, cc_ref):
    gh = S5_GROUPS // 2
    rows_gp, cols_gn = gh * S5_GROUP, gh * S5_STATE
    own_group = (lax.broadcasted_iota(jnp.int32, (rows_gp, cols_gn), 0) // S5_GROUP
                 == lax.broadcasted_iota(jnp.int32, (rows_gp, cols_gn), 1) // S5_STATE)

    def block_diag(per_p):
        tiled = jnp.broadcast_to(per_p[None], (gh,) + per_p.shape).reshape(rows_gp, cols_gn)
        return jnp.where(own_group, tiled, 0.0)

    lr, li = lr_ref[...], li_ref[...]
    step = jnp.exp(ls_ref[...])
    mag = jnp.exp(lr * step)
    ang = li * step
    ab_re, ab_im = mag * jnp.cos(ang), mag * jnp.sin(ang)
    abre_ref[...] = ab_re
    abim_ref[...] = ab_im
    den = lr * lr + li * li
    nr = ab_re - 1.0
    coef_re = (nr * lr + ab_im * li) / den
    coef_im = (ab_im * lr - nr * li) / den
    for layer in range(lr.shape[0]):
        for hf in range(2):
            cols = slice(hf * cols_gn, (hf + 1) * cols_gn)
            cr, ci = coef_re[layer:layer + 1, cols], coef_im[layer:layer + 1, cols]
            br, bi = bre_ref[layer, :, cols], bim_ref[layer, :, cols]
            wb_ref[layer, hf, :, 0:cols_gn] = block_diag(cr * br - ci * bi).astype(BF16)
            wb_ref[layer, hf, :, cols_gn:2 * cols_gn] = block_diag(cr * bi + ci * br).astype(BF16)
            cc_ref[layer, hf, 0:cols_gn, :] = block_diag(cre_ref[layer, :, cols]).T.astype(BF16)
            cc_ref[layer, hf, cols_gn:2 * cols_gn, :] = (-block_diag(cim_ref[layer, :, cols])).T.astype(BF16)


def _s5_prep(lam_re, lam_im, b_re, b_im, c_re, c_im, log_step):
    nl = lam_re.shape[0]
    g, n, p = S5_GROUPS, S5_STATE, S5_GROUP
    ls = jnp.repeat(log_step, n, axis=-1)
    row = lambda v: v.reshape(nl, g * n)
    b_chan = lambda v: v.transpose(0, 3, 1, 2).reshape(nl, p, g * n)
    c_chan = lambda v: v.transpose(0, 2, 1, 3).reshape(nl, p, g * n)
    gh = g // 2
    return pl.pallas_call(
        _s5_prep_kernel,
        out_shape=(jax.ShapeDtypeStruct((nl, g * n), F32), jax.ShapeDtypeStruct((nl, g * n), F32),
                   jax.ShapeDtypeStruct((nl, 2, gh * p, 2 * gh * n), BF16),
                   jax.ShapeDtypeStruct((nl, 2, 2 * gh * n, gh * p), BF16)),
        name="s5_prep",
    )(row(lam_re), row(lam_im), row(ls), b_chan(b_re), b_chan(b_im), c_chan(c_re), c_chan(c_im))


S5_HALF = S5_PAIRS
S5_PERM = MXU_DIM


def _perm_time_major(nb):
    tt = S5_PERM // nb
    i = lax.broadcasted_iota(jnp.int32, (S5_PERM, S5_PERM), 0)
    j = lax.broadcasted_iota(jnp.int32, (S5_PERM, S5_PERM), 1)
    return jnp.where(j == (i % nb) * tt + i // nb, 1.0, 0.0).astype(BF16)


def _s5_kernel(x_ref, nw_ref, wug_ref, wb_ref, cc_ref, d_ref, wglu_ref, abre_ref, abim_ref, o_ref, h_ref,
               hp_ref, ug_ref, st_ref, carry_ref, *, steps, nb, layer):
    @pl.when(pl.program_id(0) == 0)
    def _():
        carry_ref[...] = jnp.zeros(carry_ref.shape, F32)

    tt = S5_PERM // nb
    perm = _perm_time_major(nb)
    for k in range(steps // tt):
        hk = _rms(x_ref[:, k * tt:(k + 1) * tt, :], nw_ref[layer:layer + 1, :]).astype(BF16)
        h_ref[:, k * tt:(k + 1) * tt, :] = hk
        blk = hk.reshape(S5_PERM, D_MODEL)
        hp_ref[k * S5_PERM:(k + 1) * S5_PERM, :] = _dot(perm, blk).astype(BF16)

    ug_ref[:, 0:WIDTH] = _dot(hp_ref[...], wug_ref[:, 0:WIDTH])
    ub = ug_ref[:, 0:WIDTH].astype(BF16)
    hw = WIDTH // 2
    for hf in range(2):
        st_ref[:, hf * S5_HALF:(hf + 1) * S5_HALF] = _dot(ub[:, hf * hw:(hf + 1) * hw], wb_ref[hf])
    ug_ref[:, WIDTH:2 * WIDTH] = _dot(hp_ref[...], wug_ref[:, WIDTH:2 * WIDTH])

    pairs_half = S5_HALF // 2
    for hf in range(2):
        cre = hf * S5_HALF
        cim = cre + pairs_half
        tc = hf * pairs_half
        ar = jnp.broadcast_to(abre_ref[layer:layer + 1, tc:tc + pairs_half], (nb, pairs_half))
        ai = jnp.broadcast_to(abim_ref[layer:layer + 1, tc:tc + pairs_half], (nb, pairs_half))

        def step(t, carry, cre=cre, cim=cim, ar=ar, ai=ai):
            sr, si = carry
            r0 = pl.multiple_of(t * nb, nb)
            nr = (ar * sr - ai * si) + st_ref[pl.ds(r0, nb), cre:cre + pairs_half]
            ni = (ar * si + ai * sr) + st_ref[pl.ds(r0, nb), cim:cim + pairs_half]
            st_ref[pl.ds(r0, nb), cre:cre + pairs_half] = nr
            st_ref[pl.ds(r0, nb), cim:cim + pairs_half] = ni
            return nr, ni

        sr, si = lax.fori_loop(
            0, steps, step,
            (carry_ref[:, cre:cre + pairs_half], carry_ref[:, cim:cim + pairs_half]), unroll=True)
        carry_ref[:, cre:cre + pairs_half] = sr
        carry_ref[:, cim:cim + pairs_half] = si

    ys = []
    for hf in range(2):
        ys.append(_dot(st_ref[:, hf * S5_HALF:(hf + 1) * S5_HALF].astype(BF16), cc_ref[hf]))
    y = jnp.concatenate(ys, axis=-1) + d_ref[layer:layer + 1, :] * ug_ref[:, 0:WIDTH]
    y = _gelu(y)
    y = y * _sigmoid(_dot(y.astype(BF16), wglu_ref[...]))
    y = (y * _silu(ug_ref[:, WIDTH:2 * WIDTH])).astype(BF16)
    for k in range(steps // tt):
        back = _dot_tn(perm, y[k * S5_PERM:(k + 1) * S5_PERM, :])
        o_ref[:, k * tt:(k + 1) * tt, :] = back.reshape(nb, tt, WIDTH).astype(o_ref.dtype)


def _s5_branch(x, norm_w, win, wb, cc, d, wglu, ab_re, ab_im, layer):
    b, l, dm = x.shape
    assert S5_PERM % b == 0 and b % SUBLANES == 0
    steps = min(S5_STEPS, l)
    rows = steps * b
    tile = lambda cols: pl.BlockSpec((b, steps, cols), lambda c: (0, c, 0))
    return pl.pallas_call(
        functools.partial(_s5_kernel, steps=steps, nb=b, layer=layer),
        grid=(l // steps,),
        in_specs=[tile(dm), _full(norm_w.shape), _in_cols(layer, IN_OFFS[0], 2 * WIDTH),
                  _of_layer(wb, layer), _of_layer(cc, layer), _full(d.shape), _of_layer(wglu, layer),
                  _full(ab_re.shape), _full(ab_im.shape)],
        out_specs=(tile(WIDTH), tile(dm)),
        out_shape=(jax.ShapeDtypeStruct((b, l, WIDTH), BF16), jax.ShapeDtypeStruct((b, l, dm), BF16)),
        scratch_shapes=[pltpu.VMEM((rows, dm), BF16),
                        pltpu.VMEM((rows, 2 * WIDTH), F32),
                        pltpu.VMEM((rows, 2 * S5_HALF), F32),
                        pltpu.VMEM((b, 2 * S5_HALF), F32)],
        compiler_params=_params("arbitrary"),
        name="s5_branch",
    )(x, norm_w, win, wb, cc, d, wglu, ab_re, ab_im)


def _sgu_kernel(h_ref, wu_ref, wv_ref, wg_ref, lnw_ref, lnb_ref, ws_ref, bias_ref, o_ref, *, rows, layer):
    t = SGU_CHUNK
    causal = lax.broadcasted_iota(jnp.int32, (t, t), 0) >= lax.broadcasted_iota(jnp.int32, (t, t), 1)
    first_head = lax.broadcasted_iota(jnp.int32, (t, LANES), 1) < SGU_HEAD_DIM
    wm = [jnp.where(causal, ws_ref[hd], 0.0).astype(BF16) for hd in range(SGU_HEADS)]
    zero = jnp.zeros((), BF16)
    part = rows // SPLIT
    for sp in range(SPLIT):
        h = h_ref[sp * part:(sp + 1) * part, :]
        v = _gelu(_dot(h, wv_ref[...]))
        mu = jnp.mean(v, axis=-1, keepdims=True)
        vc = v - mu
        var = jnp.mean(vc * vc, axis=-1, keepdims=True)
        vn = (vc * lax.rsqrt(var + EPS) * lnw_ref[layer:layer + 1, :] + lnb_ref[layer:layer + 1, :]).astype(BF16)
        ug = _gelu(_dot(h, wu_ref[...])) * _silu(_dot(h, wg_ref[...]))
        for c in range(part // t):
            r = slice(c * t, (c + 1) * t)
            outs = []
            for j in range(WIDTH // LANES):
                blk = vn[r, j * LANES:(j + 1) * LANES]
                outs.append(_dot(wm[2 * j], jnp.where(first_head, blk, zero))
                            + _dot(wm[2 * j + 1], jnp.where(first_head, zero, blk)))
            s = jnp.concatenate(outs, axis=-1) + bias_ref[...]
            o_ref[sp * part + c * t:sp * part + (c + 1) * t, :] = (ug[r] * s).astype(o_ref.dtype)


def _sgu_branch(h, win, ln_w, ln_b, w_s, bias, layer):
    b, l, dm = h.shape
    rows = min(SGU_ROWS, l)
    return pl.pallas_call(
        functools.partial(_sgu_kernel, rows=rows, layer=layer),
        grid=(b, l // rows),
        in_specs=[_seq_spec(rows, dm)] + [_in_cols(layer, IN_OFFS[k], WIDTH) for k in (2, 3, 4)]
                 + [_full(ln_w.shape), _full(ln_b.shape), _of_layer(w_s, layer), _of_layer(bias, layer)],
        out_specs=_seq_spec(rows, WIDTH),
        out_shape=jax.ShapeDtypeStruct((b, l, WIDTH), BF16),
        compiler_params=_params("parallel", "parallel"),
        name="sgu_branch",
    )(h, win, win, win, ln_w, ln_b, w_s, bias)


SSD_QUANT = 3
SSD_PARTS = 3
SSD_LANE_GROUPS = SSD_QUANT * SSD_PARTS


def _ssd_expand_matrix():
    e = np.zeros((LANES, SSD_QUANT * WIDTH), np.float32)
    for grp in range(SSD_LANE_GROUPS):
        for hd in range(M2_HEADS):
            c0 = (grp // SSD_PARTS) * WIDTH + hd * M2_HEAD_DIM
            e[grp * M2_HEADS + hd, c0:c0 + M2_HEAD_DIM] = 1.0
    return jnp.asarray(e, BF16)


def _cumsum_rows(x):
    q = x.shape[0]
    row = lax.broadcasted_iota(jnp.int32, x.shape, 0)
    dist = 1
    while dist < q:
        x = x + jnp.where(row >= dist, pltpu.roll(x, dist, 0), 0.0)
        dist *= 2
    return x


def _ssd_sc_kernel(h_ref, wz_ref, wxbc_ref, wdt_ref, cw_ref, cb_ref, dtb_ref, al_ref, e_ref, d_ref, nw_ref,
                   wsc_ref, scw_ref, o_ref, osc_ref, prev_ref, st_ref, scprev_ref, *, rows, layer):
    @pl.when(pl.program_id(1) == 0)
    def _():
        prev_ref[...] = jnp.zeros(prev_ref.shape, F32)
        st_ref[...] = jnp.zeros(st_ref.shape, F32)
        scprev_ref[...] = jnp.zeros(scprev_ref.shape, F32)

    sc_proj = lambda j: _dot(h_ref[...], wsc_ref[:, j * WIDTH:(j + 1) * WIDTH])

    rpart = rows // SPLIT
    hs = [h_ref[sp * rpart:(sp + 1) * rpart, :] for sp in range(SPLIT)]
    raws = [_dot(hh, wxbc_ref[...]) for hh in hs]
    zs = [_dot(hh, wz_ref[...]) for hh in hs]
    dtrs = [_dot(hh, wdt_ref[...]) for hh in hs]
    sc_v = sc_proj(1) * sc_proj(2)
    prev = prev_ref[...]
    xbcs, dts, das = [], [], []
    neg_a = -jnp.exp(al_ref[layer:layer + 1, :])
    for sp in range(SPLIT):
        raw = raws[sp]
        cat = jnp.concatenate([prev, raw], axis=0)
        acc = cb_ref[layer:layer + 1, :] + cw_ref[M2_CONV - 1:M2_CONV, :] * raw
        for j in range(1, M2_CONV):
            k = M2_CONV - 1 - j
            acc = acc + cw_ref[k:k + 1, :] * _shift_rows(cat, j)
        prev = raw[rpart - SUBLANES:rpart, :]
        xbcs.append(_silu(acc))
        dts.append(_softplus(dtrs[sp] + dtb_ref[layer:layer + 1, :]))
        das.append(dts[sp] * neg_a)
    prev_ref[...] = prev
    grp = lax.broadcasted_iota(jnp.int32, (1, LANES), 1) // M2_HEADS
    quant = grp // SSD_PARTS
    part = grp % SSD_PARTS

    q = M2_CHUNK
    hd_w = M2_HEAD_DIM
    causal = lax.broadcasted_iota(jnp.int32, (q, q), 0) >= lax.broadcasted_iota(jnp.int32, (q, q), 1)
    first_head = lax.broadcasted_iota(jnp.int32, (q, LANES), 1) < hd_w
    zero = jnp.zeros((), BF16)
    gw = WIDTH // M2_GROUPS

    def chunk(s):
        sp = (s * q) // rpart
        xbc, z, dt, da = xbcs[sp], zs[sp], dts[sp], das[sp]
        x = xbc[:, 0:WIDTH]
        r = slice(s * q - sp * rpart, (s + 1) * q - sp * rpart)
        cs = _cumsum_rows(da[r])
        cs_row = cs.T[0:M2_HEADS, :]
        cs_last = cs[q - 1:q, :]
        val = jnp.where(quant == 0, dt[r], jnp.where(quant == 1, dt[r] * jnp.exp(cs_last - cs), jnp.exp(cs)))
        p0 = val.astype(BF16)
        r1 = val - p0.astype(F32)
        p1 = r1.astype(BF16)
        p2 = (r1 - p1.astype(F32)).astype(BF16)
        spread = _dot(jnp.where(part == 0, p0, jnp.where(part == 1, p1, p2)), e_ref[...])
        dt_e = spread[:, 0:WIDTH]
        dtdec_e = spread[:, WIDTH:2 * WIDTH]
        exp_cs = spread[:, 2 * WIDTH:3 * WIDTH]
        exp_last = exp_cs[q - 1:q, :]
        x_c = x[r]
        xdt_b = (x_c * dt_e).astype(BF16)
        xdec = (x_c * dtdec_e).astype(BF16)
        ys = []
        for g in range(M2_GROUPS):
            bo = WIDTH + g * M2_STATE
            co = WIDTH + M2_GROUPS * M2_STATE + g * M2_STATE
            bg = xbc[r, bo:bo + M2_STATE].astype(BF16)
            cg = xbc[r, co:co + M2_STATE].astype(BF16)
            scores = _dot_nt(cg, bg)
            state = st_ref[g]
            y_g = _dot(cg, state.astype(BF16)) * exp_cs[:, g * gw:(g + 1) * gw]
            diag = []
            for jp in range(gw // LANES):
                blk = xdt_b[:, g * gw + jp * LANES:g * gw + (jp + 1) * LANES]
                acc2 = None
                for qq in range(2):
                    hd = g * (M2_HEADS // M2_GROUPS) + jp * 2 + qq
                    seg = cs[:, hd:hd + 1] - cs_row[hd:hd + 1, :]
                    decay = jnp.exp(jnp.where(causal, seg, -jnp.inf))
                    m = (scores * decay).astype(BF16)
                    half = jnp.where(first_head, blk, zero) if qq == 0 else jnp.where(first_head, zero, blk)
                    term = _dot(m, half)
                    acc2 = term if acc2 is None else acc2 + term
                diag.append(acc2)
            ys.append(y_g + jnp.concatenate(diag, axis=-1))
            st_ref[g] = state * exp_last[:, g * gw:(g + 1) * gw] + _dot_tn(bg, xdec[:, g * gw:(g + 1) * gw])
        y = jnp.concatenate(ys, axis=-1) + d_ref[layer:layer + 1, :] * x_c
        y = y * _silu(z[r])
        o_ref[s * q:(s + 1) * q, :] = _rms(y, nw_ref[layer:layer + 1, :]).astype(o_ref.dtype)

    def sc_tail(sc_b, sc_gate):
        cat = jnp.concatenate([scprev_ref[...], sc_v], axis=0)
        acc = scw_ref[SC_CONV - 1:SC_CONV, :] * sc_v
        for j in range(1, SC_CONV):
            k = SC_CONV - 1 - j
            acc = acc + scw_ref[k:k + 1, :] * _shift_rows(cat, j)
        scprev_ref[...] = sc_v[rows - SUBLANES:rows, :]
        osc_ref[...] = (sc_b * acc * _silu(sc_gate)).astype(osc_ref.dtype)

    n_chunks = rows // q
    sc_b = sc_gate = None
    for s in range(n_chunks):
        chunk(s)
        if s == 0:
            sc_b = sc_proj(0)
        if s == min(1, n_chunks - 1):
            sc_gate = sc_proj(3)
        if s == min(2, n_chunks - 1):
            sc_tail(sc_b, sc_gate)


def _ssd_sc_branch(h, win, wdt_groups, conv_w, conv_b, dtb_groups, al_groups, expand, d_lanes, norm_w,
                   w_sc, sc_conv_w, layer):
    b, l, dm = h.shape
    rows = min(SSD_ROWS, l)
    out = jax.ShapeDtypeStruct((b, l, WIDTH), BF16)
    return pl.pallas_call(
        functools.partial(_ssd_sc_kernel, rows=rows, layer=layer),
        grid=(b, l // rows),
        in_specs=[_seq_spec(rows, dm), _in_cols(layer, IN_OFFS[5], WIDTH), _in_cols(layer, IN_OFFS[6], M2_CONV_CH),
                  _of_layer(wdt_groups, layer), _of_layer(conv_w, layer), _full(conv_b.shape),
                  _full(dtb_groups.shape), _full(al_groups.shape), _full(expand.shape), _full(d_lanes.shape),
                  _full(norm_w.shape), _of_layer(w_sc, layer), _of_layer(sc_conv_w, layer)],
        out_specs=(_seq_spec(rows, WIDTH), _seq_spec(rows, WIDTH)),
        out_shape=(out, out),
        scratch_shapes=[pltpu.VMEM((SUBLANES, M2_CONV_CH), F32),
                        pltpu.VMEM((M2_GROUPS, M2_STATE, WIDTH // M2_GROUPS), F32),
                        pltpu.VMEM((SUBLANES, WIDTH), F32)],
        compiler_params=_params("parallel", "arbitrary"),
        name="ssd_sc_branch",
    )(h, win, win, wdt_groups, conv_w, conv_b, dtb_groups, al_groups, expand, d_lanes, norm_w, w_sc, sc_conv_w)


def _merge_kernel(x_ref, h_ref, ya_ref, yb_ref, yc_ref, yd_ref, wm_ref, mb_ref, wbr_ref, wo_ref, nw_ref,
                  o_ref, *, final, rows):
    d = D_MODEL
    part = rows // SPLIT
    for sp in range(SPLIT):
        r = slice(sp * part, (sp + 1) * part)
        h = h_ref[r, :]
        merged = None
        for k, y_ref in enumerate((ya_ref, yb_ref, yc_ref, yd_ref)):
            gate = _sigmoid(_dot(h, wm_ref[:, k * d:(k + 1) * d]) + mb_ref[k:k + 1, :])
            term = gate * _dot(y_ref[r, :], wbr_ref[k])
            merged = term if merged is None else merged + term
        xn = x_ref[r, :] + _dot(merged.astype(BF16), wo_ref[...])
        o_ref[r, :] = _rms(xn, nw_ref[...]) if final else xn


def _merge(x2d, h2d, ys, wm, mb, wbr, wo, final_norm_w, layer, final):
    t, d = x2d.shape
    tile = lambda cols: pl.BlockSpec((MERGE_ROWS, cols), lambda i: (i, 0))
    return pl.pallas_call(
        functools.partial(_merge_kernel, final=final, rows=MERGE_ROWS),
        grid=(t // MERGE_ROWS,),
        in_specs=[tile(d), tile(d)] + [tile(WIDTH)] * N_BRANCH
                 + [_of_layer(wm, layer), _of_layer(mb, layer), _of_layer(wbr, layer), _of_layer(wo, layer),
                    _full((1, d))],
        out_specs=tile(d),
        out_shape=jax.ShapeDtypeStruct((t, d), F32),
        compiler_params=_params("parallel"),
        name="merge",
    )(x2d, h2d, *ys, wm, mb, wbr, wo, final_norm_w.reshape(1, d))


def kernel(x, norm_w, w_in, s5_lambda_re, s5_lambda_im, s5_b_re, s5_b_im, s5_c_re, s5_c_im, s5_d,
           s5_log_step, s5_w_glu, sgu_ln_w, sgu_ln_b, sgu_w, sgu_b, m2_conv_w, m2_conv_b, m2_dt_bias,
           m2_a_log, m2_d, m2_norm_w, sc_conv_w, merge_b, w_branch, w_out, final_norm_w):
    b, l, d = x.shape
    t = b * l
    nl = w_in.shape[0]

    win = w_in.astype(BF16)
    w_sc = win[:, :, IN_OFFS[8]:IN_OFFS[12]]
    w_merge = win[:, :, IN_OFFS[12]:IN_OFFS[13]]
    used = SSD_LANE_GROUPS * M2_HEADS
    lane_groups = lambda v: jnp.pad(jnp.tile(v, (1,) * (v.ndim - 1) + (SSD_LANE_GROUPS,)),
                                    ((0, 0),) * (v.ndim - 1) + ((0, LANES - used),))
    wdt_groups = lane_groups(win[:, :, IN_OFFS[7]:IN_OFFS[8]])
    dtb_groups = lane_groups(m2_dt_bias)
    al_groups = lane_groups(m2_a_log)
    d_lanes = jnp.repeat(m2_d, M2_HEAD_DIM, axis=-1)
    expand = _ssd_expand_matrix()
    sgu_bias = jnp.repeat(sgu_b.transpose(0, 2, 1), SGU_HEAD_DIM, axis=-1)
    ab_re, ab_im, s5_wb, s5_cc = _s5_prep(s5_lambda_re, s5_lambda_im, s5_b_re, s5_b_im, s5_c_re, s5_c_im,
                                          s5_log_step)
    s5_dd = s5_d.reshape(nl, WIDTH)
    wglu = s5_w_glu.astype(BF16)
    wbr = w_branch.astype(BF16)
    wo = w_out.astype(BF16)

    x2d = x.reshape(t, d)
    for i in range(DEPTH):
        ya, h = _s5_branch(x2d.reshape(b, l, d), norm_w, win, s5_wb, s5_cc, s5_dd, wglu, ab_re, ab_im, i)
        yb = _sgu_branch(h, win, sgu_ln_w, sgu_ln_b, sgu_w, sgu_bias, i)
        yc, yd = _ssd_sc_branch(h, win, wdt_groups, m2_conv_w, m2_conv_b, dtb_groups, al_groups, expand, d_lanes,
                                m2_norm_w, w_sc, sc_conv_w, i)
        x2d = _merge(x2d, h.reshape(t, d), [y.reshape(t, WIDTH) for y in (ya, yb, yc, yd)], w_merge, merge_b,
                     wbr, wo, final_norm_w, i, i == DEPTH - 1)
    return x2d.reshape(b, l, d)
```

```python
import functools

import jax
import jax.numpy as jnp
import numpy as np
from jax import lax
from jax.experimental import pallas as pl
from jax.experimental.pallas import tpu as pltpu

F32 = jnp.float32
BF16 = jnp.bfloat16

D_MODEL = 1024
DEPTH = 2
WIDTH = 512
N_BRANCH = 4
EPS = 1e-6
S5_GROUP = 16
S5_GROUPS = WIDTH // S5_GROUP
S5_STATE = 64
S5_PAIRS = S5_GROUPS * S5_STATE
SGU_CHUNK = 128
SGU_HEADS = 8
SGU_HEAD_DIM = WIDTH // SGU_HEADS
M2_HEAD_DIM = 64
M2_HEADS = WIDTH // M2_HEAD_DIM
M2_GROUPS = 2
M2_STATE = 128
M2_CONV = 4
M2_CHUNK = 128
M2_CONV_CH = WIDTH + 2 * M2_GROUPS * M2_STATE
SC_CONV = 3

IN_SIZES = (WIDTH, WIDTH, WIDTH, WIDTH, WIDTH, WIDTH, M2_CONV_CH, M2_HEADS,
            WIDTH, WIDTH, WIDTH, WIDTH, N_BRANCH * D_MODEL)
IN_OFFS = tuple(int(v) for v in np.cumsum((0,) + IN_SIZES))
TAIL_FIRST = IN_OFFS[8]
TAIL_BLOCK = IN_OFFS[12] - IN_OFFS[8]
MERGE_BLOCKS = (IN_OFFS[13] - IN_OFFS[12]) // TAIL_BLOCK
assert MERGE_BLOCKS * TAIL_BLOCK == N_BRANCH * D_MODEL and TAIL_BLOCK % D_MODEL == 0

SUBLANES = 8
LANES = 128
MXU_DIM = 256
VMEM_LIMIT_BYTES = 56 * 1024 * 1024

S5_STEPS = 128
SGU_ROWS = 1024
SSD_ROWS = 1024
MERGE_ROWS = 1024
SPLIT = 2


def _dot(a, b):
    return jnp.dot(a, b, preferred_element_type=F32)


def _dot_nt(a, b):
    return lax.dot_general(a, b, (((1,), (1,)), ((), ())), preferred_element_type=F32)


def _dot_tn(a, b):
    return lax.dot_general(a, b, (((0,), (0,)), ((), ())), preferred_element_type=F32)


def _sigmoid(x):
    return 0.5 * (jnp.tanh(0.5 * x) + 1.0)


def _silu(x):
    hx = 0.5 * x
    return hx * jnp.tanh(hx) + hx


def _gelu(x):
    c = np.float32(np.sqrt(2.0 / np.pi))
    return x * (0.5 * (1.0 + jnp.tanh(c * (x + 0.044715 * (x * x * x)))))


def _softplus(x):
    return jnp.maximum(x, 0.0) + jnp.log1p(jnp.exp(-jnp.abs(x)))


def _rms(x, w):
    return x * lax.rsqrt(jnp.mean(x * x, axis=-1, keepdims=True) + EPS) * w


def _shift_rows(cat, j):
    n8, c = cat.shape
    rot = pltpu.roll(cat.reshape(n8 // SUBLANES, SUBLANES, c), j, 1)
    sub = lax.broadcasted_iota(jnp.int32, (1, SUBLANES, c), 1)
    out = jnp.where(sub >= j, rot[1:], rot[:-1])
    return out.reshape(n8 - SUBLANES, c)


def _params(*semantics):
    return pltpu.CompilerParams(dimension_semantics=semantics, vmem_limit_bytes=VMEM_LIMIT_BYTES)


RESIDENT = pl.Buffered(1)


def _full(shape):
    zeros = (0,) * len(shape)
    return pl.BlockSpec(tuple(shape), lambda *_: zeros, pipeline_mode=RESIDENT)


def _of_layer(arr, layer):
    zeros = (0,) * (arr.ndim - 1)
    return pl.BlockSpec((None,) + tuple(arr.shape[1:]), lambda *_: (layer,) + zeros, pipeline_mode=RESIDENT)


def _in_cols(layer, first, width):
    assert first % width == 0
    return pl.BlockSpec((None, D_MODEL, width), lambda *_: (layer, 0, first // width), pipeline_mode=RESIDENT)


def _seq_spec(rows, cols):
    return pl.BlockSpec((None, rows, cols), lambda b, c: (b, c, 0))


def _s5_prep_kernel(lr_ref, li_ref, ls_ref, bre_ref, bim_ref, cre_ref, cim_ref,
                    abre_ref, abim_ref, wb_ref, cc_ref):
    gh = S5_GROUPS // 2
    rows_gp, cols_gn = gh * S5_GROUP, gh * S5_STATE
    own_group = (lax.broadcasted_iota(jnp.int32, (rows_gp, cols_gn), 0) // S5_GROUP
                 == lax.broadcasted_iota(jnp.int32, (rows_gp, cols_gn), 1) // S5_STATE)

    def block_diag(per_p):
        tiled = jnp.broadcast_to(per_p[None], (gh,) + per_p.shape).reshape(rows_gp, cols_gn)
        return jnp.where(own_group, tiled, 0.0)

    lr, li = lr_ref[...], li_ref[...]
    step = jnp.exp(ls_ref[...])
    mag = jnp.exp(lr * step)
    ang = li * step
    ab_re, ab_im = mag * jnp.cos(ang), mag * jnp.sin(ang)
    abre_ref[...] = ab_re
    abim_ref[...] = ab_im
    den = lr * lr + li * li
    nr = ab_re - 1.0
    coef_re = (nr * lr + ab_im * li) / den
    coef_im = (ab_im * lr - nr * li) / den
    for layer in range(lr.shape[0]):
        for hf in range(2):
            cols = slice(hf * cols_gn, (hf + 1) * cols_gn)
            cr, ci = coef_re[layer:layer + 1, cols], coef_im[layer:layer + 1, cols]
            br, bi = bre_ref[layer, :, cols], bim_ref[layer, :, cols]
            wb_ref[layer, hf, :, 0:cols_gn] = block_diag(cr * br - ci * bi).astype(BF16)
            wb_ref[layer, hf, :, cols_gn:2 * cols_gn] = block_diag(cr * bi + ci * br).astype(BF16)
            cc_ref[layer, hf, 0:cols_gn, :] = block_diag(cre_ref[layer, :, cols]).T.astype(BF16)
            cc_ref[layer, hf, cols_gn:2 * cols_gn, :] = (-block_diag(cim_ref[layer, :, cols])).T.astype(BF16)


def _s5_prep(lam_re, lam_im, b_re, b_im, c_re, c_im, log_step):
    nl = lam_re.shape[0]
    g, n, p = S5_GROUPS, S5_STATE, S5_GROUP
    ls = jnp.repeat(log_step, n, axis=-1)
    row = lambda v: v.reshape(nl, g * n)
    b_chan = lambda v: v.transpose(0, 3, 1, 2).reshape(nl, p, g * n)
    c_chan = lambda v: v.transpose(0, 2, 1, 3).reshape(nl, p, g * n)
    gh = g // 2
    return pl.pallas_call(
        _s5_prep_kernel,
        out_shape=(jax.ShapeDtypeStruct((nl, g * n), F32), jax.ShapeDtypeStruct((nl, g * n), F32),
                   jax.ShapeDtypeStruct((nl, 2, gh * p, 2 * gh * n), BF16),
                   jax.ShapeDtypeStruct((nl, 2, 2 * gh * n, gh * p), BF16)),
        name="s5_prep",
    )(row(lam_re), row(lam_im), row(ls), b_chan(b_re), b_chan(b_im), c_chan(c_re), c_chan(c_im))


S5_HALF = S5_PAIRS
S5_PERM = MXU_DIM


def _perm_time_major(nb):
    tt = S5_PERM // nb
    i = lax.broadcasted_iota(jnp.int32, (S5_PERM, S5_PERM), 0)
    j = lax.broadcasted_iota(jnp.int32, (S5_PERM, S5_PERM), 1)
    return jnp.where(j == (i % nb) * tt + i // nb, 1.0, 0.0).astype(BF16)


def _s5_kernel(x_ref, nw_ref, wug_ref, wb_ref, cc_ref, d_ref, wglu_ref, abre_ref, abim_ref, o_ref, h_ref,
               hp_ref, ug_ref, st_ref, carry_ref, *, steps, nb, layer):
    @pl.when(pl.program_id(0) == 0)
    def _():
        carry_ref[...] = jnp.zeros(carry_ref.shape, F32)

    tt = S5_PERM // nb
    perm = _perm_time_major(nb)
    for k in range(steps // tt):
        hk = _rms(x_ref[:, k * tt:(k + 1) * tt, :], nw_ref[layer:layer + 1, :]).astype(BF16)
        h_ref[:, k * tt:(k + 1) * tt, :] = hk
        blk = hk.reshape(S5_PERM, D_MODEL)
        hp_ref[k * S5_PERM:(k + 1) * S5_PERM, :] = _dot(perm, blk).astype(BF16)

    ug_ref[:, 0:WIDTH] = _dot(hp_ref[...], wug_ref[:, 0:WIDTH])
    ub = ug_ref[:, 0:WIDTH].astype(BF16)
    hw = WIDTH // 2
    for hf in range(2):
        st_ref[:, hf * S5_HALF:(hf + 1) * S5_HALF] = _dot(ub[:, hf * hw:(hf + 1) * hw], wb_ref[hf])
    ug_ref[:, WIDTH:2 * WIDTH] = _dot(hp_ref[...], wug_ref[:, WIDTH:2 * WIDTH])

    pairs_half = S5_HALF // 2
    for hf in range(2):
        cre = hf * S5_HALF
        cim = cre + pairs_half
        tc = hf * pairs_half
        ar = jnp.broadcast_to(abre_ref[layer:layer + 1, tc:tc + pairs_half], (nb, pairs_half))
        ai = jnp.broadcast_to(abim_ref[layer:layer + 1, tc:tc + pairs_half], (nb, pairs_half))

        def step(t, carry, cre=cre, cim=cim, ar=ar, ai=ai):
            sr, si = carry
            r0 = pl.multiple_of(t * nb, nb)
            nr = (ar * sr - ai * si) + st_ref[pl.ds(r0, nb), cre:cre + pairs_half]
            ni = (ar * si + ai * sr) + st_ref[pl.ds(r0, nb), cim:cim + pairs_half]
            st_ref[pl.ds(r0, nb), cre:cre + pairs_half] = nr
            st_ref[pl.ds(r0, nb), cim:cim + pairs_half] = ni
            return nr, ni

        sr, si = lax.fori_loop(
            0, steps, step,
            (carry_ref[:, cre:cre + pairs_half], carry_ref[:, cim:cim + pairs_half]), unroll=True)
        carry_ref[:, cre:cre + pairs_half] = sr
        carry_ref[:, cim:cim + pairs_half] = si

    ys = []
    for hf in range(2):
        ys.append(_dot(st_ref[:, hf * S5_HALF:(hf + 1) * S5_HALF].astype(BF16), cc_ref[hf]))
    y = jnp.concatenate(ys, axis=-1) + d_ref[layer:layer + 1, :] * ug_ref[:, 0:WIDTH]
    y = _gelu(y)
    y = y * _sigmoid(_dot(y.astype(BF16), wglu_ref[...]))
    y = (y * _silu(ug_ref[:, WIDTH:2 * WIDTH])).astype(BF16)
    for k in range(steps // tt):
        back = _dot_tn(perm, y[k * S5_PERM:(k + 1) * S5_PERM, :])
        o_ref[:, k * tt:(k + 1) * tt, :] = back.reshape(nb, tt, WIDTH).astype(o_ref.dtype)


def _s5_branch(x, norm_w, win, wb, cc, d, wglu, ab_re, ab_im, layer):
    b, l, dm = x.shape
    assert S5_PERM % b == 0 and b % SUBLANES == 0
    steps = min(S5_STEPS, l)
    rows = steps * b
    tile = lambda cols: pl.BlockSpec((b, steps, cols), lambda c: (0, c, 0))
    return pl.pallas_call(
        functools.partial(_s5_kernel, steps=steps, nb=b, layer=layer),
        grid=(l // steps,),
        in_specs=[tile(dm), _full(norm_w.shape), _in_cols(layer, IN_OFFS[0], 2 * WIDTH),
                  _of_layer(wb, layer), _of_layer(cc, layer), _full(d.shape), _of_layer(wglu, layer),
                  _full(ab_re.shape), _full(ab_im.shape)],
        out_specs=(tile(WIDTH), tile(dm)),
        out_shape=(jax.ShapeDtypeStruct((b, l, WIDTH), BF16), jax.ShapeDtypeStruct((b, l, dm), BF16)),
        scratch_shapes=[pltpu.VMEM((rows, dm), BF16),
                        pltpu.VMEM((rows, 2 * WIDTH), F32),
                        pltpu.VMEM((rows, 2 * S5_HALF), F32),
                        pltpu.VMEM((b, 2 * S5_HALF), F32)],
        compiler_params=_params("arbitrary"),
        name="s5_branch",
    )(x, norm_w, win, wb, cc, d, wglu, ab_re, ab_im)


def _sgu_kernel(h_ref, wu_ref, wv_ref, wg_ref, lnw_ref, lnb_ref, ws_ref, bias_ref, o_ref, *, rows, layer):
    t = SGU_CHUNK
    causal = lax.broadcasted_iota(jnp.int32, (t, t), 0) >= lax.broadcasted_iota(jnp.int32, (t, t), 1)
    first_head = lax.broadcasted_iota(jnp.int32, (t, LANES), 1) < SGU_HEAD_DIM
    wm = [jnp.where(causal, ws_ref[hd], 0.0).astype(BF16) for hd in range(SGU_HEADS)]
    zero = jnp.zeros((), BF16)
    part = rows // SPLIT
    for sp in range(SPLIT):
        h = h_ref[sp * part:(sp + 1) * part, :]
        v = _gelu(_dot(h, wv_ref[...]))
        mu = jnp.mean(v, axis=-1, keepdims=True)
        vc = v - mu
        var = jnp.mean(vc * vc, axis=-1, keepdims=True)
        vn = (vc * lax.rsqrt(var + EPS) * lnw_ref[layer:layer + 1, :] + lnb_ref[layer:layer + 1, :]).astype(BF16)
        ug = _gelu(_dot(h, wu_ref[...])) * _silu(_dot(h, wg_ref[...]))
        for c in range(part // t):
            r = slice(c * t, (c + 1) * t)
            outs = []
            for j in range(WIDTH // LANES):
                blk = vn[r, j * LANES:(j + 1) * LANES]
                outs.append(_dot(wm[2 * j], jnp.where(first_head, blk, zero))
                            + _dot(wm[2 * j + 1], jnp.where(first_head, zero, blk)))
            s = jnp.concatenate(outs, axis=-1) + bias_ref[...]
            o_ref[sp * part + c * t:sp * part + (c + 1) * t, :] = (ug[r] * s).astype(o_ref.dtype)


def _sgu_branch(h, win, ln_w, ln_b, w_s, bias, layer):
    b, l, dm = h.shape
    rows = min(SGU_ROWS, l)
    return pl.pallas_call(
        functools.partial(_sgu_kernel, rows=rows, layer=layer),
        grid=(b, l // rows),
        in_specs=[_seq_spec(rows, dm)] + [_in_cols(layer, IN_OFFS[k], WIDTH) for k in (2, 3, 4)]
                 + [_full(ln_w.shape), _full(ln_b.shape), _of_layer(w_s, layer), _of_layer(bias, layer)],
        out_specs=_seq_spec(rows, WIDTH),
        out_shape=jax.ShapeDtypeStruct((b, l, WIDTH), BF16),
        compiler_params=_params("parallel", "parallel"),
        name="sgu_branch",
    )(h, win, win, win, ln_w, ln_b, w_s, bias)


SSD_QUANT = 3
SSD_PARTS = 3
SSD_LANE_GROUPS = SSD_QUANT * SSD_PARTS


def _ssd_expand_matrix():
    e = np.zeros((LANES, SSD_QUANT * WIDTH), np.float32)
    for grp in range(SSD_LANE_GROUPS):
        for hd in range(M2_HEADS):
            c0 = (grp // SSD_PARTS) * WIDTH + hd * M2_HEAD_DIM
            e[grp * M2_HEADS + hd, c0:c0 + M2_HEAD_DIM] = 1.0
    return jnp.asarray(e, BF16)


def _cumsum_rows(x):
    q = x.shape[0]
    row = lax.broadcasted_iota(jnp.int32, x.shape, 0)
    dist = 1
    while dist < q:
        x = x + jnp.where(row >= dist, pltpu.roll(x, dist, 0), 0.0)
        dist *= 2
    return x


def _ssd_sc_kernel(h_ref, wz_ref, wxbc_ref, wdt_ref, cw_ref, cb_ref, dtb_ref, al_ref, e_ref, d_ref, nw_ref,
                   wsc_ref, scw_ref, o_ref, osc_ref, prev_ref, st_ref, scprev_ref, *, rows, layer):
    @pl.when(pl.program_id(1) == 0)
    def _():
        prev_ref[...] = jnp.zeros(prev_ref.shape, F32)
        st_ref[...] = jnp.zeros(st_ref.shape, F32)
        scprev_ref[...] = jnp.zeros(scprev_ref.shape, F32)

    sc_proj = lambda j: _dot(h_ref[...], wsc_ref[:, j * WIDTH:(j + 1) * WIDTH])

    rpart = rows // SPLIT
    hs = [h_ref[sp * rpart:(sp + 1) * rpart, :] for sp in range(SPLIT)]
    raws = [_dot(hh, wxbc_ref[...]) for hh in hs]
    zs = [_dot(hh, wz_ref[...]) for hh in hs]
    dtrs = [_dot(hh, wdt_ref[...]) for hh in hs]
    sc_v = sc_proj(1) * sc_proj(2)
    prev = prev_ref[...]
    xbcs, dts, das = [], [], []
    neg_a = -jnp.exp(al_ref[layer:layer + 1, :])
    for sp in range(SPLIT):
        raw = raws[sp]
        cat = jnp.concatenate([prev, raw], axis=0)
        acc = cb_ref[layer:layer + 1, :] + cw_ref[M2_CONV - 1:M2_CONV, :] * raw
        for j in range(1, M2_CONV):
            k = M2_CONV - 1 - j
            acc = acc + cw_ref[k:k + 1, :] * _shift_rows(cat, j)
        prev = raw[rpart - SUBLANES:rpart, :]
        xbcs.append(_silu(acc))
        dts.append(_softplus(dtrs[sp] + dtb_ref[layer:layer + 1, :]))
        das.append(dts[sp] * neg_a)
    prev_ref[...] = prev
    grp = lax.broadcasted_iota(jnp.int32, (1, LANES), 1) // M2_HEADS
    quant = grp // SSD_PARTS
    part = grp % SSD_PARTS

    q = M2_CHUNK
    hd_w = M2_HEAD_DIM
    causal = lax.broadcasted_iota(jnp.int32, (q, q), 0) >= lax.broadcasted_iota(jnp.int32, (q, q), 1)
    first_head = lax.broadcasted_iota(jnp.int32, (q, LANES), 1) < hd_w
    zero = jnp.zeros((), BF16)
    gw = WIDTH // M2_GROUPS

    def chunk(s):
        sp = (s * q) // rpart
        xbc, z, dt, da = xbcs[sp], zs[sp], dts[sp], das[sp]
        x = xbc[:, 0:WIDTH]
        r = slice(s * q - sp * rpart, (s + 1) * q - sp * rpart)
        cs = _cumsum_rows(da[r])
        cs_row = cs.T[0:M2_HEADS, :]
        cs_last = cs[q - 1:q, :]
        val = jnp.where(quant == 0, dt[r], jnp.where(quant == 1, dt[r] * jnp.exp(cs_last - cs), jnp.exp(cs)))
        p0 = val.astype(BF16)
        r1 = val - p0.astype(F32)
        p1 = r1.astype(BF16)
        p2 = (r1 - p1.astype(F32)).astype(BF16)
        spread = _dot(jnp.where(part == 0, p0, jnp.where(part == 1, p1, p2)), e_ref[...])
        dt_e = spread[:, 0:WIDTH]
        dtdec_e = spread[:, WIDTH:2 * WIDTH]
        exp_cs = spread[:, 2 * WIDTH:3 * WIDTH]
        exp_last = exp_cs[q - 1:q, :]
        x_c = x[r]
        xdt_b = (x_c * dt_e).astype(BF16)
        xdec = (x_c * dtdec_e).astype(BF16)
        ys = []
        for g in range(M2_GROUPS):
            bo = WIDTH + g * M2_STATE
            co = WIDTH + M2_GROUPS * M2_STATE + g * M2_STATE
            bg = xbc[r, bo:bo + M2_STATE].astype(BF16)
            cg = xbc[r, co:co + M2_STATE].astype(BF16)
            scores = _dot_nt(cg, bg)
            state = st_ref[g]
            y_g = _dot(cg, state.astype(BF16)) * exp_cs[:, g * gw:(g + 1) * gw]
            diag = []
            for jp in range(gw // LANES):
                blk = xdt_b[:, g * gw + jp * LANES:g * gw + (jp + 1) * LANES]
                acc2 = None
                for qq in range(2):
                    hd = g * (M2_HEADS // M2_GROUPS) + jp * 2 + qq
                    seg = cs[:, hd:hd + 1] - cs_row[hd:hd + 1, :]
                    decay = jnp.exp(jnp.where(causal, seg, -jnp.inf))
                    m = (scores * decay).astype(BF16)
                    half = jnp.where(first_head, blk, zero) if qq == 0 else jnp.where(first_head, zero, blk)
                    term = _dot(m, half)
                    acc2 = term if acc2 is None else acc2 + term
                diag.append(acc2)
            ys.append(y_g + jnp.concatenate(diag, axis=-1))
            st_ref[g] = state * exp_last[:, g * gw:(g + 1) * gw] + _dot_tn(bg, xdec[:, g * gw:(g + 1) * gw])
        y = jnp.concatenate(ys, axis=-1) + d_ref[layer:layer + 1, :] * x_c
        y = y * _silu(z[r])
        o_ref[s * q:(s + 1) * q, :] = _rms(y, nw_ref[layer:layer + 1, :]).astype(o_ref.dtype)

    def sc_tail(sc_b, sc_gate):
        cat = jnp.concatenate([scprev_ref[...], sc_v], axis=0)
        acc = scw_ref[SC_CONV - 1:SC_CONV, :] * sc_v
        for j in range(1, SC_CONV):
            k = SC_CONV - 1 - j
            acc = acc + scw_ref[k:k + 1, :] * _shift_rows(cat, j)
        scprev_ref[...] = sc_v[rows - SUBLANES:rows, :]
        osc_ref[...] = (sc_b * acc * _silu(sc_gate)).astype(osc_ref.dtype)

    n_chunks = rows // q
    sc_b = sc_gate = None
    for s in range(n_chunks):
        chunk(s)
        if s == 0:
            sc_b = sc_proj(0)
        if s == min(1, n_chunks - 1):
            sc_gate = sc_proj(3)
        if s == min(2, n_chunks - 1):
            sc_tail(sc_b, sc_gate)


def _ssd_sc_branch(h, win, win_tail, wdt_groups, conv_w, conv_b, dtb_groups, al_groups, expand, d_lanes, norm_w,
                   sc_conv_w, layer):
    b, l, dm = h.shape
    rows = min(SSD_ROWS, l)
    out = jax.ShapeDtypeStruct((b, l, WIDTH), BF16)
    return pl.pallas_call(
        functools.partial(_ssd_sc_kernel, rows=rows, layer=layer),
        grid=(b, l // rows),
        in_specs=[_seq_spec(rows, dm), _in_cols(layer, IN_OFFS[5], WIDTH), _in_cols(layer, IN_OFFS[6], M2_CONV_CH),
                  _of_layer(wdt_groups, layer), _of_layer(conv_w, layer), _full(conv_b.shape),
                  _full(dtb_groups.shape), _full(al_groups.shape), _full(expand.shape), _full(d_lanes.shape),
                  _full(norm_w.shape), _in_cols(layer, 0, TAIL_BLOCK), _of_layer(sc_conv_w, layer)],
        out_specs=(_seq_spec(rows, WIDTH), _seq_spec(rows, WIDTH)),
        out_shape=(out, out),
        scratch_shapes=[pltpu.VMEM((SUBLANES, M2_CONV_CH), F32),
                        pltpu.VMEM((M2_GROUPS, M2_STATE, WIDTH // M2_GROUPS), F32),
                        pltpu.VMEM((SUBLANES, WIDTH), F32)],
        compiler_params=_params("parallel", "arbitrary"),
        name="ssd_sc_branch",
    )(h, win, win, wdt_groups, conv_w, conv_b, dtb_groups, al_groups, expand, d_lanes, norm_w, win_tail,
      sc_conv_w)


def _merge_kernel(x_ref, h_ref, ya_ref, yb_ref, yc_ref, yd_ref, wma_ref, wmb_ref, mb_ref, wbr_ref, wo_ref,
                  nw_ref, o_ref, *, final, rows):
    d = D_MODEL
    per_block = TAIL_BLOCK // d
    part = rows // SPLIT
    for sp in range(SPLIT):
        r = slice(sp * part, (sp + 1) * part)
        h = h_ref[r, :]
        merged = None
        for k, y_ref in enumerate((ya_ref, yb_ref, yc_ref, yd_ref)):
            wm_ref, kk = (wma_ref, wmb_ref)[k // per_block], k % per_block
            gate = _sigmoid(_dot(h, wm_ref[:, kk * d:(kk + 1) * d]) + mb_ref[k:k + 1, :])
            term = gate * _dot(y_ref[r, :], wbr_ref[k])
            merged = term if merged is None else merged + term
        xn = x_ref[r, :] + _dot(merged.astype(BF16), wo_ref[...])
        o_ref[r, :] = _rms(xn, nw_ref[...]) if final else xn


def _merge(x2d, h2d, ys, win_tail, mb, wbr, wo, final_norm_w, layer, final):
    t, d = x2d.shape
    assert MERGE_BLOCKS == 2
    tile = lambda cols: pl.BlockSpec((MERGE_ROWS, cols), lambda i: (i, 0))
    return pl.pallas_call(
        functools.partial(_merge_kernel, final=final, rows=MERGE_ROWS),
        grid=(t // MERGE_ROWS,),
        in_specs=[tile(d), tile(d)] + [tile(WIDTH)] * N_BRANCH
                 + [_in_cols(layer, TAIL_BLOCK, TAIL_BLOCK), _in_cols(layer, 2 * TAIL_BLOCK, TAIL_BLOCK),
                    _of_layer(mb, layer), _of_layer(wbr, layer), _of_layer(wo, layer), _full((1, d))],
        out_specs=tile(d),
        out_shape=jax.ShapeDtypeStruct((t, d), F32),
        compiler_params=_params("parallel"),
        name="merge",
    )(x2d, h2d, *ys, win_tail, win_tail, mb, wbr, wo, final_norm_w.reshape(1, d))


def kernel(x, norm_w, w_in, s5_lambda_re, s5_lambda_im, s5_b_re, s5_b_im, s5_c_re, s5_c_im, s5_d,
           s5_log_step, s5_w_glu, sgu_ln_w, sgu_ln_b, sgu_w, sgu_b, m2_conv_w, m2_conv_b, m2_dt_bias,
           m2_a_log, m2_d, m2_norm_w, sc_conv_w, merge_b, w_branch, w_out, final_norm_w):
    b, l, d = x.shape
    t = b * l
    nl = w_in.shape[0]

    win = w_in[:, :, :TAIL_FIRST].astype(BF16)
    win_tail = w_in[:, :, TAIL_FIRST:].astype(BF16)
    used = SSD_LANE_GROUPS * M2_HEADS
    lane_groups = lambda v: jnp.pad(jnp.tile(v, (1,) * (v.ndim - 1) + (SSD_LANE_GROUPS,)),
                                    ((0, 0),) * (v.ndim - 1) + ((0, LANES - used),))
    wdt_groups = lane_groups(win[:, :, IN_OFFS[7]:IN_OFFS[8]])
    dtb_groups = lane_groups(m2_dt_bias)
    al_groups = lane_groups(m2_a_log)
    d_lanes = jnp.repeat(m2_d, M2_HEAD_DIM, axis=-1)
    expand = _ssd_expand_matrix()
    sgu_bias = jnp.repeat(sgu_b.transpose(0, 2, 1), SGU_HEAD_DIM, axis=-1)
    ab_re, ab_im, s5_wb, s5_cc = _s5_prep(s5_lambda_re, s5_lambda_im, s5_b_re, s5_b_im, s5_c_re, s5_c_im,
                                          s5_log_step)
    s5_dd = s5_d.reshape(nl, WIDTH)
    wglu = s5_w_glu.astype(BF16)
    wbr = w_branch.astype(BF16)
    wo = w_out.astype(BF16)

    x2d = x.reshape(t, d)
    for i in range(DEPTH):
        ya, h = _s5_branch(x2d.reshape(b, l, d), norm_w, win, s5_wb, s5_cc, s5_dd, wglu, ab_re, ab_im, i)
        yb = _sgu_branch(h, win, sgu_ln_w, sgu_ln_b, sgu_w, sgu_bias, i)
        yc, yd = _ssd_sc_branch(h, win, win_tail, wdt_groups, m2_conv_w, m2_conv_b, dtb_groups, al_groups, expand,
                                d_lanes, m2_norm_w, sc_conv_w, i)
        x2d = _merge(x2d, h.reshape(t, d), [y.reshape(t, WIDTH) for y in (ya, yb, yc, yd)], win_tail, merge_b,
                     wbr, wo, final_norm_w, i, i == DEPTH - 1)
    return x2d.reshape(b, l, d)
```

```python
import functools

import jax
import jax.numpy as jnp
import numpy as np
from jax import lax
from jax.experimental import pallas as pl
from jax.experimental.pallas import tpu as pltpu

F32 = jnp.float32
BF16 = jnp.bfloat16

D_MODEL = 1024
DEPTH = 2
WIDTH = 512
N_BRANCH = 4
EPS = 1e-6
S5_GROUP = 16
S5_GROUPS = WIDTH // S5_GROUP
S5_STATE = 64
S5_PAIRS = S5_GROUPS * S5_STATE
SGU_CHUNK = 128
SGU_HEADS = 8
SGU_HEAD_DIM = WIDTH // SGU_HEADS
M2_HEAD_DIM = 64
M2_HEADS = WIDTH // M2_HEAD_DIM
M2_GROUPS = 2
M2_STATE = 128
M2_CONV = 4
M2_CHUNK = 128
M2_CONV_CH = WIDTH + 2 * M2_GROUPS * M2_STATE
SC_CONV = 3

IN_SIZES = (WIDTH, WIDTH, WIDTH, WIDTH, WIDTH, WIDTH, M2_CONV_CH, M2_HEADS,
            WIDTH, WIDTH, WIDTH, WIDTH, N_BRANCH * D_MODEL)
IN_OFFS = tuple(int(v) for v in np.cumsum((0,) + IN_SIZES))

SUBLANES = 8
LANES = 128
MXU_DIM = 256
VMEM_LIMIT_BYTES = 56 * 1024 * 1024

S5_STEPS = 128
SGU_ROWS = 1024
SSD_ROWS = 1024
MERGE_ROWS = 1024
SPLIT = 2


def _dot(a, b):
    return jnp.dot(a, b, preferred_element_type=F32)


def _dot_nt(a, b):
    return lax.dot_general(a, b, (((1,), (1,)), ((), ())), preferred_element_type=F32)


def _dot_tn(a, b):
    return lax.dot_general(a, b, (((0,), (0,)), ((), ())), preferred_element_type=F32)


def _sigmoid(x):
    return 0.5 * (jnp.tanh(0.5 * x) + 1.0)


def _silu(x):
    hx = 0.5 * x
    return hx * jnp.tanh(hx) + hx


def _gelu(x):
    c = np.float32(np.sqrt(2.0 / np.pi))
    return x * (0.5 * (1.0 + jnp.tanh(c * (x + 0.044715 * (x * x * x)))))


def _softplus(x):
    return jnp.maximum(x, 0.0) + jnp.log1p(jnp.exp(-jnp.abs(x)))


def _rms(x, w):
    return x * lax.rsqrt(jnp.mean(x * x, axis=-1, keepdims=True) + EPS) * w


def _shift_rows(cat, j):
    n8, c = cat.shape
    rot = pltpu.roll(cat.reshape(n8 // SUBLANES, SUBLANES, c), j, 1)
    sub = lax.broadcasted_iota(jnp.int32, (1, SUBLANES, c), 1)
    out = jnp.where(sub >= j, rot[1:], rot[:-1])
    return out.reshape(n8 - SUBLANES, c)


def _params(*semantics):
    return pltpu.CompilerParams(dimension_semantics=semantics, vmem_limit_bytes=VMEM_LIMIT_BYTES)


RESIDENT = pl.Buffered(1)


def _full(shape):
    zeros = (0,) * len(shape)
    return pl.BlockSpec(tuple(shape), lambda *_: zeros, pipeline_mode=RESIDENT)


def _of_layer(arr, layer):
    zeros = (0,) * (arr.ndim - 1)
    return pl.BlockSpec((None,) + tuple(arr.shape[1:]), lambda *_: (layer,) + zeros, pipeline_mode=RESIDENT)


def _in_cols(layer, first, width):
    assert first % width == 0
    return pl.BlockSpec((None, D_MODEL, width), lambda *_: (layer, 0, first // width), pipeline_mode=RESIDENT)


def _seq_spec(rows, cols):
    return pl.BlockSpec((None, rows, cols), lambda b, c: (b, c, 0))


WPREP_COLS = 512


def _wprep_kernel(wt_ref, o_ref):
    o_ref[...] = wt_ref[0].T.astype(o_ref.dtype)


def _in_proj_cols(w_in_t, first, ncols):
    nl, _, d = w_in_t.shape
    assert ncols % WPREP_COLS == 0 and first % SUBLANES == 0
    return pl.pallas_call(
        _wprep_kernel,
        grid=(nl, ncols // WPREP_COLS),
        in_specs=[pl.BlockSpec((pl.Element(1), pl.Element(WPREP_COLS), pl.Element(d)),
                               lambda l, j: (l, pl.multiple_of(first + j * WPREP_COLS, SUBLANES), 0))],
        out_specs=pl.BlockSpec((None, d, WPREP_COLS), lambda l, j: (l, 0, j)),
        out_shape=jax.ShapeDtypeStruct((nl, d, ncols), BF16),
        compiler_params=_params("parallel", "parallel"),
        name="w_in_cols",
    )(w_in_t)


def _s5_prep_kernel(lr_ref, li_ref, ls_ref, bre_ref, bim_ref, cre_ref, cim_ref,
                    abre_ref, abim_ref, wb_ref, cc_ref):
    gh = S5_GROUPS // 2
    rows_gp, cols_gn = gh * S5_GROUP, gh * S5_STATE
    own_group = (lax.broadcasted_iota(jnp.int32, (rows_gp, cols_gn), 0) // S5_GROUP
                 == lax.broadcasted_iota(jnp.int32, (rows_gp, cols_gn), 1) // S5_STATE)

    def block_diag(per_p):
        tiled = jnp.broadcast_to(per_p[None], (gh,) + per_p.shape).reshape(rows_gp, cols_gn)
        return jnp.where(own_group, tiled, 0.0)

    lr, li = lr_ref[...], li_ref[...]
    step = jnp.exp(ls_ref[...])
    mag = jnp.exp(lr * step)
    ang = li * step
    ab_re, ab_im = mag * jnp.cos(ang), mag * jnp.sin(ang)
    abre_ref[...] = ab_re
    abim_ref[...] = ab_im
    den = lr * lr + li * li
    nr = ab_re - 1.0
    coef_re = (nr * lr + ab_im * li) / den
    coef_im = (ab_im * lr - nr * li) / den
    for layer in range(lr.shape[0]):
        for hf in range(2):
            cols = slice(hf * cols_gn, (hf + 1) * cols_gn)
            cr, ci = coef_re[layer:layer + 1, cols], coef_im[layer:layer + 1, cols]
            br, bi = bre_ref[layer, :, cols], bim_ref[layer, :, cols]
            wb_ref[layer, hf, :, 0:cols_gn] = block_diag(cr * br - ci * bi).astype(BF16)
            wb_ref[layer, hf, :, cols_gn:2 * cols_gn] = block_diag(cr * bi + ci * br).astype(BF16)
            cc_ref[layer, hf, 0:cols_gn, :] = block_diag(cre_ref[layer, :, cols]).T.astype(BF16)
            cc_ref[layer, hf, cols_gn:2 * cols_gn, :] = (-block_diag(cim_ref[layer, :, cols])).T.astype(BF16)


def _s5_prep(lam_re, lam_im, b_re, b_im, c_re, c_im, log_step):
    nl = lam_re.shape[0]
    g, n, p = S5_GROUPS, S5_STATE, S5_GROUP
    ls = jnp.repeat(log_step, n, axis=-1)
    row = lambda v: v.reshape(nl, g * n)
    b_chan = lambda v: v.transpose(0, 3, 1, 2).reshape(nl, p, g * n)
    c_chan = lambda v: v.transpose(0, 2, 1, 3).reshape(nl, p, g * n)
    gh = g // 2
    return pl.pallas_call(
        _s5_prep_kernel,
        out_shape=(jax.ShapeDtypeStruct((nl, g * n), F32), jax.ShapeDtypeStruct((nl, g * n), F32),
                   jax.ShapeDtypeStruct((nl, 2, gh * p, 2 * gh * n), BF16),
                   jax.ShapeDtypeStruct((nl, 2, 2 * gh * n, gh * p), BF16)),
        name="s5_prep",
    )(row(lam_re), row(lam_im), row(ls), b_chan(b_re), b_chan(b_im), c_chan(c_re), c_chan(c_im))


S5_HALF = S5_PAIRS
S5_PERM = MXU_DIM


def _perm_time_major(nb):
    tt = S5_PERM // nb
    i = lax.broadcasted_iota(jnp.int32, (S5_PERM, S5_PERM), 0)
    j = lax.broadcasted_iota(jnp.int32, (S5_PERM, S5_PERM), 1)
    return jnp.where(j == (i % nb) * tt + i // nb, 1.0, 0.0).astype(BF16)


def _s5_kernel(x_ref, nw_ref, wug_ref, wb_ref, cc_ref, d_ref, wglu_ref, abre_ref, abim_ref, o_ref, h_ref,
               hp_ref, ug_ref, st_ref, carry_ref, *, steps, nb, layer):
    @pl.when(pl.program_id(0) == 0)
    def _():
        carry_ref[...] = jnp.zeros(carry_ref.shape, F32)

    tt = S5_PERM // nb
    perm = _perm_time_major(nb)
    for k in range(steps // tt):
        hk = _rms(x_ref[:, k * tt:(k + 1) * tt, :], nw_ref[layer:layer + 1, :]).astype(BF16)
        h_ref[:, k * tt:(k + 1) * tt, :] = hk
        blk = hk.reshape(S5_PERM, D_MODEL)
        hp_ref[k * S5_PERM:(k + 1) * S5_PERM, :] = _dot(perm, blk).astype(BF16)

    ug_ref[:, 0:WIDTH] = _dot(hp_ref[...], wug_ref[:, 0:WIDTH])
    ub = ug_ref[:, 0:WIDTH].astype(BF16)
    hw = WIDTH // 2
    for hf in range(2):
        st_ref[:, hf * S5_HALF:(hf + 1) * S5_HALF] = _dot(ub[:, hf * hw:(hf + 1) * hw], wb_ref[hf])
    ug_ref[:, WIDTH:2 * WIDTH] = _dot(hp_ref[...], wug_ref[:, WIDTH:2 * WIDTH])

    pairs_half = S5_HALF // 2
    for hf in range(2):
        cre = hf * S5_HALF
        cim = cre + pairs_half
        tc = hf * pairs_half
        ar = jnp.broadcast_to(abre_ref[layer:layer + 1, tc:tc + pairs_half], (nb, pairs_half))
        ai = jnp.broadcast_to(abim_ref[layer:layer + 1, tc:tc + pairs_half], (nb, pairs_half))

        def step(t, carry, cre=cre, cim=cim, ar=ar, ai=ai):
            sr, si = carry
            r0 = pl.multiple_of(t * nb, nb)
            nr = (ar * sr - ai * si) + st_ref[pl.ds(r0, nb), cre:cre + pairs_half]
            ni = (ar * si + ai * sr) + st_ref[pl.ds(r0, nb), cim:cim + pairs_half]
            st_ref[pl.ds(r0, nb), cre:cre + pairs_half] = nr
            st_ref[pl.ds(r0, nb), cim:cim + pairs_half] = ni
            return nr, ni

        sr, si = lax.fori_loop(
            0, steps, step,
            (carry_ref[:, cre:cre + pairs_half], carry_ref[:, cim:cim + pairs_half]), unroll=True)
        carry_ref[:, cre:cre + pairs_half] = sr
        carry_ref[:, cim:cim + pairs_half] = si

    ys = []
    for hf in range(2):
        ys.append(_dot(st_ref[:, hf * S5_HALF:(hf + 1) * S5_HALF].astype(BF16), cc_ref[hf]))
    y = jnp.concatenate(ys, axis=-1) + d_ref[layer:layer + 1, :] * ug_ref[:, 0:WIDTH]
    y = _gelu(y)
    y = y * _sigmoid(_dot(y.astype(BF16), wglu_ref[...]))
    y = (y * _silu(ug_ref[:, WIDTH:2 * WIDTH])).astype(BF16)
    for k in range(steps // tt):
        back = _dot_tn(perm, y[k * S5_PERM:(k + 1) * S5_PERM, :])
        o_ref[:, k * tt:(k + 1) * tt, :] = back.reshape(nb, tt, WIDTH).astype(o_ref.dtype)


def _s5_branch(x, norm_w, win, wb, cc, d, wglu, ab_re, ab_im, layer):
    b, l, dm = x.shape
    assert S5_PERM % b == 0 and b % SUBLANES == 0
    steps = min(S5_STEPS, l)
    rows = steps * b
    tile = lambda cols: pl.BlockSpec((b, steps, cols), lambda c: (0, c, 0))
    return pl.pallas_call(
        functools.partial(_s5_kernel, steps=steps, nb=b, layer=layer),
        grid=(l // steps,),
        in_specs=[tile(dm), _full(norm_w.shape), _in_cols(layer, IN_OFFS[0], 2 * WIDTH),
                  _of_layer(wb, layer), _of_layer(cc, layer), _full(d.shape), _of_layer(wglu, layer),
                  _full(ab_re.shape), _full(ab_im.shape)],
        out_specs=(tile(WIDTH), tile(dm)),
        out_shape=(jax.ShapeDtypeStruct((b, l, WIDTH), BF16), jax.ShapeDtypeStruct((b, l, dm), BF16)),
        scratch_shapes=[pltpu.VMEM((rows, dm), BF16),
                        pltpu.VMEM((rows, 2 * WIDTH), F32),
                        pltpu.VMEM((rows, 2 * S5_HALF), F32),
                        pltpu.VMEM((b, 2 * S5_HALF), F32)],
        compiler_params=_params("arbitrary"),
        name="s5_branch",
    )(x, norm_w, win, wb, cc, d, wglu, ab_re, ab_im)


def _sgu_kernel(h_ref, wu_ref, wv_ref, wg_ref, lnw_ref, lnb_ref, ws_ref, bias_ref, o_ref, *, rows, layer):
    t = SGU_CHUNK
    causal = lax.broadcasted_iota(jnp.int32, (t, t), 0) >= lax.broadcasted_iota(jnp.int32, (t, t), 1)
    first_head = lax.broadcasted_iota(jnp.int32, (t, LANES), 1) < SGU_HEAD_DIM
    wm = [jnp.where(causal, ws_ref[hd], 0.0).astype(BF16) for hd in range(SGU_HEADS)]
    zero = jnp.zeros((), BF16)
    part = rows // SPLIT
    for sp in range(SPLIT):
        h = h_ref[sp * part:(sp + 1) * part, :]
        v = _gelu(_dot(h, wv_ref[...]))
        mu = jnp.mean(v, axis=-1, keepdims=True)
        vc = v - mu
        var = jnp.mean(vc * vc, axis=-1, keepdims=True)
        vn = (vc * lax.rsqrt(var + EPS) * lnw_ref[layer:layer + 1, :] + lnb_ref[layer:layer + 1, :]).astype(BF16)
        ug = _gelu(_dot(h, wu_ref[...])) * _silu(_dot(h, wg_ref[...]))
        for c in range(part // t):
            r = slice(c * t, (c + 1) * t)
            outs = []
            for j in range(WIDTH // LANES):
                blk = vn[r, j * LANES:(j + 1) * LANES]
                outs.append(_dot(wm[2 * j], jnp.where(first_head, blk, zero))
                            + _dot(wm[2 * j + 1], jnp.where(first_head, zero, blk)))
            s = jnp.concatenate(outs, axis=-1) + bias_ref[...]
            o_ref[sp * part + c * t:sp * part + (c + 1) * t, :] = (ug[r] * s).astype(o_ref.dtype)


def _sgu_branch(h, win, ln_w, ln_b, w_s, bias, layer):
    b, l, dm = h.shape
    rows = min(SGU_ROWS, l)
    return pl.pallas_call(
        functools.partial(_sgu_kernel, rows=rows, layer=layer),
        grid=(b, l // rows),
        in_specs=[_seq_spec(rows, dm)] + [_in_cols(layer, IN_OFFS[k], WIDTH) for k in (2, 3, 4)]
                 + [_full(ln_w.shape), _full(ln_b.shape), _of_layer(w_s, layer), _of_layer(bias, layer)],
        out_specs=_seq_spec(rows, WIDTH),
        out_shape=jax.ShapeDtypeStruct((b, l, WIDTH), BF16),
        compiler_params=_params("parallel", "parallel"),
        name="sgu_branch",
    )(h, win, win, win, ln_w, ln_b, w_s, bias)


SSD_QUANT = 3
SSD_PARTS = 3
SSD_LANE_GROUPS = SSD_QUANT * SSD_PARTS


def _ssd_expand_matrix():
    e = np.zeros((LANES, SSD_QUANT * WIDTH), np.float32)
    for grp in range(SSD_LANE_GROUPS):
        for hd in range(M2_HEADS):
            c0 = (grp // SSD_PARTS) * WIDTH + hd * M2_HEAD_DIM
            e[grp * M2_HEADS + hd, c0:c0 + M2_HEAD_DIM] = 1.0
    return jnp.asarray(e, BF16)


def _cumsum_rows(x):
    q = x.shape[0]
    row = lax.broadcasted_iota(jnp.int32, x.shape, 0)
    dist = 1
    while dist < q:
        x = x + jnp.where(row >= dist, pltpu.roll(x, dist, 0), 0.0)
        dist *= 2
    return x


def _ssd_sc_kernel(h_ref, wz_ref, wxbc_ref, wdt_ref, cw_ref, cb_ref, dtb_ref, al_ref, e_ref, d_ref, nw_ref,
                   wsc_ref, scw_ref, o_ref, osc_ref, prev_ref, st_ref, scprev_ref, *, rows, layer):
    @pl.when(pl.program_id(1) == 0)
    def _():
        prev_ref[...] = jnp.zeros(prev_ref.shape, F32)
        st_ref[...] = jnp.zeros(st_ref.shape, F32)
        scprev_ref[...] = jnp.zeros(scprev_ref.shape, F32)

    sc_proj = lambda j: _dot(h_ref[...], wsc_ref[:, j * WIDTH:(j + 1) * WIDTH])

    rpart = rows // SPLIT
    hs = [h_ref[sp * rpart:(sp + 1) * rpart, :] for sp in range(SPLIT)]
    raws = [_dot(hh, wxbc_ref[...]) for hh in hs]
    zs = [_dot(hh, wz_ref[...]) for hh in hs]
    dtrs = [_dot(hh, wdt_ref[...]) for hh in hs]
    sc_v = sc_proj(1) * sc_proj(2)
    prev = prev_ref[...]
    xbcs, dts, das = [], [], []
    neg_a = -jnp.exp(al_ref[layer:layer + 1, :])
    for sp in range(SPLIT):
        raw = raws[sp]
        cat = jnp.concatenate([prev, raw], axis=0)
        acc = cb_ref[layer:layer + 1, :] + cw_ref[M2_CONV - 1:M2_CONV, :] * raw
        for j in range(1, M2_CONV):
            k = M2_CONV - 1 - j
            acc = acc + cw_ref[k:k + 1, :] * _shift_rows(cat, j)
        prev = raw[rpart - SUBLANES:rpart, :]
        xbcs.append(_silu(acc))
        dts.append(_softplus(dtrs[sp] + dtb_ref[layer:layer + 1, :]))
        das.append(dts[sp] * neg_a)
    prev_ref[...] = prev
    grp = lax.broadcasted_iota(jnp.int32, (1, LANES), 1) // M2_HEADS
    quant = grp // SSD_PARTS
    part = grp % SSD_PARTS

    q = M2_CHUNK
    hd_w = M2_HEAD_DIM
    causal = lax.broadcasted_iota(jnp.int32, (q, q), 0) >= lax.broadcasted_iota(jnp.int32, (q, q), 1)
    first_head = lax.broadcasted_iota(jnp.int32, (q, LANES), 1) < hd_w
    zero = jnp.zeros((), BF16)
    gw = WIDTH // M2_GROUPS

    def chunk(s):
        sp = (s * q) // rpart
        xbc, z, dt, da = xbcs[sp], zs[sp], dts[sp], das[sp]
        x = xbc[:, 0:WIDTH]
        r = slice(s * q - sp * rpart, (s + 1) * q - sp * rpart)
        cs = _cumsum_rows(da[r])
        cs_row = cs.T[0:M2_HEADS, :]
        cs_last = cs[q - 1:q, :]
        val = jnp.where(quant == 0, dt[r], jnp.where(quant == 1, dt[r] * jnp.exp(cs_last - cs), jnp.exp(cs)))
        p0 = val.astype(BF16)
        r1 = val - p0.astype(F32)
        p1 = r1.astype(BF16)
        p2 = (r1 - p1.astype(F32)).astype(BF16)
        spread = _dot(jnp.where(part == 0, p0, jnp.where(part == 1, p1, p2)), e_ref[...])
        dt_e = spread[:, 0:WIDTH]
        dtdec_e = spread[:, WIDTH:2 * WIDTH]
        exp_cs = spread[:, 2 * WIDTH:3 * WIDTH]
        exp_last = exp_cs[q - 1:q, :]
        x_c = x[r]
        xdt_b = (x_c * dt_e).astype(BF16)
        xdec = (x_c * dtdec_e).astype(BF16)
        ys = []
        for g in range(M2_GROUPS):
            bo = WIDTH + g * M2_STATE
            co = WIDTH + M2_GROUPS * M2_STATE + g * M2_STATE
            bg = xbc[r, bo:bo + M2_STATE].astype(BF16)
            cg = xbc[r, co:co + M2_STATE].astype(BF16)
            scores = _dot_nt(cg, bg)
            state = st_ref[g]
            y_g = _dot(cg, state.astype(BF16)) * exp_cs[:, g * gw:(g + 1) * gw]
            diag = []
            for jp in range(gw // LANES):
                blk = xdt_b[:, g * gw + jp * LANES:g * gw + (jp + 1) * LANES]
                acc2 = None
                for qq in range(2):
                    hd = g * (M2_HEADS // M2_GROUPS) + jp * 2 + qq
                    seg = cs[:, hd:hd + 1] - cs_row[hd:hd + 1, :]
                    decay = jnp.exp(jnp.where(causal, seg, -jnp.inf))
                    m = (scores * decay).astype(BF16)
                    half = jnp.where(first_head, blk, zero) if qq == 0 else jnp.where(first_head, zero, blk)
                    term = _dot(m, half)
                    acc2 = term if acc2 is None else acc2 + term
                diag.append(acc2)
            ys.append(y_g + jnp.concatenate(diag, axis=-1))
            st_ref[g] = state * exp_last[:, g * gw:(g + 1) * gw] + _dot_tn(bg, xdec[:, g * gw:(g + 1) * gw])
        y = jnp.concatenate(ys, axis=-1) + d_ref[layer:layer + 1, :] * x_c
        y = y * _silu(z[r])
        o_ref[s * q:(s + 1) * q, :] = _rms(y, nw_ref[layer:layer + 1, :]).astype(o_ref.dtype)

    def sc_tail(sc_b, sc_gate):
        cat = jnp.concatenate([scprev_ref[...], sc_v], axis=0)
        acc = scw_ref[SC_CONV - 1:SC_CONV, :] * sc_v
        for j in range(1, SC_CONV):
            k = SC_CONV - 1 - j
            acc = acc + scw_ref[k:k + 1, :] * _shift_rows(cat, j)
        scprev_ref[...] = sc_v[rows - SUBLANES:rows, :]
        osc_ref[...] = (sc_b * acc * _silu(sc_gate)).astype(osc_ref.dtype)

    n_chunks = rows // q
    sc_b = sc_gate = None
    for s in range(n_chunks):
        chunk(s)
        if s == 0:
            sc_b = sc_proj(0)
        if s == min(1, n_chunks - 1):
            sc_gate = sc_proj(3)
        if s == min(2, n_chunks - 1):
            sc_tail(sc_b, sc_gate)


def _ssd_sc_branch(h, win, wdt_groups, conv_w, conv_b, dtb_groups, al_groups, expand, d_lanes, norm_w,
                   w_sc, sc_conv_w, layer):
    b, l, dm = h.shape
    rows = min(SSD_ROWS, l)
    out = jax.ShapeDtypeStruct((b, l, WIDTH), BF16)
    return pl.pallas_call(
        functools.partial(_ssd_sc_kernel, rows=rows, layer=layer),
        grid=(b, l // rows),
        in_specs=[_seq_spec(rows, dm), _in_cols(layer, IN_OFFS[5], WIDTH), _in_cols(layer, IN_OFFS[6], M2_CONV_CH),
                  _of_layer(wdt_groups, layer), _of_layer(conv_w, layer), _full(conv_b.shape),
                  _full(dtb_groups.shape), _full(al_groups.shape), _full(expand.shape), _full(d_lanes.shape),
                  _full(norm_w.shape), _of_layer(w_sc, layer), _of_layer(sc_conv_w, layer)],
        out_specs=(_seq_spec(rows, WIDTH), _seq_spec(rows, WIDTH)),
        out_shape=(out, out),
        scratch_shapes=[pltpu.VMEM((SUBLANES, M2_CONV_CH), F32),
                        pltpu.VMEM((M2_GROUPS, M2_STATE, WIDTH // M2_GROUPS), F32),
                        pltpu.VMEM((SUBLANES, WIDTH), F32)],
        compiler_params=_params("parallel", "arbitrary"),
        name="ssd_sc_branch",
    )(h, win, win, wdt_groups, conv_w, conv_b, dtb_groups, al_groups, expand, d_lanes, norm_w, w_sc, sc_conv_w)


def _merge_kernel(x_ref, h_ref, ya_ref, yb_ref, yc_ref, yd_ref, wm_ref, mb_ref, wbr_ref, wo_ref, nw_ref,
                  o_ref, *, final, rows):
    d = D_MODEL
    part = rows // SPLIT
    for sp in range(SPLIT):
        r = slice(sp * part, (sp + 1) * part)
        h = h_ref[r, :]
        merged = None
        for k, y_ref in enumerate((ya_ref, yb_ref, yc_ref, yd_ref)):
            gate = _sigmoid(_dot(h, wm_ref[:, k * d:(k + 1) * d]) + mb_ref[k:k + 1, :])
            term = gate * _dot(y_ref[r, :], wbr_ref[k])
            merged = term if merged is None else merged + term
        xn = x_ref[r, :] + _dot(merged.astype(BF16), wo_ref[...])
        o_ref[r, :] = _rms(xn, nw_ref[...]) if final else xn


def _merge(x2d, h2d, ys, wm, mb, wbr, wo, final_norm_w, layer, final):
    t, d = x2d.shape
    tile = lambda cols: pl.BlockSpec((MERGE_ROWS, cols), lambda i: (i, 0))
    return pl.pallas_call(
        functools.partial(_merge_kernel, final=final, rows=MERGE_ROWS),
        grid=(t // MERGE_ROWS,),
        in_specs=[tile(d), tile(d)] + [tile(WIDTH)] * N_BRANCH
                 + [_of_layer(wm, layer), _of_layer(mb, layer), _of_layer(wbr, layer), _of_layer(wo, layer),
                    _full((1, d))],
        out_specs=tile(d),
        out_shape=jax.ShapeDtypeStruct((t, d), F32),
        compiler_params=_params("parallel"),
        name="merge",
    )(x2d, h2d, *ys, wm, mb, wbr, wo, final_norm_w.reshape(1, d))


def kernel(x, norm_w, w_in, s5_lambda_re, s5_lambda_im, s5_b_re, s5_b_im, s5_c_re, s5_c_im, s5_d,
           s5_log_step, s5_w_glu, sgu_ln_w, sgu_ln_b, sgu_w, sgu_b, m2_conv_w, m2_conv_b, m2_dt_bias,
           m2_a_log, m2_d, m2_norm_w, sc_conv_w, merge_b, w_branch, w_out, final_norm_w):
    b, l, d = x.shape
    t = b * l
    nl = w_in.shape[0]

    w_in_t = jnp.swapaxes(w_in, 1, 2)
    win = _in_proj_cols(w_in_t, 0, IN_OFFS[7])
    w_sc = _in_proj_cols(w_in_t, IN_OFFS[8], IN_OFFS[12] - IN_OFFS[8])
    w_merge = _in_proj_cols(w_in_t, IN_OFFS[12], IN_OFFS[13] - IN_OFFS[12])
    used = SSD_LANE_GROUPS * M2_HEADS
    lane_groups = lambda v: jnp.pad(jnp.tile(v, (1,) * (v.ndim - 1) + (SSD_LANE_GROUPS,)),
                                    ((0, 0),) * (v.ndim - 1) + ((0, LANES - used),))
    wdt_groups = lane_groups(w_in[:, :, IN_OFFS[7]:IN_OFFS[8]].astype(BF16))
    dtb_groups = lane_groups(m2_dt_bias)
    al_groups = lane_groups(m2_a_log)
    d_lanes = jnp.repeat(m2_d, M2_HEAD_DIM, axis=-1)
    expand = _ssd_expand_matrix()
    sgu_bias = jnp.repeat(sgu_b.transpose(0, 2, 1), SGU_HEAD_DIM, axis=-1)
    ab_re, ab_im, s5_wb, s5_cc = _s5_prep(s5_lambda_re, s5_lambda_im, s5_b_re, s5_b_im, s5_c_re, s5_c_im,
                                          s5_log_step)
    s5_dd = s5_d.reshape(nl, WIDTH)
    wglu = s5_w_glu.astype(BF16)
    wbr = w_branch.astype(BF16)
    wo = w_out.astype(BF16)

    x2d = x.reshape(t, d)
    for i in range(DEPTH):
        ya, h = _s5_branch(x2d.reshape(b, l, d), norm_w, win, s5_wb, s5_cc, s5_dd, wglu, ab_re, ab_im, i)
        yb = _sgu_branch(h, win, sgu_ln_w, sgu_ln_b, sgu_w, sgu_bias, i)
        yc, yd = _ssd_sc_branch(h, win, wdt_groups, m2_conv_w, m2_conv_b, dtb_groups, al_groups, expand, d_lanes,
                                m2_norm_w, w_sc, sc_conv_w, i)
        x2d = _merge(x2d, h.reshape(t, d), [y.reshape(t, WIDTH) for y in (ya, yb, yc, yd)], w_merge, merge_b,
                     wbr, wo, final_norm_w, i, i == DEPTH - 1)
    return x2d.reshape(b, l, d)
```

```python
import functools

import jax
import jax.numpy as jnp
import numpy as np
from jax import lax
from jax.experimental import pallas as pl
from jax.experimental.pallas import tpu as pltpu

F32 = jnp.float32
BF16 = jnp.bfloat16

D_MODEL = 1024
DEPTH = 2
WIDTH = 512
N_BRANCH = 4
EPS = 1e-6
S5_GROUP = 16
S5_GROUPS = WIDTH // S5_GROUP
S5_STATE = 64
S5_PAIRS = S5_GROUPS * S5_STATE
SGU_CHUNK = 128
SGU_HEADS = 8
SGU_HEAD_DIM = WIDTH // SGU_HEADS
M2_HEAD_DIM = 64
M2_HEADS = WIDTH // M2_HEAD_DIM
M2_GROUPS = 2
M2_STATE = 128
M2_CONV = 4
M2_CHUNK = 128
M2_CONV_CH = WIDTH + 2 * M2_GROUPS * M2_STATE
SC_CONV = 3

IN_SIZES = (WIDTH, WIDTH, WIDTH, WIDTH, WIDTH, WIDTH, M2_CONV_CH, M2_HEADS,
            WIDTH, WIDTH, WIDTH, WIDTH, N_BRANCH * D_MODEL)
IN_OFFS = tuple(int(v) for v in np.cumsum((0,) + IN_SIZES))

SUBLANES = 8
LANES = 128
MXU_DIM = 256
VMEM_LIMIT_BYTES = 56 * 1024 * 1024

S5_STEPS = 128
SGU_ROWS = 1024
SSD_ROWS = 1024
MERGE_ROWS = 1024
SPLIT = 2


def _dot(a, b):
    return jnp.dot(a, b, preferred_element_type=F32)


def _dot_nt(a, b):
    return lax.dot_general(a, b, (((1,), (1,)), ((), ())), preferred_element_type=F32)


def _dot_tn(a, b):
    return lax.dot_general(a, b, (((0,), (0,)), ((), ())), preferred_element_type=F32)


def _sigmoid(x):
    return 0.5 * (jnp.tanh(0.5 * x) + 1.0)


def _silu(x):
    hx = 0.5 * x
    return hx * jnp.tanh(hx) + hx


def _gelu(x):
    c = np.float32(np.sqrt(2.0 / np.pi))
    return x * (0.5 * (1.0 + jnp.tanh(c * (x + 0.044715 * (x * x * x)))))


def _softplus(x):
    return jnp.maximum(x, 0.0) + jnp.log1p(jnp.exp(-jnp.abs(x)))


def _rms(x, w):
    return x * lax.rsqrt(jnp.mean(x * x, axis=-1, keepdims=True) + EPS) * w


def _shift_rows(cat, j):
    n8, c = cat.shape
    rot = pltpu.roll(cat.reshape(n8 // SUBLANES, SUBLANES, c), j, 1)
    sub = lax.broadcasted_iota(jnp.int32, (1, SUBLANES, c), 1)
    out = jnp.where(sub >= j, rot[1:], rot[:-1])
    return out.reshape(n8 - SUBLANES, c)


def _params(*semantics):
    return pltpu.CompilerParams(dimension_semantics=semantics, vmem_limit_bytes=VMEM_LIMIT_BYTES)


RESIDENT = pl.Buffered(1)


def _full(shape):
    zeros = (0,) * len(shape)
    return pl.BlockSpec(tuple(shape), lambda *_: zeros, pipeline_mode=RESIDENT)


def _of_layer(arr, layer):
    zeros = (0,) * (arr.ndim - 1)
    return pl.BlockSpec((None,) + tuple(arr.shape[1:]), lambda *_: (layer,) + zeros, pipeline_mode=RESIDENT)


def _in_cols(layer, first, width):
    assert first % width == 0
    return pl.BlockSpec((None, D_MODEL, width), lambda *_: (layer, 0, first // width), pipeline_mode=RESIDENT)


def _seq_spec(rows, cols):
    return pl.BlockSpec((None, rows, cols), lambda b, c: (b, c, 0))


WPREP_COLS = 1024


def _wprep_kernel(wt_ref, o_ref):
    o_ref[...] = wt_ref[0].T.astype(o_ref.dtype)


def _in_proj_cols(w_in_t, first, ncols):
    nl, _, d = w_in_t.shape
    assert ncols % WPREP_COLS == 0 and first % SUBLANES == 0
    return pl.pallas_call(
        _wprep_kernel,
        grid=(nl, ncols // WPREP_COLS),
        in_specs=[pl.BlockSpec((pl.Element(1), pl.Element(WPREP_COLS), pl.Element(d)),
                               lambda l, j: (l, pl.multiple_of(first + j * WPREP_COLS, SUBLANES), 0))],
        out_specs=pl.BlockSpec((None, d, WPREP_COLS), lambda l, j: (l, 0, j)),
        out_shape=jax.ShapeDtypeStruct((nl, d, ncols), BF16),
        compiler_params=_params("parallel", "parallel"),
        name="w_in_cols",
    )(w_in_t)


def _s5_prep_kernel(lr_ref, li_ref, ls_ref, bre_ref, bim_ref, cre_ref, cim_ref,
                    abre_ref, abim_ref, wb_ref, cc_ref):
    gh = S5_GROUPS // 2
    rows_gp, cols_gn = gh * S5_GROUP, gh * S5_STATE
    own_group = (lax.broadcasted_iota(jnp.int32, (rows_gp, cols_gn), 0) // S5_GROUP
                 == lax.broadcasted_iota(jnp.int32, (rows_gp, cols_gn), 1) // S5_STATE)

    def block_diag(per_p):
        tiled = jnp.broadcast_to(per_p[None], (gh,) + per_p.shape).reshape(rows_gp, cols_gn)
        return jnp.where(own_group, tiled, 0.0)

    lr, li = lr_ref[...], li_ref[...]
    step = jnp.exp(ls_ref[...])
    mag = jnp.exp(lr * step)
    ang = li * step
    ab_re, ab_im = mag * jnp.cos(ang), mag * jnp.sin(ang)
    abre_ref[...] = ab_re
    abim_ref[...] = ab_im
    den = lr * lr + li * li
    nr = ab_re - 1.0
    coef_re = (nr * lr + ab_im * li) / den
    coef_im = (ab_im * lr - nr * li) / den
    for layer in range(lr.shape[0]):
        for hf in range(2):
            cols = slice(hf * cols_gn, (hf + 1) * cols_gn)
            cr, ci = coef_re[layer:layer + 1, cols], coef_im[layer:layer + 1, cols]
            br, bi = bre_ref[layer, :, cols], bim_ref[layer, :, cols]
            wb_ref[layer, hf, :, 0:cols_gn] = block_diag(cr * br - ci * bi).astype(BF16)
            wb_ref[layer, hf, :, cols_gn:2 * cols_gn] = block_diag(cr * bi + ci * br).astype(BF16)
            cc_ref[layer, hf, 0:cols_gn, :] = block_diag(cre_ref[layer, :, cols]).T.astype(BF16)
            cc_ref[layer, hf, cols_gn:2 * cols_gn, :] = (-block_diag(cim_ref[layer, :, cols])).T.astype(BF16)


def _s5_prep(lam_re, lam_im, b_re, b_im, c_re, c_im, log_step):
    nl = lam_re.shape[0]
    g, n, p = S5_GROUPS, S5_STATE, S5_GROUP
    ls = jnp.repeat(log_step, n, axis=-1)
    row = lambda v: v.reshape(nl, g * n)
    b_chan = lambda v: v.transpose(0, 3, 1, 2).reshape(nl, p, g * n)
    c_chan = lambda v: v.transpose(0, 2, 1, 3).reshape(nl, p, g * n)
    gh = g // 2
    return pl.pallas_call(
        _s5_prep_kernel,
        out_shape=(jax.ShapeDtypeStruct((nl, g * n), F32), jax.ShapeDtypeStruct((nl, g * n), F32),
                   jax.ShapeDtypeStruct((nl, 2, gh * p, 2 * gh * n), BF16),
                   jax.ShapeDtypeStruct((nl, 2, 2 * gh * n, gh * p), BF16)),
        name="s5_prep",
    )(row(lam_re), row(lam_im), row(ls), b_chan(b_re), b_chan(b_im), c_chan(c_re), c_chan(c_im))


S5_HALF = S5_PAIRS
S5_PERM = MXU_DIM


def _perm_time_major(nb):
    tt = S5_PERM // nb
    i = lax.broadcasted_iota(jnp.int32, (S5_PERM, S5_PERM), 0)
    j = lax.broadcasted_iota(jnp.int32, (S5_PERM, S5_PERM), 1)
    return jnp.where(j == (i % nb) * tt + i // nb, 1.0, 0.0).astype(BF16)


def _s5_kernel(x_ref, nw_ref, wug_ref, wb_ref, cc_ref, d_ref, wglu_ref, abre_ref, abim_ref, o_ref, h_ref,
               hp_ref, ug_ref, st_ref, carry_ref, *, steps, nb, layer):
    @pl.when(pl.program_id(0) == 0)
    def _():
        carry_ref[...] = jnp.zeros(carry_ref.shape, F32)

    tt = S5_PERM // nb
    perm = _perm_time_major(nb)
    for k in range(steps // tt):
        hk = _rms(x_ref[:, k * tt:(k + 1) * tt, :], nw_ref[layer:layer + 1, :]).astype(BF16)
        h_ref[:, k * tt:(k + 1) * tt, :] = hk
        blk = hk.reshape(S5_PERM, D_MODEL)
        hp_ref[k * S5_PERM:(k + 1) * S5_PERM, :] = _dot(perm, blk).astype(BF16)

    ug_ref[:, 0:WIDTH] = _dot(hp_ref[...], wug_ref[:, 0:WIDTH])
    ub = ug_ref[:, 0:WIDTH].astype(BF16)
    hw = WIDTH // 2
    for hf in range(2):
        st_ref[:, hf * S5_HALF:(hf + 1) * S5_HALF] = _dot(ub[:, hf * hw:(hf + 1) * hw], wb_ref[hf])
    ug_ref[:, WIDTH:2 * WIDTH] = _dot(hp_ref[...], wug_ref[:, WIDTH:2 * WIDTH])

    pairs_half = S5_HALF // 2
    for hf in range(2):
        cre = hf * S5_HALF
        cim = cre + pairs_half
        tc = hf * pairs_half
        ar = jnp.broadcast_to(abre_ref[layer:layer + 1, tc:tc + pairs_half], (nb, pairs_half))
        ai = jnp.broadcast_to(abim_ref[layer:layer + 1, tc:tc + pairs_half], (nb, pairs_half))

        def step(t, carry, cre=cre, cim=cim, ar=ar, ai=ai):
            sr, si = carry
            r0 = pl.multiple_of(t * nb, nb)
            nr = (ar * sr - ai * si) + st_ref[pl.ds(r0, nb), cre:cre + pairs_half]
            ni = (ar * si + ai * sr) + st_ref[pl.ds(r0, nb), cim:cim + pairs_half]
            st_ref[pl.ds(r0, nb), cre:cre + pairs_half] = nr
            st_ref[pl.ds(r0, nb), cim:cim + pairs_half] = ni
            return nr, ni

        sr, si = lax.fori_loop(
            0, steps, step,
            (carry_ref[:, cre:cre + pairs_half], carry_ref[:, cim:cim + pairs_half]), unroll=True)
        carry_ref[:, cre:cre + pairs_half] = sr
        carry_ref[:, cim:cim + pairs_half] = si

    ys = []
    for hf in range(2):
        ys.append(_dot(st_ref[:, hf * S5_HALF:(hf + 1) * S5_HALF].astype(BF16), cc_ref[hf]))
    y = jnp.concatenate(ys, axis=-1) + d_ref[layer:layer + 1, :] * ug_ref[:, 0:WIDTH]
    y = _gelu(y)
    y = y * _sigmoid(_dot(y.astype(BF16), wglu_ref[...]))
    y = (y * _silu(ug_ref[:, WIDTH:2 * WIDTH])).astype(BF16)
    for k in range(steps // tt):
        back = _dot_tn(perm, y[k * S5_PERM:(k + 1) * S5_PERM, :])
        o_ref[:, k * tt:(k + 1) * tt, :] = back.reshape(nb, tt, WIDTH).astype(o_ref.dtype)


def _s5_branch(x, norm_w, win, wb, cc, d, wglu, ab_re, ab_im, layer):
    b, l, dm = x.shape
    assert S5_PERM % b == 0 and b % SUBLANES == 0
    steps = min(S5_STEPS, l)
    rows = steps * b
    tile = lambda cols: pl.BlockSpec((b, steps, cols), lambda c: (0, c, 0))
    return pl.pallas_call(
        functools.partial(_s5_kernel, steps=steps, nb=b, layer=layer),
        grid=(l // steps,),
        in_specs=[tile(dm), _full(norm_w.shape), _in_cols(layer, IN_OFFS[0], 2 * WIDTH),
                  _of_layer(wb, layer), _of_layer(cc, layer), _full(d.shape), _of_layer(wglu, layer),
                  _full(ab_re.shape), _full(ab_im.shape)],
        out_specs=(tile(WIDTH), tile(dm)),
        out_shape=(jax.ShapeDtypeStruct((b, l, WIDTH), BF16), jax.ShapeDtypeStruct((b, l, dm), BF16)),
        scratch_shapes=[pltpu.VMEM((rows, dm), BF16),
                        pltpu.VMEM((rows, 2 * WIDTH), F32),
                        pltpu.VMEM((rows, 2 * S5_HALF), F32),
                        pltpu.VMEM((b, 2 * S5_HALF), F32)],
        compiler_params=_params("arbitrary"),
        name="s5_branch",
    )(x, norm_w, win, wb, cc, d, wglu, ab_re, ab_im)


def _sgu_kernel(h_ref, wu_ref, wv_ref, wg_ref, lnw_ref, lnb_ref, ws_ref, bias_ref, o_ref, *, rows, layer):
    t = SGU_CHUNK
    causal = lax.broadcasted_iota(jnp.int32, (t, t), 0) >= lax.broadcasted_iota(jnp.int32, (t, t), 1)
    first_head = lax.broadcasted_iota(jnp.int32, (t, LANES), 1) < SGU_HEAD_DIM
    wm = [jnp.where(causal, ws_ref[hd], 0.0).astype(BF16) for hd in range(SGU_HEADS)]
    zero = jnp.zeros((), BF16)
    part = rows // SPLIT
    for sp in range(SPLIT):
        h = h_ref[sp * part:(sp + 1) * part, :]
        v = _gelu(_dot(h, wv_ref[...]))
        mu = jnp.mean(v, axis=-1, keepdims=True)
        vc = v - mu
        var = jnp.mean(vc * vc, axis=-1, keepdims=True)
        vn = (vc * lax.rsqrt(var + EPS) * lnw_ref[layer:layer + 1, :] + lnb_ref[layer:layer + 1, :]).astype(BF16)
        ug = _gelu(_dot(h, wu_ref[...])) * _silu(_dot(h, wg_ref[...]))
        for c in range(part // t):
            r = slice(c * t, (c + 1) * t)
            outs = []
            for j in range(WIDTH // LANES):
                blk = vn[r, j * LANES:(j + 1) * LANES]
                outs.append(_dot(wm[2 * j], jnp.where(first_head, blk, zero))
                            + _dot(wm[2 * j + 1], jnp.where(first_head, zero, blk)))
            s = jnp.concatenate(outs, axis=-1) + bias_ref[...]
            o_ref[sp * part + c * t:sp * part + (c + 1) * t, :] = (ug[r] * s).astype(o_ref.dtype)


def _sgu_branch(h, win, ln_w, ln_b, w_s, bias, layer):
    b, l, dm = h.shape
    rows = min(SGU_ROWS, l)
    return pl.pallas_call(
        functools.partial(_sgu_kernel, rows=rows, layer=layer),
        grid=(b, l // rows),
        in_specs=[_seq_spec(rows, dm)] + [_in_cols(layer, IN_OFFS[k], WIDTH) for k in (2, 3, 4)]
                 + [_full(ln_w.shape), _full(ln_b.shape), _of_layer(w_s, layer), _of_layer(bias, layer)],
        out_specs=_seq_spec(rows, WIDTH),
        out_shape=jax.ShapeDtypeStruct((b, l, WIDTH), BF16),
        compiler_params=_params("parallel", "parallel"),
        name="sgu_branch",
    )(h, win, win, win, ln_w, ln_b, w_s, bias)


SSD_QUANT = 3
SSD_PARTS = 3
SSD_LANE_GROUPS = SSD_QUANT * SSD_PARTS


def _ssd_expand_matrix():
    e = np.zeros((LANES, SSD_QUANT * WIDTH), np.float32)
    for grp in range(SSD_LANE_GROUPS):
        for hd in range(M2_HEADS):
            c0 = (grp // SSD_PARTS) * WIDTH + hd * M2_HEAD_DIM
            e[grp * M2_HEADS + hd, c0:c0 + M2_HEAD_DIM] = 1.0
    return jnp.asarray(e, BF16)


def _cumsum_rows(x):
    q = x.shape[0]
    row = lax.broadcasted_iota(jnp.int32, x.shape, 0)
    dist = 1
    while dist < q:
        x = x + jnp.where(row >= dist, pltpu.roll(x, dist, 0), 0.0)
        dist *= 2
    return x


def _ssd_sc_kernel(h_ref, wz_ref, wxbc_ref, wdt_ref, cw_ref, cb_ref, dtb_ref, al_ref, e_ref, d_ref, nw_ref,
                   wsc_ref, scw_ref, o_ref, osc_ref, prev_ref, st_ref, scprev_ref, *, rows, layer):
    @pl.when(pl.program_id(1) == 0)
    def _():
        prev_ref[...] = jnp.zeros(prev_ref.shape, F32)
        st_ref[...] = jnp.zeros(st_ref.shape, F32)
        scprev_ref[...] = jnp.zeros(scprev_ref.shape, F32)

    sc_proj = lambda j: _dot(h_ref[...], wsc_ref[:, j * WIDTH:(j + 1) * WIDTH])

    rpart = rows // SPLIT
    hs = [h_ref[sp * rpart:(sp + 1) * rpart, :] for sp in range(SPLIT)]
    raws = [_dot(hh, wxbc_ref[...]) for hh in hs]
    zs = [_dot(hh, wz_ref[...]) for hh in hs]
    dtrs = [_dot(hh, wdt_ref[...]) for hh in hs]
    sc_v = sc_proj(1) * sc_proj(2)
    prev = prev_ref[...]
    xbcs, dts, das = [], [], []
    neg_a = -jnp.exp(al_ref[layer:layer + 1, :])
    for sp in range(SPLIT):
        raw = raws[sp]
        cat = jnp.concatenate([prev, raw], axis=0)
        acc = cb_ref[layer:layer + 1, :] + cw_ref[M2_CONV - 1:M2_CONV, :] * raw
        for j in range(1, M2_CONV):
            k = M2_CONV - 1 - j
            acc = acc + cw_ref[k:k + 1, :] * _shift_rows(cat, j)
        prev = raw[rpart - SUBLANES:rpart, :]
        xbcs.append(_silu(acc))
        dts.append(_softplus(dtrs[sp] + dtb_ref[layer:layer + 1, :]))
        das.append(dts[sp] * neg_a)
    prev_ref[...] = prev
    grp = lax.broadcasted_iota(jnp.int32, (1, LANES), 1) // M2_HEADS
    quant = grp // SSD_PARTS
    part = grp % SSD_PARTS

    q = M2_CHUNK
    hd_w = M2_HEAD_DIM
    causal = lax.broadcasted_iota(jnp.int32, (q, q), 0) >= lax.broadcasted_iota(jnp.int32, (q, q), 1)
    first_head = lax.broadcasted_iota(jnp.int32, (q, LANES), 1) < hd_w
    zero = jnp.zeros((), BF16)
    gw = WIDTH // M2_GROUPS

    def chunk(s):
        sp = (s * q) // rpart
        xbc, z, dt, da = xbcs[sp], zs[sp], dts[sp], das[sp]
        x = xbc[:, 0:WIDTH]
        r = slice(s * q - sp * rpart, (s + 1) * q - sp * rpart)
        cs = _cumsum_rows(da[r])
        cs_row = cs.T[0:M2_HEADS, :]
        cs_last = cs[q - 1:q, :]
        val = jnp.where(quant == 0, dt[r], jnp.where(quant == 1, dt[r] * jnp.exp(cs_last - cs), jnp.exp(cs)))
        p0 = val.astype(BF16)
        r1 = val - p0.astype(F32)
        p1 = r1.astype(BF16)
        p2 = (r1 - p1.astype(F32)).astype(BF16)
        spread = _dot(jnp.where(part == 0, p0, jnp.where(part == 1, p1, p2)), e_ref[...])
        dt_e = spread[:, 0:WIDTH]
        dtdec_e = spread[:, WIDTH:2 * WIDTH]
        exp_cs = spread[:, 2 * WIDTH:3 * WIDTH]
        exp_last = exp_cs[q - 1:q, :]
        x_c = x[r]
        xdt_b = (x_c * dt_e).astype(BF16)
        xdec = (x_c * dtdec_e).astype(BF16)
        ys = []
        for g in range(M2_GROUPS):
            bo = WIDTH + g * M2_STATE
            co = WIDTH + M2_GROUPS * M2_STATE + g * M2_STATE
            bg = xbc[r, bo:bo + M2_STATE].astype(BF16)
            cg = xbc[r, co:co + M2_STATE].astype(BF16)
            scores = _dot_nt(cg, bg)
            state = st_ref[g]
            y_g = _dot(cg, state.astype(BF16)) * exp_cs[:, g * gw:(g + 1) * gw]
            diag = []
            for jp in range(gw // LANES):
                blk = xdt_b[:, g * gw + jp * LANES:g * gw + (jp + 1) * LANES]
                acc2 = None
                for qq in range(2):
                    hd = g * (M2_HEADS // M2_GROUPS) + jp * 2 + qq
                    seg = cs[:, hd:hd + 1] - cs_row[hd:hd + 1, :]
                    decay = jnp.exp(jnp.where(causal, seg, -jnp.inf))
                    m = (scores * decay).astype(BF16)
                    half = jnp.where(first_head, blk, zero) if qq == 0 else jnp.where(first_head, zero, blk)
                    term = _dot(m, half)
                    acc2 = term if acc2 is None else acc2 + term
                diag.append(acc2)
            ys.append(y_g + jnp.concatenate(diag, axis=-1))
            st_ref[g] = state * exp_last[:, g * gw:(g + 1) * gw] + _dot_tn(bg, xdec[:, g * gw:(g + 1) * gw])
        y = jnp.concatenate(ys, axis=-1) + d_ref[layer:layer + 1, :] * x_c
        y = y * _silu(z[r])
        o_ref[s * q:(s + 1) * q, :] = _rms(y, nw_ref[layer:layer + 1, :]).astype(o_ref.dtype)

    def sc_tail(sc_b, sc_gate):
        cat = jnp.concatenate([scprev_ref[...], sc_v], axis=0)
        acc = scw_ref[SC_CONV - 1:SC_CONV, :] * sc_v
        for j in range(1, SC_CONV):
            k = SC_CONV - 1 - j
            acc = acc + scw_ref[k:k + 1, :] * _shift_rows(cat, j)
        scprev_ref[...] = sc_v[rows - SUBLANES:rows, :]
        osc_ref[...] = (sc_b * acc * _silu(sc_gate)).astype(osc_ref.dtype)

    n_chunks = rows // q
    sc_b = sc_gate = None
    for s in range(n_chunks):
        chunk(s)
        if s == 0:
            sc_b = sc_proj(0)
        if s == min(1, n_chunks - 1):
            sc_gate = sc_proj(3)
        if s == min(2, n_chunks - 1):
            sc_tail(sc_b, sc_gate)


def _ssd_sc_branch(h, win, wdt_groups, conv_w, conv_b, dtb_groups, al_groups, expand, d_lanes, norm_w,
                   w_sc, sc_conv_w, layer):
    b, l, dm = h.shape
    rows = min(SSD_ROWS, l)
    out = jax.ShapeDtypeStruct((b, l, WIDTH), BF16)
    return pl.pallas_call(
        functools.partial(_ssd_sc_kernel, rows=rows, layer=layer),
        grid=(b, l // rows),
        in_specs=[_seq_spec(rows, dm), _in_cols(layer, IN_OFFS[5], WIDTH), _in_cols(layer, IN_OFFS[6], M2_CONV_CH),
                  _of_layer(wdt_groups, layer), _of_layer(conv_w, layer), _full(conv_b.shape),
                  _full(dtb_groups.shape), _full(al_groups.shape), _full(expand.shape), _full(d_lanes.shape),
                  _full(norm_w.shape), _of_layer(w_sc, layer), _of_layer(sc_conv_w, layer)],
        out_specs=(_seq_spec(rows, WIDTH), _seq_spec(rows, WIDTH)),
        out_shape=(out, out),
        scratch_shapes=[pltpu.VMEM((SUBLANES, M2_CONV_CH), F32),
                        pltpu.VMEM((M2_GROUPS, M2_STATE, WIDTH // M2_GROUPS), F32),
                        pltpu.VMEM((SUBLANES, WIDTH), F32)],
        compiler_params=_params("parallel", "arbitrary"),
        name="ssd_sc_branch",
    )(h, win, win, wdt_groups, conv_w, conv_b, dtb_groups, al_groups, expand, d_lanes, norm_w, w_sc, sc_conv_w)


def _merge_kernel(x_ref, h_ref, ya_ref, yb_ref, yc_ref, yd_ref, wm_ref, mb_ref, wbr_ref, wo_ref, nw_ref,
                  o_ref, *, final, rows):
    d = D_MODEL
    part = rows // SPLIT
    for sp in range(SPLIT):
        r = slice(sp * part, (sp + 1) * part)
        h = h_ref[r, :]
        merged = None
        for k, y_ref in enumerate((ya_ref, yb_ref, yc_ref, yd_ref)):
            gate = _sigmoid(_dot(h, wm_ref[:, k * d:(k + 1) * d]) + mb_ref[k:k + 1, :])
            term = gate * _dot(y_ref[r, :], wbr_ref[k])
            merged = term if merged is None else merged + term
        xn = x_ref[r, :] + _dot(merged.astype(BF16), wo_ref[...])
        o_ref[r, :] = _rms(xn, nw_ref[...]) if final else xn


def _merge(x2d, h2d, ys, wm, mb, wbr, wo, final_norm_w, layer, final):
    t, d = x2d.shape
    tile = lambda cols: pl.BlockSpec((MERGE_ROWS, cols), lambda i: (i, 0))
    return pl.pallas_call(
        functools.partial(_merge_kernel, final=final, rows=MERGE_ROWS),
        grid=(t // MERGE_ROWS,),
        in_specs=[tile(d), tile(d)] + [tile(WIDTH)] * N_BRANCH
                 + [_of_layer(wm, layer), _of_layer(mb, layer), _of_layer(wbr, layer), _of_layer(wo, layer),
                    _full((1, d))],
        out_specs=tile(d),
        out_shape=jax.ShapeDtypeStruct((t, d), F32),
        compiler_params=_params("parallel"),
        name="merge",
    )(x2d, h2d, *ys, wm, mb, wbr, wo, final_norm_w.reshape(1, d))


def kernel(x, norm_w, w_in, s5_lambda_re, s5_lambda_im, s5_b_re, s5_b_im, s5_c_re, s5_c_im, s5_d,
           s5_log_step, s5_w_glu, sgu_ln_w, sgu_ln_b, sgu_w, sgu_b, m2_conv_w, m2_conv_b, m2_dt_bias,
           m2_a_log, m2_d, m2_norm_w, sc_conv_w, merge_b, w_branch, w_out, final_norm_w):
    b, l, d = x.shape
    t = b * l
    nl = w_in.shape[0]

    w_in_t = jnp.swapaxes(w_in, 1, 2)
    win = _in_proj_cols(w_in_t, 0, IN_OFFS[7])
    w_sc = _in_proj_cols(w_in_t, IN_OFFS[8], IN_OFFS[12] - IN_OFFS[8])
    w_merge = _in_proj_cols(w_in_t, IN_OFFS[12], IN_OFFS[13] - IN_OFFS[12])
    used = SSD_LANE_GROUPS * M2_HEADS
    lane_groups = lambda v: jnp.pad(jnp.tile(v, (1,) * (v.ndim - 1) + (SSD_LANE_GROUPS,)),
                                    ((0, 0),) * (v.ndim - 1) + ((0, LANES - used),))
    wdt_groups = lane_groups(w_in[:, :, IN_OFFS[7]:IN_OFFS[8]].astype(BF16))
    dtb_groups = lane_groups(m2_dt_bias)
    al_groups = lane_groups(m2_a_log)
    d_lanes = jnp.repeat(m2_d, M2_HEAD_DIM, axis=-1)
    expand = _ssd_expand_matrix()
    sgu_bias = jnp.repeat(sgu_b.transpose(0, 2, 1), SGU_HEAD_DIM, axis=-1)
    ab_re, ab_im, s5_wb, s5_cc = _s5_prep(s5_lambda_re, s5_lambda_im, s5_b_re, s5_b_im, s5_c_re, s5_c_im,
                                          s5_log_step)
    s5_dd = s5_d.reshape(nl, WIDTH)
    wglu = s5_w_glu.astype(BF16)
    wbr = w_branch.astype(BF16)
    wo = w_out.astype(BF16)

    x2d = x.reshape(t, d)
    for i in range(DEPTH):
        ya, h = _s5_branch(x2d.reshape(b, l, d), norm_w, win, s5_wb, s5_cc, s5_dd, wglu, ab_re, ab_im, i)
        yb = _sgu_branch(h, win, sgu_ln_w, sgu_ln_b, sgu_w, sgu_bias, i)
        yc, yd = _ssd_sc_branch(h, win, wdt_groups, m2_conv_w, m2_conv_b, dtb_groups, al_groups, expand, d_lanes,
                                m2_norm_w, w_sc, sc_conv_w, i)
        x2d = _merge(x2d, h.reshape(t, d), [y.reshape(t, WIDTH) for y in (ya, yb, yc, yd)], w_merge, merge_b,
                     wbr, wo, final_norm_w, i, i == DEPTH - 1)
    return x2d.reshape(b, l, d)
```

```python
import functools

import jax
import jax.numpy as jnp
import numpy as np
from jax import lax
from jax.experimental import pallas as pl
from jax.experimental.pallas import tpu as pltpu

F32 = jnp.float32
BF16 = jnp.bfloat16

D_MODEL = 1024
DEPTH = 2
WIDTH = 512
N_BRANCH = 4
EPS = 1e-6
S5_GROUP = 16
S5_GROUPS = WIDTH // S5_GROUP
S5_STATE = 64
S5_PAIRS = S5_GROUPS * S5_STATE
SGU_CHUNK = 128
SGU_HEADS = 8
SGU_HEAD_DIM = WIDTH // SGU_HEADS
M2_HEAD_DIM = 64
M2_HEADS = WIDTH // M2_HEAD_DIM
M2_GROUPS = 2
M2_STATE = 128
M2_CONV = 4
M2_CHUNK = 128
M2_CONV_CH = WIDTH + 2 * M2_GROUPS * M2_STATE
SC_CONV = 3

IN_SIZES = (WIDTH, WIDTH, WIDTH, WIDTH, WIDTH, WIDTH, M2_CONV_CH, M2_HEADS,
            WIDTH, WIDTH, WIDTH, WIDTH, N_BRANCH * D_MODEL)
IN_OFFS = tuple(int(v) for v in np.cumsum((0,) + IN_SIZES))

SUBLANES = 8
LANES = 128
MXU_DIM = 256
VMEM_LIMIT_BYTES = 56 * 1024 * 1024

S5_STEPS = 128
SGU_ROWS = 1024
SSD_ROWS = 1024
MERGE_ROWS = 1024
SPLIT = 2


def _dot(a, b):
    return jnp.dot(a, b, preferred_element_type=F32)


def _dot_nt(a, b):
    return lax.dot_general(a, b, (((1,), (1,)), ((), ())), preferred_element_type=F32)


def _dot_tn(a, b):
    return lax.dot_general(a, b, (((0,), (0,)), ((), ())), preferred_element_type=F32)


def _sigmoid(x):
    return 0.5 * (jnp.tanh(0.5 * x) + 1.0)


def _silu(x):
    hx = 0.5 * x
    return hx * jnp.tanh(hx) + hx


def _gelu(x):
    c = np.float32(np.sqrt(2.0 / np.pi))
    return x * (0.5 * (1.0 + jnp.tanh(c * (x + 0.044715 * (x * x * x)))))


def _softplus(x):
    return jnp.maximum(x, 0.0) + jnp.log1p(jnp.exp(-jnp.abs(x)))


def _rms(x, w):
    return x * lax.rsqrt(jnp.mean(x * x, axis=-1, keepdims=True) + EPS) * w


def _shift_rows(cat, j):
    n8, c = cat.shape
    rot = pltpu.roll(cat.reshape(n8 // SUBLANES, SUBLANES, c), j, 1)
    sub = lax.broadcasted_iota(jnp.int32, (1, SUBLANES, c), 1)
    out = jnp.where(sub >= j, rot[1:], rot[:-1])
    return out.reshape(n8 - SUBLANES, c)


def _params(*semantics):
    return pltpu.CompilerParams(dimension_semantics=semantics, vmem_limit_bytes=VMEM_LIMIT_BYTES)


RESIDENT = pl.Buffered(1)


def _full(shape):
    zeros = (0,) * len(shape)
    return pl.BlockSpec(tuple(shape), lambda *_: zeros, pipeline_mode=RESIDENT)


def _of_layer(arr, layer):
    zeros = (0,) * (arr.ndim - 1)
    return pl.BlockSpec((None,) + tuple(arr.shape[1:]), lambda *_: (layer,) + zeros, pipeline_mode=RESIDENT)


def _in_cols(layer, first, width):
    assert first % width == 0
    return pl.BlockSpec((None, D_MODEL, width), lambda *_: (layer, 0, first // width), pipeline_mode=RESIDENT)


def _seq_spec(rows, cols):
    return pl.BlockSpec((None, rows, cols), lambda b, c: (b, c, 0))


WPREP_COLS = 2048


def _wprep_kernel(wt_ref, o_ref):
    o_ref[...] = wt_ref[0].T.astype(o_ref.dtype)


def _in_proj_cols(w_in_t, first, ncols):
    nl, _, d = w_in_t.shape
    assert ncols % WPREP_COLS == 0 and first % SUBLANES == 0
    return pl.pallas_call(
        _wprep_kernel,
        grid=(nl, ncols // WPREP_COLS),
        in_specs=[pl.BlockSpec((pl.Element(1), pl.Element(WPREP_COLS), pl.Element(d)),
                               lambda l, j: (l, pl.multiple_of(first + j * WPREP_COLS, SUBLANES), 0))],
        out_specs=pl.BlockSpec((None, d, WPREP_COLS), lambda l, j: (l, 0, j)),
        out_shape=jax.ShapeDtypeStruct((nl, d, ncols), BF16),
        compiler_params=_params("parallel", "parallel"),
        name="w_in_cols",
    )(w_in_t)


def _s5_prep_kernel(lr_ref, li_ref, ls_ref, bre_ref, bim_ref, cre_ref, cim_ref,
                    abre_ref, abim_ref, wb_ref, cc_ref):
    gh = S5_GROUPS // 2
    rows_gp, cols_gn = gh * S5_GROUP, gh * S5_STATE
    own_group = (lax.broadcasted_iota(jnp.int32, (rows_gp, cols_gn), 0) // S5_GROUP
                 == lax.broadcasted_iota(jnp.int32, (rows_gp, cols_gn), 1) // S5_STATE)

    def block_diag(per_p):
        tiled = jnp.broadcast_to(per_p[None], (gh,) + per_p.shape).reshape(rows_gp, cols_gn)
        return jnp.where(own_group, tiled, 0.0)

    lr, li = lr_ref[...], li_ref[...]
    step = jnp.exp(ls_ref[...])
    mag = jnp.exp(lr * step)
    ang = li * step
    ab_re, ab_im = mag * jnp.cos(ang), mag * jnp.sin(ang)
    abre_ref[...] = ab_re
    abim_ref[...] = ab_im
    den = lr * lr + li * li
    nr = ab_re - 1.0
    coef_re = (nr * lr + ab_im * li) / den
    coef_im = (ab_im * lr - nr * li) / den
    for layer in range(lr.shape[0]):
        for hf in range(2):
            cols = slice(hf * cols_gn, (hf + 1) * cols_gn)
            cr, ci = coef_re[layer:layer + 1, cols], coef_im[layer:layer + 1, cols]
            br, bi = bre_ref[layer, :, cols], bim_ref[layer, :, cols]
            wb_ref[layer, hf, :, 0:cols_gn] = block_diag(cr * br - ci * bi).astype(BF16)
            wb_ref[layer, hf, :, cols_gn:2 * cols_gn] = block_diag(cr * bi + ci * br).astype(BF16)
            cc_ref[layer, hf, 0:cols_gn, :] = block_diag(cre_ref[layer, :, cols]).T.astype(BF16)
            cc_ref[layer, hf, cols_gn:2 * cols_gn, :] = (-block_diag(cim_ref[layer, :, cols])).T.astype(BF16)


def _s5_prep(lam_re, lam_im, b_re, b_im, c_re, c_im, log_step):
    nl = lam_re.shape[0]
    g, n, p = S5_GROUPS, S5_STATE, S5_GROUP
    ls = jnp.repeat(log_step, n, axis=-1)
    row = lambda v: v.reshape(nl, g * n)
    b_chan = lambda v: v.transpose(0, 3, 1, 2).reshape(nl, p, g * n)
    c_chan = lambda v: v.transpose(0, 2, 1, 3).reshape(nl, p, g * n)
    gh = g // 2
    return pl.pallas_call(
        _s5_prep_kernel,
        out_shape=(jax.ShapeDtypeStruct((nl, g * n), F32), jax.ShapeDtypeStruct((nl, g * n), F32),
                   jax.ShapeDtypeStruct((nl, 2, gh * p, 2 * gh * n), BF16),
                   jax.ShapeDtypeStruct((nl, 2, 2 * gh * n, gh * p), BF16)),
        name="s5_prep",
    )(row(lam_re), row(lam_im), row(ls), b_chan(b_re), b_chan(b_im), c_chan(c_re), c_chan(c_im))


S5_HALF = S5_PAIRS
S5_PERM = MXU_DIM


def _perm_time_major(nb):
    tt = S5_PERM // nb
    i = lax.broadcasted_iota(jnp.int32, (S5_PERM, S5_PERM), 0)
    j = lax.broadcasted_iota(jnp.int32, (S5_PERM, S5_PERM), 1)
    return jnp.where(j == (i % nb) * tt + i // nb, 1.0, 0.0).astype(BF16)


def _s5_kernel(x_ref, nw_ref, wug_ref, wb_ref, cc_ref, d_ref, wglu_ref, abre_ref, abim_ref, o_ref, h_ref,
               hp_ref, ug_ref, st_ref, carry_ref, *, steps, nb, layer):
    @pl.when(pl.program_id(0) == 0)
    def _():
        carry_ref[...] = jnp.zeros(carry_ref.shape, F32)

    tt = S5_PERM // nb
    perm = _perm_time_major(nb)
    for k in range(steps // tt):
        hk = _rms(x_ref[:, k * tt:(k + 1) * tt, :], nw_ref[layer:layer + 1, :]).astype(BF16)
        h_ref[:, k * tt:(k + 1) * tt, :] = hk
        blk = hk.reshape(S5_PERM, D_MODEL)
        hp_ref[k * S5_PERM:(k + 1) * S5_PERM, :] = _dot(perm, blk).astype(BF16)

    ug_ref[:, 0:WIDTH] = _dot(hp_ref[...], wug_ref[:, 0:WIDTH])
    ub = ug_ref[:, 0:WIDTH].astype(BF16)
    hw = WIDTH // 2
    for hf in range(2):
        st_ref[:, hf * S5_HALF:(hf + 1) * S5_HALF] = _dot(ub[:, hf * hw:(hf + 1) * hw], wb_ref[hf])
    ug_ref[:, WIDTH:2 * WIDTH] = _dot(hp_ref[...], wug_ref[:, WIDTH:2 * WIDTH])

    pairs_half = S5_HALF // 2
    for hf in range(2):
        cre = hf * S5_HALF
        cim = cre + pairs_half
        tc = hf * pairs_half
        ar = jnp.broadcast_to(abre_ref[layer:layer + 1, tc:tc + pairs_half], (nb, pairs_half))
        ai = jnp.broadcast_to(abim_ref[layer:layer + 1, tc:tc + pairs_half], (nb, pairs_half))

        def step(t, carry, cre=cre, cim=cim, ar=ar, ai=ai):
            sr, si = carry
            r0 = pl.multiple_of(t * nb, nb)
            nr = (ar * sr - ai * si) + st_ref[pl.ds(r0, nb), cre:cre + pairs_half]
            ni = (ar * si + ai * sr) + st_ref[pl.ds(r0, nb), cim:cim + pairs_half]
            st_ref[pl.ds(r0, nb), cre:cre + pairs_half] = nr
            st_ref[pl.ds(r0, nb), cim:cim + pairs_half] = ni
            return nr, ni

        sr, si = lax.fori_loop(
            0, steps, step,
            (carry_ref[:, cre:cre + pairs_half], carry_ref[:, cim:cim + pairs_half]), unroll=True)
        carry_ref[:, cre:cre + pairs_half] = sr
        carry_ref[:, cim:cim + pairs_half] = si

    ys = []
    for hf in range(2):
        ys.append(_dot(st_ref[:, hf * S5_HALF:(hf + 1) * S5_HALF].astype(BF16), cc_ref[hf]))
    y = jnp.concatenate(ys, axis=-1) + d_ref[layer:layer + 1, :] * ug_ref[:, 0:WIDTH]
    y = _gelu(y)
    y = y * _sigmoid(_dot(y.astype(BF16), wglu_ref[...].astype(BF16)))
    y = (y * _silu(ug_ref[:, WIDTH:2 * WIDTH])).astype(BF16)
    for k in range(steps // tt):
        back = _dot_tn(perm, y[k * S5_PERM:(k + 1) * S5_PERM, :])
        o_ref[:, k * tt:(k + 1) * tt, :] = back.reshape(nb, tt, WIDTH).astype(o_ref.dtype)


def _s5_branch(x, norm_w, win, wb, cc, d, wglu, ab_re, ab_im, layer):
    b, l, dm = x.shape
    assert S5_PERM % b == 0 and b % SUBLANES == 0
    steps = min(S5_STEPS, l)
    rows = steps * b
    tile = lambda cols: pl.BlockSpec((b, steps, cols), lambda c: (0, c, 0))
    return pl.pallas_call(
        functools.partial(_s5_kernel, steps=steps, nb=b, layer=layer),
        grid=(l // steps,),
        in_specs=[tile(dm), _full(norm_w.shape), _in_cols(layer, IN_OFFS[0], 2 * WIDTH),
                  _of_layer(wb, layer), _of_layer(cc, layer), _full(d.shape), _of_layer(wglu, layer),
                  _full(ab_re.shape), _full(ab_im.shape)],
        out_specs=(tile(WIDTH), tile(dm)),
        out_shape=(jax.ShapeDtypeStruct((b, l, WIDTH), BF16), jax.ShapeDtypeStruct((b, l, dm), BF16)),
        scratch_shapes=[pltpu.VMEM((rows, dm), BF16),
                        pltpu.VMEM((rows, 2 * WIDTH), F32),
                        pltpu.VMEM((rows, 2 * S5_HALF), F32),
                        pltpu.VMEM((b, 2 * S5_HALF), F32)],
        compiler_params=_params("arbitrary"),
        name="s5_branch",
    )(x, norm_w, win, wb, cc, d, wglu, ab_re, ab_im)


def _sgu_kernel(h_ref, wu_ref, wv_ref, wg_ref, lnw_ref, lnb_ref, ws_ref, bias_ref, o_ref, *, rows, layer):
    t = SGU_CHUNK
    causal = lax.broadcasted_iota(jnp.int32, (t, t), 0) >= lax.broadcasted_iota(jnp.int32, (t, t), 1)
    first_head = lax.broadcasted_iota(jnp.int32, (t, LANES), 1) < SGU_HEAD_DIM
    wm = [jnp.where(causal, ws_ref[hd], 0.0).astype(BF16) for hd in range(SGU_HEADS)]
    zero = jnp.zeros((), BF16)
    part = rows // SPLIT
    for sp in range(SPLIT):
        h = h_ref[sp * part:(sp + 1) * part, :]
        v = _gelu(_dot(h, wv_ref[...]))
        mu = jnp.mean(v, axis=-1, keepdims=True)
        vc = v - mu
        var = jnp.mean(vc * vc, axis=-1, keepdims=True)
        vn = (vc * lax.rsqrt(var + EPS) * lnw_ref[layer:layer + 1, :] + lnb_ref[layer:layer + 1, :]).astype(BF16)
        ug = _gelu(_dot(h, wu_ref[...])) * _silu(_dot(h, wg_ref[...]))
        for c in range(part // t):
            r = slice(c * t, (c + 1) * t)
            outs = []
            for j in range(WIDTH // LANES):
                blk = vn[r, j * LANES:(j + 1) * LANES]
                outs.append(_dot(wm[2 * j], jnp.where(first_head, blk, zero))
                            + _dot(wm[2 * j + 1], jnp.where(first_head, zero, blk)))
            s = jnp.concatenate(outs, axis=-1) + bias_ref[...]
            o_ref[sp * part + c * t:sp * part + (c + 1) * t, :] = (ug[r] * s).astype(o_ref.dtype)


def _sgu_branch(h, win, ln_w, ln_b, w_s, bias, layer):
    b, l, dm = h.shape
    rows = min(SGU_ROWS, l)
    return pl.pallas_call(
        functools.partial(_sgu_kernel, rows=rows, layer=layer),
        grid=(b, l // rows),
        in_specs=[_seq_spec(rows, dm)] + [_in_cols(layer, IN_OFFS[k], WIDTH) for k in (2, 3, 4)]
                 + [_full(ln_w.shape), _full(ln_b.shape), _of_layer(w_s, layer), _of_layer(bias, layer)],
        out_specs=_seq_spec(rows, WIDTH),
        out_shape=jax.ShapeDtypeStruct((b, l, WIDTH), BF16),
        compiler_params=_params("parallel", "parallel"),
        name="sgu_branch",
    )(h, win, win, win, ln_w, ln_b, w_s, bias)


SSD_QUANT = 3
SSD_PARTS = 3
SSD_LANE_GROUPS = SSD_QUANT * SSD_PARTS


def _ssd_expand_matrix():
    e = np.zeros((LANES, SSD_QUANT * WIDTH), np.float32)
    for grp in range(SSD_LANE_GROUPS):
        for hd in range(M2_HEADS):
            c0 = (grp // SSD_PARTS) * WIDTH + hd * M2_HEAD_DIM
            e[grp * M2_HEADS + hd, c0:c0 + M2_HEAD_DIM] = 1.0
    return jnp.asarray(e, BF16)


def _cumsum_rows(x):
    q = x.shape[0]
    row = lax.broadcasted_iota(jnp.int32, x.shape, 0)
    dist = 1
    while dist < q:
        x = x + jnp.where(row >= dist, pltpu.roll(x, dist, 0), 0.0)
        dist *= 2
    return x


def _ssd_sc_kernel(h_ref, wz_ref, wxbc_ref, wdt_ref, cw_ref, cb_ref, dtb_ref, al_ref, e_ref, d_ref, nw_ref,
                   wsc_ref, scw_ref, o_ref, osc_ref, prev_ref, st_ref, scprev_ref, *, rows, layer):
    @pl.when(pl.program_id(1) == 0)
    def _():
        prev_ref[...] = jnp.zeros(prev_ref.shape, F32)
        st_ref[...] = jnp.zeros(st_ref.shape, F32)
        scprev_ref[...] = jnp.zeros(scprev_ref.shape, F32)

    sc_proj = lambda j: _dot(h_ref[...], wsc_ref[:, j * WIDTH:(j + 1) * WIDTH])

    rpart = rows // SPLIT
    hs = [h_ref[sp * rpart:(sp + 1) * rpart, :] for sp in range(SPLIT)]
    raws = [_dot(hh, wxbc_ref[...]) for hh in hs]
    zs = [_dot(hh, wz_ref[...]) for hh in hs]
    dtrs = [_dot(hh, wdt_ref[...]) for hh in hs]
    sc_v = sc_proj(1) * sc_proj(2)
    prev = prev_ref[...]
    xbcs, dts, das = [], [], []
    neg_a = -jnp.exp(al_ref[layer:layer + 1, :])
    for sp in range(SPLIT):
        raw = raws[sp]
        cat = jnp.concatenate([prev, raw], axis=0)
        acc = cb_ref[layer:layer + 1, :] + cw_ref[M2_CONV - 1:M2_CONV, :] * raw
        for j in range(1, M2_CONV):
            k = M2_CONV - 1 - j
            acc = acc + cw_ref[k:k + 1, :] * _shift_rows(cat, j)
        prev = raw[rpart - SUBLANES:rpart, :]
        xbcs.append(_silu(acc))
        dts.append(_softplus(dtrs[sp] + dtb_ref[layer:layer + 1, :]))
        das.append(dts[sp] * neg_a)
    prev_ref[...] = prev
    grp = lax.broadcasted_iota(jnp.int32, (1, LANES), 1) // M2_HEADS
    quant = grp // SSD_PARTS
    part = grp % SSD_PARTS

    q = M2_CHUNK
    hd_w = M2_HEAD_DIM
    causal = lax.broadcasted_iota(jnp.int32, (q, q), 0) >= lax.broadcasted_iota(jnp.int32, (q, q), 1)
    first_head = lax.broadcasted_iota(jnp.int32, (q, LANES), 1) < hd_w
    zero = jnp.zeros((), BF16)
    gw = WIDTH // M2_GROUPS

    def chunk(s):
        sp = (s * q) // rpart
        xbc, z, dt, da = xbcs[sp], zs[sp], dts[sp], das[sp]
        x = xbc[:, 0:WIDTH]
        r = slice(s * q - sp * rpart, (s + 1) * q - sp * rpart)
        cs = _cumsum_rows(da[r])
        cs_row = cs.T[0:M2_HEADS, :]
        cs_last = cs[q - 1:q, :]
        val = jnp.where(quant == 0, dt[r], jnp.where(quant == 1, dt[r] * jnp.exp(cs_last - cs), jnp.exp(cs)))
        p0 = val.astype(BF16)
        r1 = val - p0.astype(F32)
        p1 = r1.astype(BF16)
        p2 = (r1 - p1.astype(F32)).astype(BF16)
        spread = _dot(jnp.where(part == 0, p0, jnp.where(part == 1, p1, p2)), e_ref[...])
        dt_e = spread[:, 0:WIDTH]
        dtdec_e = spread[:, WIDTH:2 * WIDTH]
        exp_cs = spread[:, 2 * WIDTH:3 * WIDTH]
        exp_last = exp_cs[q - 1:q, :]
        x_c = x[r]
        xdt_b = (x_c * dt_e).astype(BF16)
        xdec = (x_c * dtdec_e).astype(BF16)
        ys = []
        for g in range(M2_GROUPS):
            bo = WIDTH + g * M2_STATE
            co = WIDTH + M2_GROUPS * M2_STATE + g * M2_STATE
            bg = xbc[r, bo:bo + M2_STATE].astype(BF16)
            cg = xbc[r, co:co + M2_STATE].astype(BF16)
            scores = _dot_nt(cg, bg)
            state = st_ref[g]
            y_g = _dot(cg, state.astype(BF16)) * exp_cs[:, g * gw:(g + 1) * gw]
            diag = []
            for jp in range(gw // LANES):
                blk = xdt_b[:, g * gw + jp * LANES:g * gw + (jp + 1) * LANES]
                acc2 = None
                for qq in range(2):
                    hd = g * (M2_HEADS // M2_GROUPS) + jp * 2 + qq
                    seg = cs[:, hd:hd + 1] - cs_row[hd:hd + 1, :]
                    decay = jnp.exp(jnp.where(causal, seg, -jnp.inf))
                    m = (scores * decay).astype(BF16)
                    half = jnp.where(first_head, blk, zero) if qq == 0 else jnp.where(first_head, zero, blk)
                    term = _dot(m, half)
                    acc2 = term if acc2 is None else acc2 + term
                diag.append(acc2)
            ys.append(y_g + jnp.concatenate(diag, axis=-1))
            st_ref[g] = state * exp_last[:, g * gw:(g + 1) * gw] + _dot_tn(bg, xdec[:, g * gw:(g + 1) * gw])
        y = jnp.concatenate(ys, axis=-1) + d_ref[layer:layer + 1, :] * x_c
        y = y * _silu(z[r])
        o_ref[s * q:(s + 1) * q, :] = _rms(y, nw_ref[layer:layer + 1, :]).astype(o_ref.dtype)

    def sc_tail(sc_b, sc_gate):
        cat = jnp.concatenate([scprev_ref[...], sc_v], axis=0)
        acc = scw_ref[SC_CONV - 1:SC_CONV, :] * sc_v
        for j in range(1, SC_CONV):
            k = SC_CONV - 1 - j
            acc = acc + scw_ref[k:k + 1, :] * _shift_rows(cat, j)
        scprev_ref[...] = sc_v[rows - SUBLANES:rows, :]
        osc_ref[...] = (sc_b * acc * _silu(sc_gate)).astype(osc_ref.dtype)

    n_chunks = rows // q
    sc_b = sc_gate = None
    for s in range(n_chunks):
        chunk(s)
        if s == 0:
            sc_b = sc_proj(0)
        if s == min(1, n_chunks - 1):
            sc_gate = sc_proj(3)
        if s == min(2, n_chunks - 1):
            sc_tail(sc_b, sc_gate)


def _ssd_sc_branch(h, win, wdt_groups, conv_w, conv_b, dtb_groups, al_groups, expand, d_lanes, norm_w,
                   w_sc, sc_conv_w, layer):
    b, l, dm = h.shape
    rows = min(SSD_ROWS, l)
    out = jax.ShapeDtypeStruct((b, l, WIDTH), BF16)
    return pl.pallas_call(
        functools.partial(_ssd_sc_kernel, rows=rows, layer=layer),
        grid=(b, l // rows),
        in_specs=[_seq_spec(rows, dm), _in_cols(layer, IN_OFFS[5], WIDTH), _in_cols(layer, IN_OFFS[6], M2_CONV_CH),
                  _of_layer(wdt_groups, layer), _of_layer(conv_w, layer), _full(conv_b.shape),
                  _full(dtb_groups.shape), _full(al_groups.shape), _full(expand.shape), _full(d_lanes.shape),
                  _full(norm_w.shape), _of_layer(w_sc, layer), _of_layer(sc_conv_w, layer)],
        out_specs=(_seq_spec(rows, WIDTH), _seq_spec(rows, WIDTH)),
        out_shape=(out, out),
        scratch_shapes=[pltpu.VMEM((SUBLANES, M2_CONV_CH), F32),
                        pltpu.VMEM((M2_GROUPS, M2_STATE, WIDTH // M2_GROUPS), F32),
                        pltpu.VMEM((SUBLANES, WIDTH), F32)],
        compiler_params=_params("parallel", "arbitrary"),
        name="ssd_sc_branch",
    )(h, win, win, wdt_groups, conv_w, conv_b, dtb_groups, al_groups, expand, d_lanes, norm_w, w_sc, sc_conv_w)


def _merge_kernel(x_ref, h_ref, ya_ref, yb_ref, yc_ref, yd_ref, wm_ref, mb_ref, wbr_ref, wo_ref, nw_ref,
                  o_ref, *, final, rows):
    d = D_MODEL
    part = rows // SPLIT
    for sp in range(SPLIT):
        r = slice(sp * part, (sp + 1) * part)
        h = h_ref[r, :]
        merged = None
        for k, y_ref in enumerate((ya_ref, yb_ref, yc_ref, yd_ref)):
            gate = _sigmoid(_dot(h, wm_ref[:, k * d:(k + 1) * d]) + mb_ref[k:k + 1, :])
            term = gate * _dot(y_ref[r, :], wbr_ref[k])
            merged = term if merged is None else merged + term
        xn = x_ref[r, :] + _dot(merged.astype(BF16), wo_ref[...])
        o_ref[r, :] = _rms(xn, nw_ref[...]) if final else xn


def _merge(x2d, h2d, ys, wm, mb, wbr, wo, final_norm_w, layer, final):
    t, d = x2d.shape
    tile = lambda cols: pl.BlockSpec((MERGE_ROWS, cols), lambda i: (i, 0))
    return pl.pallas_call(
        functools.partial(_merge_kernel, final=final, rows=MERGE_ROWS),
        grid=(t // MERGE_ROWS,),
        in_specs=[tile(d), tile(d)] + [tile(WIDTH)] * N_BRANCH
                 + [_of_layer(wm, layer), _of_layer(mb, layer), _of_layer(wbr, layer), _of_layer(wo, layer),
                    _full((1, d))],
        out_specs=tile(d),
        out_shape=jax.ShapeDtypeStruct((t, d), F32),
        compiler_params=_params("parallel"),
        name="merge",
    )(x2d, h2d, *ys, wm, mb, wbr, wo, final_norm_w.reshape(1, d))


def kernel(x, norm_w, w_in, s5_lambda_re, s5_lambda_im, s5_b_re, s5_b_im, s5_c_re, s5_c_im, s5_d,
           s5_log_step, s5_w_glu, sgu_ln_w, sgu_ln_b, sgu_w, sgu_b, m2_conv_w, m2_conv_b, m2_dt_bias,
           m2_a_log, m2_d, m2_norm_w, sc_conv_w, merge_b, w_branch, w_out, final_norm_w):
    b, l, d = x.shape
    t = b * l
    nl = w_in.shape[0]

    w_in_t = jnp.swapaxes(w_in, 1, 2)
    win = _in_proj_cols(w_in_t, 0, IN_OFFS[7])
    w_sc = _in_proj_cols(w_in_t, IN_OFFS[8], IN_OFFS[12] - IN_OFFS[8])
    w_merge = _in_proj_cols(w_in_t, IN_OFFS[12], IN_OFFS[13] - IN_OFFS[12])
    used = SSD_LANE_GROUPS * M2_HEADS
    lane_groups = lambda v: jnp.pad(jnp.tile(v, (1,) * (v.ndim - 1) + (SSD_LANE_GROUPS,)),
                                    ((0, 0),) * (v.ndim - 1) + ((0, LANES - used),))
    wdt_groups = lane_groups(w_in[:, :, IN_OFFS[7]:IN_OFFS[8]].astype(BF16))
    dtb_groups = lane_groups(m2_dt_bias)
    al_groups = lane_groups(m2_a_log)
    d_lanes = jnp.repeat(m2_d, M2_HEAD_DIM, axis=-1)
    expand = _ssd_expand_matrix()
    sgu_bias = jnp.repeat(sgu_b.transpose(0, 2, 1), SGU_HEAD_DIM, axis=-1)
    ab_re, ab_im, s5_wb, s5_cc = _s5_prep(s5_lambda_re, s5_lambda_im, s5_b_re, s5_b_im, s5_c_re, s5_c_im,
                                          s5_log_step)
    s5_dd = s5_d.reshape(nl, WIDTH)
    wbr = w_branch.astype(BF16)
    wo = w_out.astype(BF16)

    x2d = x.reshape(t, d)
    for i in range(DEPTH):
        ya, h = _s5_branch(x2d.reshape(b, l, d), norm_w, win, s5_wb, s5_cc, s5_dd, s5_w_glu, ab_re, ab_im, i)
        yb = _sgu_branch(h, win, sgu_ln_w, sgu_ln_b, sgu_w, sgu_bias, i)
        yc, yd = _ssd_sc_branch(h, win, wdt_groups, m2_conv_w, m2_conv_b, dtb_groups, al_groups, expand, d_lanes,
                                m2_norm_w, w_sc, sc_conv_w, i)
        x2d = _merge(x2d, h.reshape(t, d), [y.reshape(t, WIDTH) for y in (ya, yb, yc, yd)], w_merge, merge_b,
                     wbr, wo, final_norm_w, i, i == DEPTH - 1)
    return x2d.reshape(b, l, d)
```
